```python
import math
import jax, jax.numpy as jnp
from jax import lax
import numpy as np

D_MODEL = 1024
BATCH = 8
SEQ = 8192
DEPTH = 2

PLE_DIM = 256
N_MIXERS = 2
N_MAMBA = (DEPTH + 1) // 2
N_GDN = DEPTH // 2
CONV_K = 4
EPS = 1e-6
M_INNER = 2 * D_MODEL
M_HEADDIM = 64
M_HEADS = M_INNER // M_HEADDIM
M_GROUPS = 4
M_HPG = M_HEADS // M_GROUPS
M_STATE = 128
M_CHUNK = 128
M_CONV_DIM = M_INNER + 2 * M_GROUPS * M_STATE
M_IN_DIM = M_INNER + M_CONV_DIM + M_HEADS
G_HEAD_DIM = 128
G_QK_HEADS = D_MODEL // G_HEAD_DIM
G_V_HEADS = 2 * G_QK_HEADS
G_QK_DIM = G_QK_HEADS * G_HEAD_DIM
G_V_DIM = G_V_HEADS * G_HEAD_DIM
G_CONV_DIM = 2 * G_QK_DIM + G_V_DIM
G_IN_DIM = G_CONV_DIM + G_V_DIM + 2 * G_V_HEADS
G_CHUNK = 64
N_EGROUPS = 8
EXPERTS_PER_GROUP = 8
N_EXPERTS = N_EGROUPS * EXPERTS_PER_GROUP
TOP_K = 2
D_EXPERT = D_MODEL // 2
MOE_BLOCK = 256

kernel_name = 'hybrid_ssd_gdn_hmoe_ple'


def rmsnorm(x, w):
    xf = x.astype(jnp.float32)
    y = xf * lax.rsqrt(jnp.mean(xf * xf, axis=-1, keepdims=True) + EPS)
    return (y * w.astype(jnp.float32)).astype(x.dtype)


def l2norm(x):
    xf = x.astype(jnp.float32)
    return (xf * lax.rsqrt(jnp.sum(xf * xf, axis=-1, keepdims=True) + EPS)).astype(x.dtype)


def causal_dwconv(x, w):
    k, c = w.shape
    return lax.conv_general_dilated(x, w[:, None, :], window_strides=(1,), padding=[(k - 1, 0)],
                                    dimension_numbers=('NWC', 'WIO', 'NWC'), feature_group_count=c)


def seq_chunks(t, c):
    b, l = t.shape[:2]
    return jnp.moveaxis(t.reshape(b, l // c, c, *t.shape[2:]), 1, 0)


def seq_unchunk(t):
    t = jnp.moveaxis(t, 0, 1)
    return t.reshape(t.shape[0], t.shape[1] * t.shape[2], *t.shape[3:])


def ssd_scan(x, dt, A, Bm, Cm):
    b = x.shape[0]
    causal = jnp.tril(jnp.ones((M_CHUNK, M_CHUNK), bool))[None, :, :, None, None]

    def step(S, xs):
        xc, dtc, Bc, Cc = xs
        a = jnp.cumsum(dtc * A, axis=1)
        seg = jnp.exp(jnp.where(causal, a[:, :, None] - a[:, None, :], -jnp.inf))
        cb = jnp.einsum('btgn,bsgn->btsg', Cc, Bc)
        w = cb[..., None] * seg * dtc[:, None]
        y = jnp.einsum('btsgj,bsgjp->btgjp', w, xc)
        y = y + jnp.einsum('btgn,bgjpn->btgjp', Cc, S) * jnp.exp(a)[..., None]
        a_last = a[:, -1]
        to_end = jnp.exp(a_last[:, None] - a) * dtc
        S = S * jnp.exp(a_last)[..., None, None] + jnp.einsum('bsgn,bsgj,bsgjp->bgjpn', Bc, to_end, xc)
        return S, y

    S0 = jnp.zeros((b, M_GROUPS, M_HPG, M_HEADDIM, M_STATE), x.dtype)
    xs = (seq_chunks(x, M_CHUNK), seq_chunks(dt, M_CHUNK), seq_chunks(Bm, M_CHUNK), seq_chunks(Cm, M_CHUNK))
    _, ys = lax.scan(step, S0, xs)
    return seq_unchunk(ys)


def mamba2_mixer(u, in_w, conv_w, conv_b, dt_bias, A_log, D_skip, norm_w, out_w):
    b, l, _ = u.shape
    proj = u @ in_w
    z = proj[..., :M_INNER]
    xbc = proj[..., M_INNER:M_INNER + M_CONV_DIM]
    dt = proj[..., M_INNER + M_CONV_DIM:]
    xbc = jax.nn.silu(causal_dwconv(xbc, conv_w) + conv_b)
    xs = xbc[..., :M_INNER].reshape(b, l, M_GROUPS, M_HPG, M_HEADDIM)
    Bm = xbc[..., M_INNER:M_INNER + M_GROUPS * M_STATE].reshape(b, l, M_GROUPS, M_STATE)
    Cm = xbc[..., M_INNER + M_GROUPS * M_STATE:].reshape(b, l, M_GROUPS, M_STATE)
    dt = jax.nn.softplus(dt + dt_bias).reshape(b, l, M_GROUPS, M_HPG)
    A = -jnp.exp(A_log).reshape(M_GROUPS, M_HPG)
    y = ssd_scan(xs, dt, A, Bm, Cm) + D_skip.reshape(M_GROUPS, M_HPG)[..., None] * xs
    y = y.reshape(b, l, M_INNER)
    y = rmsnorm(y * jax.nn.silu(z), norm_w)
    return y @ out_w


def gated_delta_scan(q, k, v, g, beta):
    b, _, h, dk = q.shape
    dv = v.shape[-1]
    causal = jnp.tril(jnp.ones((G_CHUNK, G_CHUNK), bool))
    strict = jnp.tril(jnp.ones((G_CHUNK, G_CHUNK), bool), -1)

    def step(S, xs):
        qc, kc, vc, gc, bc = [jnp.swapaxes(t, 1, 2) for t in xs]
        gcs = jnp.cumsum(gc, axis=-1)
        decay = jnp.exp(jnp.where(causal, gcs[..., :, None] - gcs[..., None, :], -jnp.inf))
        kb = kc * bc[..., None]
        a_kk = jnp.where(strict, jnp.einsum('bhtk,bhsk->bhts', kb, kc) * decay, 0.0)
        rhs = jnp.concatenate([vc * bc[..., None], kb * jnp.exp(gcs)[..., None]], axis=-1)
        sol = lax.linalg.triangular_solve(a_kk.astype(jnp.float32), rhs.astype(jnp.float32),
                                          left_side=True, lower=True, unit_diagonal=True).astype(vc.dtype)
        u_c, w_c = sol[..., :dv], sol[..., dv:]
        v_new = u_c - jnp.einsum('bhtk,bhkv->bhtv', w_c, S)
        a_qk = jnp.einsum('bhtk,bhsk->bhts', qc, kc) * decay
        o = (jnp.einsum('bhtk,bhkv->bhtv', qc * jnp.exp(gcs)[..., None], S)
             + jnp.einsum('bhts,bhsv->bhtv', a_qk, v_new))
        g_last = gcs[..., -1:]
        S = S * jnp.exp(g_last)[..., None] + jnp.einsum('bhtk,bhtv->bhkv', kc * jnp.exp(g_last - gcs)[..., None], v_new)
        return S, jnp.swapaxes(o, 1, 2)

    S0 = jnp.zeros((b, h, dk, dv), v.dtype)
    xs = (seq_chunks(q, G_CHUNK), seq_chunks(k, G_CHUNK), seq_chunks(v, G_CHUNK),
          seq_chunks(g, G_CHUNK), seq_chunks(beta, G_CHUNK))
    _, ys = lax.scan(step, S0, xs)
    return seq_unchunk(ys)


def gated_deltanet_mixer(u, in_w, conv_w, dt_bias, A_log, norm_w, out_w):
    b, l, _ = u.shape
    proj = u @ in_w
    qkv = jax.nn.silu(causal_dwconv(proj[..., :G_CONV_DIM], conv_w))
    z = proj[..., G_CONV_DIM:G_CONV_DIM + G_V_DIM].reshape(b, l, G_V_HEADS, G_HEAD_DIM)
    b_logit = proj[..., G_CONV_DIM + G_V_DIM:G_CONV_DIM + G_V_DIM + G_V_HEADS]
    a_logit = proj[..., G_CONV_DIM + G_V_DIM + G_V_HEADS:]
    rep = G_V_HEADS // G_QK_HEADS
    q = l2norm(qkv[..., :G_QK_DIM].reshape(b, l, G_QK_HEADS, G_HEAD_DIM))
    k = l2norm(qkv[..., G_QK_DIM:2 * G_QK_DIM].reshape(b, l, G_QK_HEADS, G_HEAD_DIM))
    q = jnp.repeat(q, rep, axis=2) * (G_HEAD_DIM ** -0.5)
    k = jnp.repeat(k, rep, axis=2)
    v = qkv[..., 2 * G_QK_DIM:].reshape(b, l, G_V_HEADS, G_HEAD_DIM)
    beta = jax.nn.sigmoid(b_logit)
    g = -jnp.exp(A_log) * jax.nn.softplus(a_logit + dt_bias)
    o = gated_delta_scan(q, k, v, g, beta)
    o = rmsnorm(o, norm_w) * jax.nn.silu(z)
    return o.reshape(b, l, G_V_DIM) @ out_w


def hierarchical_moe(x, w_group, b_group, w_expert, b_expert, w1, w3, w2):
    b, l, d = x.shape
    t = b * l
    xf = x.reshape(t, d)
    g_logits = (xf @ w_group + b_group).astype(jnp.float32)
    g_prob = jax.nn.softmax(g_logits, axis=-1)
    g_sel = jnp.argmax(g_logits, axis=-1).astype(jnp.int32)
    p_sel = jnp.take_along_axis(g_prob, g_sel[:, None], axis=-1)
    e_logits = (xf @ w_expert + b_expert).astype(jnp.float32).reshape(t, N_EGROUPS, EXPERTS_PER_GROUP)
    e_in = jnp.take_along_axis(e_logits, g_sel[:, None, None], axis=1)[:, 0]
    top_v, top_i = lax.top_k(e_in, TOP_K)
    gate = (jax.nn.softmax(top_v, axis=-1) * p_sel).astype(x.dtype)
    eid = g_sel[:, None] * EXPERTS_PER_GROUP + top_i.astype(jnp.int32)
    na = t * TOP_K
    e_flat = eid.reshape(na)
    tok = jnp.repeat(jnp.arange(t, dtype=jnp.int32), TOP_K)
    order = jnp.argsort(e_flat)
    e_s, tok_s, gate_s = e_flat[order], tok[order], gate.reshape(na)[order]
    counts = jnp.bincount(e_flat, length=N_EXPERTS)
    start = jnp.cumsum(counts) - counts
    padded = (counts + MOE_BLOCK - 1) // MOE_BLOCK * MOE_BLOCK
    pstart = jnp.cumsum(padded) - padded
    pend = pstart + padded
    pos = pstart[e_s] + (jnp.arange(na, dtype=jnp.int32) - start[e_s])
    n_pad = (na + MOE_BLOCK - 1) // MOE_BLOCK * MOE_BLOCK + N_EXPERTS * MOE_BLOCK
    nb = n_pad // MOE_BLOCK
    slot_tok = jnp.full((n_pad,), t, jnp.int32).at[pos].set(tok_s)
    slot_gate = jnp.zeros((n_pad,), x.dtype).at[pos].set(gate_s)
    blk_start = jnp.arange(nb, dtype=jnp.int32) * MOE_BLOCK
    blk_expert = jnp.minimum(jnp.sum(blk_start[:, None] >= pend[None, :], axis=-1), N_EXPERTS - 1)
    x_pad = jnp.concatenate([xf, jnp.zeros((1, d), xf.dtype)], axis=0)
    xb = x_pad[slot_tok].reshape(nb, MOE_BLOCK, d)

    def expert_block(args):
        xblk, e = args
        hid = jax.nn.silu(xblk @ w1[e]) * (xblk @ w3[e])
        return hid @ w2[e]

    yb = lax.map(expert_block, (xb, blk_expert)).reshape(n_pad, d)
    out = jnp.zeros((t + 1, d), x.dtype).at[slot_tok].add(yb * slot_gate[:, None])
    return out[:t].reshape(b, l, d)


def setup_inputs(seed: int = 0) -> dict:
    key = jax.random.key(seed)
    ks = iter(jax.random.split(key, 48))

    def nrm(shape, scale):
        return jax.random.normal(next(ks), shape, jnp.float32) * scale

    def gain(shape):
        return 1.0 + nrm(shape, 0.02)

    def dt_bias(shape):
        dt = jnp.exp(jax.random.uniform(next(ks), shape, jnp.float32, math.log(1e-3), math.log(1e-1)))
        return dt + jnp.log(-jnp.expm1(-dt))

    def a_log(shape):
        return jnp.log(jax.random.uniform(next(ks), shape, jnp.float32, 1.0, 16.0))

    return {
        'x': nrm((BATCH, SEQ, D_MODEL), 1.0),
        'p': nrm((DEPTH, BATCH, SEQ, PLE_DIM), 1.0),
        'norm_mix': gain((DEPTH, D_MODEL)),
        'norm_moe': gain((DEPTH, D_MODEL)),
        'norm_ple': gain((DEPTH, D_MODEL)),
        'final_norm': gain((D_MODEL,)),
        'm_in_w': nrm((N_MAMBA, D_MODEL, M_IN_DIM), D_MODEL ** -0.5),
        'm_conv_w': nrm((N_MAMBA, CONV_K, M_CONV_DIM), CONV_K ** -0.5),
        'm_conv_b': nrm((N_MAMBA, M_CONV_DIM), 0.02),
        'm_dt_bias': dt_bias((N_MAMBA, M_HEADS)),
        'm_A_log': a_log((N_MAMBA, M_HEADS)),
        'm_D': gain((N_MAMBA, M_HEADS)),
        'm_norm_w': gain((N_MAMBA, M_INNER)),
        'm_out_w': nrm((N_MAMBA, M_INNER, D_MODEL), M_INNER ** -0.5),
        'g_in_w': nrm((N_GDN, D_MODEL, G_IN_DIM), D_MODEL ** -0.5),
        'g_conv_w': nrm((N_GDN, CONV_K, G_CONV_DIM), CONV_K ** -0.5),
        'g_dt_bias': dt_bias((N_GDN, G_V_HEADS)),
        'g_A_log': a_log((N_GDN, G_V_HEADS)),
        'g_norm_w': gain((N_GDN, G_HEAD_DIM)),
        'g_out_w': nrm((N_GDN, G_V_DIM, D_MODEL), G_V_DIM ** -0.5),
        'moe_w_group': nrm((DEPTH, D_MODEL, N_EGROUPS), D_MODEL ** -0.5),
        'moe_b_group': nrm((DEPTH, N_EGROUPS), 0.01),
        'moe_w_expert': nrm((DEPTH, D_MODEL, N_EXPERTS), D_MODEL ** -0.5),
        'moe_b_expert': nrm((DEPTH, N_EXPERTS), 0.01),
        'moe_w1': nrm((DEPTH, N_EXPERTS, D_MODEL, D_EXPERT), D_MODEL ** -0.5),
        'moe_w3': nrm((DEPTH, N_EXPERTS, D_MODEL, D_EXPERT), D_MODEL ** -0.5),
        'moe_w2': nrm((DEPTH, N_EXPERTS, D_EXPERT, D_MODEL), D_EXPERT ** -0.5),
        'ple_w_proj': nrm((DEPTH, PLE_DIM, D_MODEL), PLE_DIM ** -0.5),
        'ple_w_gate': nrm((DEPTH, D_MODEL, D_MODEL), D_MODEL ** -0.5),
    }


def reference(x, p, norm_mix, norm_moe, norm_ple, final_norm,
              m_in_w, m_conv_w, m_conv_b, m_dt_bias, m_A_log, m_D, m_norm_w, m_out_w,
              g_in_w, g_conv_w, g_dt_bias, g_A_log, g_norm_w, g_out_w,
              moe_w_group, moe_b_group, moe_w_expert, moe_b_expert, moe_w1, moe_w3, moe_w2,
              ple_w_proj, ple_w_gate):
    h = x
    for i in range(DEPTH):
        hn = rmsnorm(h, norm_mix[i])
        j = i // N_MIXERS
        if i % N_MIXERS == 0:
            mix = mamba2_mixer(hn, m_in_w[j], m_conv_w[j], m_conv_b[j], m_dt_bias[j], m_A_log[j],
                               m_D[j], m_norm_w[j], m_out_w[j])
        else:
            mix = gated_deltanet_mixer(hn, g_in_w[j], g_conv_w[j], g_dt_bias[j], g_A_log[j],
                                       g_norm_w[j], g_out_w[j])
        h = h + mix
        h = h + hierarchical_moe(rmsnorm(h, norm_moe[i]), moe_w_group[i], moe_b_group[i],
                                 moe_w_expert[i], moe_b_expert[i], moe_w1[i], moe_w3[i], moe_w2[i])
        ple_gate = jax.nn.sigmoid(rmsnorm(h, norm_ple[i]) @ ple_w_gate[i])
        h = h + ple_gate * (p[i] @ ple_w_proj[i])
    return rmsnorm(h, final_norm)
```

```python
import functools

import jax
import jax.numpy as jnp
from jax import lax
from jax.experimental import pallas as pl
from jax.experimental.pallas import tpu as pltpu

F32 = jnp.float32
BF16 = jnp.bfloat16
I32 = jnp.int32
EPS = 1e-6
HIGHEST = lax.Precision.HIGHEST

LANES = 128
SUBLANES = 8
VMEM_LIMIT = 56 * 1024 * 1024

CONV_K = 4
SSD_CHUNK = 128
SSD_HEADDIM = 64
SSD_STATE = 128
SSD_GROUPS = 4
GDN_CHUNK = 64
GDN_HEAD = 128
N_EGROUPS = 8
EXPERTS_PER_GROUP = 8
N_EXPERTS = N_EGROUPS * EXPERTS_PER_GROUP
MOE_BLOCK = 256
ROUTE_TILE = 512


def _cparams(*sem):
    return pltpu.CompilerParams(dimension_semantics=sem, vmem_limit_bytes=VMEM_LIMIT)


def _silu(x):
    return x * jax.nn.sigmoid(x)


def _softplus(x):
    return jnp.maximum(x, 0.0) + jnp.log1p(jnp.exp(-jnp.abs(x)))


def _rms(x, w):
    return x * lax.rsqrt(jnp.mean(x * x, axis=-1, keepdims=True) + EPS) * w


def _dot(a, b):
    return jnp.dot(a.astype(BF16), b.astype(BF16), preferred_element_type=F32)


def _dot_f32(a, b):
    return jnp.dot(a, b, preferred_element_type=F32, precision=HIGHEST)


def _dot_tn(a, b):
    return lax.dot_general(a.astype(BF16), b.astype(BF16), (((0,), (0,)), ((), ())),
                           preferred_element_type=F32)


def _dot_nt(a, b):
    return lax.dot_general(a.astype(BF16), b.astype(BF16), (((1,), (1,)), ((), ())),
                           preferred_element_type=F32)


def _tri(n, strict=False):
    r = lax.broadcasted_iota(I32, (n, n), 0)
    c = lax.broadcasted_iota(I32, (n, n), 1)
    return (r > c) if strict else (r >= c)


def _norm_matmul_kernel(x_ref, nw_ref, w_ref, o_ref, *, n_chunk):
    xb = _rms(x_ref[...], nw_ref[...]).astype(BF16)
    n = o_ref.shape[1]
    for c0 in range(0, n, n_chunk):
        c1 = min(c0 + n_chunk, n)
        o_ref[:, c0:c1] = jnp.dot(xb, w_ref[:, c0:c1], preferred_element_type=F32)


def norm_matmul(x, nw, w, *, tm=256, n_chunk=512):
    t, d = x.shape
    n = w.shape[1]
    return pl.pallas_call(
        functools.partial(_norm_matmul_kernel, n_chunk=n_chunk),
        grid=(t // tm,),
        in_specs=[pl.BlockSpec((tm, d), lambda i: (i, 0)),
                  pl.BlockSpec((1, d), lambda i: (0, 0)),
                  pl.BlockSpec((d, n), lambda i: (0, 0))],
        out_specs=pl.BlockSpec((tm, n), lambda i: (i, 0)),
        out_shape=jax.ShapeDtypeStruct((t, n), F32),
        compiler_params=_cparams("arbitrary"),
        name="norm_matmul",
    )(x, nw.reshape(1, d), w)


def _matmul_residual_kernel(y_ref, w_ref, r_ref, o_ref):
    o_ref[...] = r_ref[...] + jnp.dot(y_ref[...].astype(BF16), w_ref[...],
                                      preferred_element_type=F32)


def matmul_residual(y, w, res, *, tm=512):
    t, k = y.shape
    d = w.shape[1]
    return pl.pallas_call(
        _matmul_residual_kernel,
        grid=(t // tm,),
        in_specs=[pl.BlockSpec((tm, k), lambda i: (i, 0)),
                  pl.BlockSpec((k, d), lambda i: (0, 0)),
                  pl.BlockSpec((tm, d), lambda i: (i, 0))],
        out_specs=pl.BlockSpec((tm, d), lambda i: (i, 0)),
        out_shape=jax.ShapeDtypeStruct((t, d), F32),
        compiler_params=_cparams("arbitrary"),
        name="matmul_residual",
    )(y, w, res)


def _conv_silu(x_ref, stage_ref, w_ref, bias, first):
    q = x_ref.shape[0]

    @pl.when(first)
    def _():
        stage_ref[0:SUBLANES, :] = jnp.zeros((SUBLANES, stage_ref.shape[1]), F32)

    stage_ref[SUBLANES:SUBLANES + q, :] = x_ref[...]
    acc = stage_ref[SUBLANES:SUBLANES + q, :] * w_ref[CONV_K - 1:CONV_K, :]
    for j in range(CONV_K - 1):
        off = SUBLANES - (CONV_K - 1) + j
        acc = acc + stage_ref[off:off + q, :] * w_ref[j:j + 1, :]
    if bias is not None:
        acc = acc + bias
    stage_ref[0:SUBLANES, :] = stage_ref[q:q + SUBLANES, :]
    return _silu(acc)


def _ssd_kernel(z_ref, x_ref, b_ref, c_ref, dt_ref,
                cwx_ref, cwb_ref, cwc_ref, cbx_ref, cbb_ref, cbc_ref,
                dtb_ref, alog_ref, dfull_ref, nw_ref,
                o_ref,
                sx_ref, sb_ref, sc_ref, y_ref, xw_ref, st_ref):
    q = x_ref.shape[0]
    hpg = x_ref.shape[1] // (SSD_GROUPS * SSD_HEADDIM)
    first = pl.program_id(1) == 0

    @pl.when(first)
    def _():
        st_ref[...] = jnp.zeros(st_ref.shape, F32)

    xs = _conv_silu(x_ref, sx_ref, cwx_ref, cbx_ref[...], first)
    bm = _conv_silu(b_ref, sb_ref, cwb_ref, cbb_ref[...], first)
    cm = _conv_silu(c_ref, sc_ref, cwc_ref, cbc_ref[...], first)

    dt = _softplus(dt_ref[...] + dtb_ref[...])
    da = dt * (-jnp.exp(alog_ref[...]))
    causal = _tri(q)
    a = _dot_f32(causal.astype(F32), da)
    a_t = a.T
    ea = jnp.exp(a)
    a_last = a[q - 1:q, :]
    to_end = jnp.exp(a_last - a) * dt
    ea_last = jnp.exp(a_last)
    dt_t = dt.T

    for g in range(SSD_GROUPS):
        bg = bm[:, g * SSD_STATE:(g + 1) * SSD_STATE]
        cg = cm[:, g * SSD_STATE:(g + 1) * SSD_STATE]
        cb = _dot_nt(cg, bg)
        gw = hpg * SSD_HEADDIM
        y_inter = _dot(cg, st_ref[g])
        for jj in range(hpg):
            j = g * hpg + jj
            lo, hi = j * SSD_HEADDIM, (j + 1) * SSD_HEADDIM
            xj = xs[:, lo:hi]
            diff = a[:, j:j + 1] - a_t[j:j + 1, :]
            seg = jnp.exp(jnp.where(causal, diff, -jnp.inf))
            m = cb * seg * dt_t[j:j + 1, :]
            yj = _dot(m, xj) + y_inter[:, jj * SSD_HEADDIM:(jj + 1) * SSD_HEADDIM] * ea[:, j:j + 1]
            y_ref[:, lo:hi] = yj
            xw_ref[:, lo:hi] = xj * to_end[:, j:j + 1]
        upd = _dot_tn(bg, xw_ref[:, g * gw:(g + 1) * gw])
        for jj in range(hpg):
            j = g * hpg + jj
            sl = slice(jj * SSD_HEADDIM, (jj + 1) * SSD_HEADDIM)
            st_ref[g, :, sl] = st_ref[g, :, sl] * ea_last[:, j:j + 1] + upd[:, sl]

    y = y_ref[...] + dfull_ref[...] * xs
    o_ref[...] = _rms(y * _silu(z_ref[...]), nw_ref[...])


def ssd_mixer_core(proj, conv_w, conv_b, dt_bias, a_log, d_skip, norm_w, *, batch, seq,
                   inner, heads):
    t = proj.shape[0]
    q = SSD_CHUNK
    gn = SSD_GROUPS * SSD_STATE
    nc = seq // q
    cwx, cwb, cwc = conv_w[:, :inner], conv_w[:, inner:inner + gn], conv_w[:, inner + gn:]
    cbx = conv_b[:inner].reshape(1, inner)
    cbb = conv_b[inner:inner + gn].reshape(1, gn)
    cbc = conv_b[inner + gn:].reshape(1, gn)
    pad = LANES - heads
    dtb = jnp.pad(dt_bias, (0, pad)).reshape(1, LANES)
    alog = jnp.pad(a_log, (0, pad)).reshape(1, LANES)
    dfull = jnp.repeat(d_skip, SSD_HEADDIM).reshape(1, inner)
    nw = norm_w.reshape(1, inner)

    def row(i, c):
        return i * nc + c

    const = lambda shape: pl.BlockSpec(shape, lambda i, c: (0, 0))
    x_blk = inner // inner
    b_blk = (2 * inner) // gn
    c_blk = b_blk + 1
    dt_blk = (2 * inner + 2 * gn) // LANES
    return pl.pallas_call(
        _ssd_kernel,
        grid=(batch, nc),
        in_specs=[pl.BlockSpec((q, inner), lambda i, c: (row(i, c), 0)),
                  pl.BlockSpec((q, inner), lambda i, c: (row(i, c), x_blk)),
                  pl.BlockSpec((q, gn), lambda i, c: (row(i, c), b_blk)),
                  pl.BlockSpec((q, gn), lambda i, c: (row(i, c), c_blk)),
                  pl.BlockSpec((q, LANES), lambda i, c: (row(i, c), dt_blk)),
                  const((CONV_K, inner)), const((CONV_K, gn)), const((CONV_K, gn)),
                  const((1, inner)), const((1, gn)), const((1, gn)),
                  const((1, LANES)), const((1, LANES)), const((1, inner)), const((1, inner))],
        out_specs=pl.BlockSpec((q, inner), lambda i, c: (row(i, c), 0)),
        out_shape=jax.ShapeDtypeStruct((t, inner), F32),
        scratch_shapes=[pltpu.VMEM((q + SUBLANES, inner), F32),
                        pltpu.VMEM((q + SUBLANES, gn), F32),
                        pltpu.VMEM((q + SUBLANES, gn), F32),
                        pltpu.VMEM((q, inner), F32),
                        pltpu.VMEM((q, inner), F32),
                        pltpu.VMEM((SSD_GROUPS, SSD_STATE, inner // SSD_GROUPS), F32)],
        compiler_params=_cparams("arbitrary", "arbitrary"),
        name="ssd_scan",
    )(proj, proj, proj, proj, proj, cwx, cwb, cwc, cbx, cbb, cbc, dtb, alog, dfull, nw)


def _l2n(x):
    return x * lax.rsqrt(jnp.sum(x * x, axis=-1, keepdims=True) + EPS)


def _gdn_kernel(q_ref, k_ref, v_ref, z_ref, ba_ref,
                cwq_ref, cwk_ref, cwv_ref, dtb_ref, alog_ref, nw_ref,
                o_ref,
                sq_ref, sk_ref, sv_ref, s_ref, *, v_heads):
    c = q_ref.shape[0]
    qk_heads = q_ref.shape[1] // GDN_HEAD
    rep = v_heads // qk_heads
    first = pl.program_id(1) == 0

    @pl.when(first)
    def _():
        s_ref[...] = jnp.zeros(s_ref.shape, F32)

    qc = _conv_silu(q_ref, sq_ref, cwq_ref, None, first)
    kc = _conv_silu(k_ref, sk_ref, cwk_ref, None, first)
    vc = _conv_silu(v_ref, sv_ref, cwv_ref, None, first)

    ba = ba_ref[...]
    beta = jax.nn.sigmoid(ba)
    gl = -jnp.exp(alog_ref[...]) * _softplus(ba + dtb_ref[...])
    causal = _tri(c)
    strict = _tri(c, strict=True)
    eye = (lax.broadcasted_iota(I32, (c, c), 0) == lax.broadcasted_iota(I32, (c, c), 1)).astype(F32)
    gcs = _dot_f32(causal.astype(F32), gl)
    gcs_t = gcs.T
    eg = jnp.exp(gcs)
    g_last = gcs[c - 1:c, :]
    eg_last = jnp.exp(g_last)
    eg_end = jnp.exp(g_last - gcs)

    for hq in range(qk_heads):
        sl = slice(hq * GDN_HEAD, (hq + 1) * GDN_HEAD)
        qn = _l2n(qc[:, sl]) * (GDN_HEAD ** -0.5)
        kn = _l2n(kc[:, sl])
        kk = _dot_nt(kn, kn)
        qk = _dot_nt(qn, kn)
        for r in range(rep):
            h = hq * rep + r
            hs = slice(h * GDN_HEAD, (h + 1) * GDN_HEAD)
            gi = v_heads + h
            diff = gcs[:, gi:gi + 1] - gcs_t[gi:gi + 1, :]
            decay = jnp.exp(jnp.where(causal, diff, -jnp.inf))
            bcol = beta[:, h:h + 1]
            a_kk = jnp.where(strict, bcol * kk * decay, 0.0)
            pw = -a_kk
            tinv = eye + pw
            sz = 1
            while 2 * sz < c:
                pw = _dot_f32(pw, pw)
                tinv = tinv + _dot_f32(tinv, pw)
                sz *= 2
            kb = kn * bcol
            u = _dot_f32(tinv, vc[:, hs] * bcol)
            w = _dot_f32(tinv, kb * eg[:, gi:gi + 1])
            s = s_ref[h]
            v_new = u - _dot(w, s)
            a_qk = jnp.where(causal, qk * decay, 0.0)
            o = _dot(qn * eg[:, gi:gi + 1], s) + _dot(a_qk, v_new)
            s_ref[h] = s * eg_last[:, gi:gi + 1] + _dot_tn(kn * eg_end[:, gi:gi + 1], v_new)
            o_ref[:, hs] = _rms(o, nw_ref[...]) * _silu(z_ref[:, hs])


def gdn_mixer_core(proj, conv_w, dt_bias, a_log, norm_w, *, batch, seq, qk_dim, v_dim, v_heads):
    t = proj.shape[0]
    c = GDN_CHUNK
    nc = seq // c
    cwq, cwk, cwv = conv_w[:, :qk_dim], conv_w[:, qk_dim:2 * qk_dim], conv_w[:, 2 * qk_dim:]
    dtb = jnp.pad(dt_bias, (v_heads, LANES - 2 * v_heads)).reshape(1, LANES)
    alog = jnp.pad(a_log, (v_heads, LANES - 2 * v_heads)).reshape(1, LANES)
    nw = norm_w.reshape(1, GDN_HEAD)

    def row(i, cc):
        return i * nc + cc

    const = lambda shape: pl.BlockSpec(shape, lambda i, cc: (0, 0))
    v_blk = (2 * qk_dim) // v_dim
    z_blk = v_blk + 1
    ba_blk = (2 * qk_dim + 2 * v_dim) // LANES
    return pl.pallas_call(
        functools.partial(_gdn_kernel, v_heads=v_heads),
        grid=(batch, nc),
        in_specs=[pl.BlockSpec((c, qk_dim), lambda i, cc: (row(i, cc), 0)),
                  pl.BlockSpec((c, qk_dim), lambda i, cc: (row(i, cc), 1)),
                  pl.BlockSpec((c, v_dim), lambda i, cc: (row(i, cc), v_blk)),
                  pl.BlockSpec((c, v_dim), lambda i, cc: (row(i, cc), z_blk)),
                  pl.BlockSpec((c, LANES), lambda i, cc: (row(i, cc), ba_blk)),
                  const((CONV_K, qk_dim)), const((CONV_K, qk_dim)), const((CONV_K, v_dim)),
                  const((1, LANES)), const((1, LANES)), const((1, GDN_HEAD))],
        out_specs=pl.BlockSpec((c, v_dim), lambda i, cc: (row(i, cc), 0)),
        out_shape=jax.ShapeDtypeStruct((t, v_dim), F32),
        scratch_shapes=[pltpu.VMEM((c + SUBLANES, qk_dim), F32),
                        pltpu.VMEM((c + SUBLANES, qk_dim), F32),
                        pltpu.VMEM((c + SUBLANES, v_dim), F32),
                        pltpu.VMEM((v_heads, GDN_HEAD, GDN_HEAD), F32)],
        compiler_params=_cparams("arbitrary", "arbitrary"),
        name="gdn_scan",
    )(proj, proj, proj, proj, proj, cwq, cwk, cwv, dtb, alog, nw)


def _first_index(mask, n):
    idx = lax.broadcasted_iota(I32, mask.shape, 0)
    return jnp.min(jnp.where(mask, idx, n), axis=0, keepdims=True)


def _router_kernel(h_ref, nw_ref, wr_ref, br_ref, upper_ref,
                   eid_ref, rank_ref, gate_ref, cnt_ref, base_ref):
    tq = h_ref.shape[0]
    i = pl.program_id(0)

    @pl.when(i == 0)
    def _():
        base_ref[...] = jnp.zeros(base_ref.shape, F32)

    xn = _rms(h_ref[...], nw_ref[...])
    logits = lax.dot_general(wr_ref[...], xn, (((1,), (1,)), ((), ())),
                             preferred_element_type=F32, precision=HIGHEST) + br_ref[:, 0:1]
    gl = logits[0:N_EGROUPS, :]
    gmax = jnp.max(gl, axis=0, keepdims=True)
    g_sel = _first_index(gl == gmax, N_EGROUPS)
    p_sel = 1.0 / jnp.sum(jnp.exp(gl - gmax), axis=0, keepdims=True)
    e_in = logits[N_EGROUPS:N_EGROUPS + EXPERTS_PER_GROUP, :]
    for g in range(1, N_EGROUPS):
        lo = N_EGROUPS + g * EXPERTS_PER_GROUP
        e_in = jnp.where(g_sel == g, logits[lo:lo + EXPERTS_PER_GROUP, :], e_in)
    m1 = jnp.max(e_in, axis=0, keepdims=True)
    i1 = _first_index(e_in == m1, EXPERTS_PER_GROUP)
    sub = lax.broadcasted_iota(I32, e_in.shape, 0)
    rest = jnp.where(sub == i1, -jnp.inf, e_in)
    m2 = jnp.max(rest, axis=0, keepdims=True)
    i2 = _first_index(rest == m2, EXPERTS_PER_GROUP)
    e2 = jnp.exp(m2 - m1)
    denom = 1.0 + e2
    gate_ref[0:1, :] = (1.0 / denom) * p_sel
    gate_ref[1:2, :] = (e2 / denom) * p_sel
    eid0 = g_sel * EXPERTS_PER_GROUP + i1
    eid1 = g_sel * EXPERTS_PER_GROUP + i2
    eid_ref[0:1, :] = eid0
    eid_ref[1:2, :] = eid1

    erow = lax.broadcasted_iota(I32, (N_EXPERTS, tq), 0)
    oh0 = erow == eid0
    oh1 = erow == eid1
    oh0f = oh0.astype(F32)
    oh1f = oh1.astype(F32)
    cum0 = jnp.dot(oh0f.astype(BF16), upper_ref[...], preferred_element_type=F32)
    cum1 = jnp.dot(oh1f.astype(BF16), upper_ref[...], preferred_element_type=F32)
    base = base_ref[:, 0:1]
    tot0 = jnp.sum(oh0f, axis=1, keepdims=True)
    tot1 = jnp.sum(oh1f, axis=1, keepdims=True)
    r0 = jnp.sum(jnp.where(oh0, base + cum0, 0.0), axis=0, keepdims=True)
    r1 = jnp.sum(jnp.where(oh1, base + tot0 + cum1, 0.0), axis=0, keepdims=True)
    rank_ref[0:1, :] = r0.astype(I32)
    rank_ref[1:2, :] = r1.astype(I32)
    new_base = base + tot0 + tot1
    base_ref[...] = jnp.broadcast_to(new_base, base_ref.shape)
    cnt_ref[...] = jnp.broadcast_to(new_base, cnt_ref.shape).astype(I32)


def moe_router(h, nw, w_group, b_group, w_expert, b_expert):
    t, d = h.shape
    tq = ROUTE_TILE
    nr = N_EGROUPS + N_EXPERTS
    wr = jnp.pad(jnp.concatenate([w_group, w_expert], axis=1).T, ((0, LANES - nr), (0, 0)))
    br = jnp.pad(jnp.concatenate([b_group, b_expert]), (0, LANES - nr))
    br = jnp.broadcast_to(br[:, None], (LANES, LANES))
    upper = (jnp.arange(tq)[:, None] < jnp.arange(tq)[None, :]).astype(BF16)
    row_spec = pl.BlockSpec((2, tq), lambda i: (0, i))
    eid, rank, gate, cnt = pl.pallas_call(
        _router_kernel,
        grid=(t // tq,),
        in_specs=[pl.BlockSpec((tq, d), lambda i: (i, 0)),
                  pl.BlockSpec((1, d), lambda i: (0, 0)),
                  pl.BlockSpec((LANES, d), lambda i: (0, 0)),
                  pl.BlockSpec((LANES, LANES), lambda i: (0, 0)),
                  pl.BlockSpec((tq, tq), lambda i: (0, 0))],
        out_specs=[row_spec, row_spec, row_spec,
                   pl.BlockSpec((N_EXPERTS, LANES), lambda i: (0, 0))],
        out_shape=[jax.ShapeDtypeStruct((2, t), I32), jax.ShapeDtypeStruct((2, t), I32),
                   jax.ShapeDtypeStruct((2, t), F32),
                   jax.ShapeDtypeStruct((N_EXPERTS, LANES), I32)],
        scratch_shapes=[pltpu.VMEM((N_EXPERTS, LANES), F32)],
        compiler_params=_cparams("arbitrary"),
        name="moe_router",
    )(h, nw.reshape(1, d), wr, br, upper)
    return eid, rank, gate, cnt[:, 0]


def _tile_index_layout(pos, tq):
    t = pos.shape[1]
    sub = tq // LANES
    return jnp.transpose(pos.reshape(2, t // tq, sub, LANES), (1, 0, 2, 3)).reshape(
        t // tq, 2 * sub, LANES)


def _row_copy(src, src_row, dst, dst_row, sem):
    return pltpu.make_async_copy(src.at[pl.ds(src_row, 1)], dst.at[pl.ds(dst_row, 1)], sem)


def _dispatch_kernel(pos_hbm, h_hbm, xb_in, xb_out, pos_smem, sem_idx, sem_rows):
    del xb_in
    tq = ROUTE_TILE
    sub = tq // LANES
    i = pl.program_id(0)
    cp = pltpu.make_async_copy(pos_hbm.at[i], pos_smem, sem_idx)
    cp.start()
    cp.wait()
    for rr in range(sub):
        def body(cc, carry, rr=rr):
            t = i * tq + rr * LANES + cc
            _row_copy(h_hbm, t, xb_out, pos_smem[rr, cc], sem_rows).start()
            _row_copy(h_hbm, t, xb_out, pos_smem[sub + rr, cc], sem_rows).start()
            return carry
        lax.fori_loop(0, LANES, body, 0)
    pltpu.make_async_copy(h_hbm.at[pl.ds(0, 2 * tq)], xb_out.at[pl.ds(0, 2 * tq)], sem_rows).wait()


def moe_dispatch(h, pos_tiles, n_pad):
    t, d = h.shape
    tq = ROUTE_TILE
    xb0 = jnp.zeros((n_pad, d), F32)
    return pl.pallas_call(
        _dispatch_kernel,
        grid=(t // tq,),
        in_specs=[pl.BlockSpec(memory_space=pl.ANY),
                  pl.BlockSpec(memory_space=pl.ANY),
                  pl.BlockSpec(memory_space=pl.ANY)],
        out_specs=pl.BlockSpec(memory_space=pl.ANY),
        out_shape=jax.ShapeDtypeStruct((n_pad, d), F32),
        scratch_shapes=[pltpu.SMEM((2 * tq // LANES, LANES), I32),
                        pltpu.SemaphoreType.DMA, pltpu.SemaphoreType.DMA],
        input_output_aliases={2: 0},
        compiler_params=_cparams("arbitrary"),
        name="moe_dispatch",
    )(pos_tiles, h, xb0)


def _expert_kernel(be_ref, nu_ref, x_ref, nw_ref, w1_ref, w3_ref, w2_ref, o_ref):
    del be_ref
    b = pl.program_id(0)

    @pl.when(b < nu_ref[0])
    def _():
        xb = _rms(x_ref[...], nw_ref[...]).astype(BF16)
        h1 = jnp.dot(xb, w1_ref[0], preferred_element_type=F32)
        h3 = jnp.dot(xb, w3_ref[0], preferred_element_type=F32)
        hid = (_silu(h1) * h3).astype(BF16)
        o_ref[...] = jnp.dot(hid, w2_ref[0], preferred_element_type=F32)

    @pl.when(b >= nu_ref[0])
    def _():
        o_ref[...] = jnp.zeros(o_ref.shape, F32)


def moe_experts(xb, nw, w1, w3, w2, blk_expert, n_used):
    n_pad, d = xb.shape
    de = w1.shape[2]
    nb = n_pad // MOE_BLOCK
    grid_spec = pltpu.PrefetchScalarGridSpec(
        num_scalar_prefetch=2,
        grid=(nb,),
        in_specs=[pl.BlockSpec((MOE_BLOCK, d), lambda b, be, nu: (b, 0)),
                  pl.BlockSpec((1, d), lambda b, be, nu: (0, 0)),
                  pl.BlockSpec((1, d, de), lambda b, be, nu: (be[b], 0, 0)),
                  pl.BlockSpec((1, d, de), lambda b, be, nu: (be[b], 0, 0)),
                  pl.BlockSpec((1, de, d), lambda b, be, nu: (be[b], 0, 0))],
        out_specs=pl.BlockSpec((MOE_BLOCK, d), lambda b, be, nu: (b, 0)),
    )
    return pl.pallas_call(
        _expert_kernel,
        grid_spec=grid_spec,
        out_shape=jax.ShapeDtypeStruct((n_pad, d), F32),
        compiler_params=_cparams("arbitrary"),
        name="moe_experts",
    )(blk_expert, n_used, xb, nw.reshape(1, d), w1, w3, w2)


def _combine_ple_kernel(pos_hbm, yb_hbm, h_ref, gate_ref, p_ref, nw_ref, wg_ref, wp_ref, fw_ref,
                        o_ref, pos_smem, buf0_ref, buf1_ref, sem_idx, sem_rows, *, final):
    tq = h_ref.shape[0]
    sub = tq // LANES
    i = pl.program_id(0)
    cp = pltpu.make_async_copy(pos_hbm.at[i], pos_smem, sem_idx)
    cp.start()
    cp.wait()
    for rr in range(sub):
        def body(cc, carry, rr=rr):
            r = rr * LANES + cc
            _row_copy(yb_hbm, pos_smem[rr, cc], buf0_ref, r, sem_rows).start()
            _row_copy(yb_hbm, pos_smem[sub + rr, cc], buf1_ref, r, sem_rows).start()
            return carry
        lax.fori_loop(0, LANES, body, 0)
    pltpu.make_async_copy(yb_hbm.at[pl.ds(0, tq)], buf0_ref, sem_rows).wait()
    pltpu.make_async_copy(yb_hbm.at[pl.ds(0, tq)], buf1_ref, sem_rows).wait()

    gate = gate_ref[...]
    h2 = h_ref[...] + gate[:, 0:1] * buf0_ref[...] + gate[:, 1:2] * buf1_ref[...]
    hn = _rms(h2, nw_ref[...]).astype(BF16)
    pg = jax.nn.sigmoid(jnp.dot(hn, wg_ref[...], preferred_element_type=F32))
    pp = jnp.dot(p_ref[...].astype(BF16), wp_ref[...], preferred_element_type=F32)
    out = h2 + pg * pp
    if final:
        out = _rms(out, fw_ref[...])
    o_ref[...] = out


def moe_combine_ple(h, yb, pos_tiles, gate_cols, p, nw, wg, wp, fw, *, final):
    t, d = h.shape
    tq = ROUTE_TILE
    pd = p.shape[1]
    return pl.pallas_call(
        functools.partial(_combine_ple_kernel, final=final),
        grid=(t // tq,),
        in_specs=[pl.BlockSpec(memory_space=pl.ANY),
                  pl.BlockSpec(memory_space=pl.ANY),
                  pl.BlockSpec((tq, d), lambda i: (i, 0)),
                  pl.BlockSpec((tq, 2), lambda i: (i, 0)),
                  pl.BlockSpec((tq, pd), lambda i: (i, 0)),
                  pl.BlockSpec((1, d), lambda i: (0, 0)),
                  pl.BlockSpec((d, d), lambda i: (0, 0)),
                  pl.BlockSpec((pd, d), lambda i: (0, 0)),
                  pl.BlockSpec((1, d), lambda i: (0, 0))],
        out_specs=pl.BlockSpec((tq, d), lambda i: (i, 0)),
        out_shape=jax.ShapeDtypeStruct((t, d), F32),
        scratch_shapes=[pltpu.SMEM((2 * tq // LANES, LANES), I32),
                        pltpu.VMEM((tq, d), F32), pltpu.VMEM((tq, d), F32),
                        pltpu.SemaphoreType.DMA, pltpu.SemaphoreType.DMA],
        compiler_params=_cparams("arbitrary"),
        name="moe_combine_ple",
    )(pos_tiles, yb, h, gate_cols, p, nw.reshape(1, d), wg, wp, fw.reshape(1, d))


def moe_ple_layer(h, p, norm_moe, w_group, b_group, w_expert, b_expert, w1, w3, w2,
                  norm_ple, wg, wp, fw, *, final):
    t, d = h.shape
    eid, rank, gate, counts = moe_router(h, norm_moe, w_group, b_group, w_expert, b_expert)
    padded = (counts + MOE_BLOCK - 1) // MOE_BLOCK * MOE_BLOCK
    pend = jnp.cumsum(padded)
    pstart = pend - padded
    na = 2 * t
    n_pad = (na + MOE_BLOCK - 1) // MOE_BLOCK * MOE_BLOCK + N_EXPERTS * MOE_BLOCK
    nb = n_pad // MOE_BLOCK
    blk_start = jnp.arange(nb, dtype=I32) * MOE_BLOCK
    blk_expert = jnp.minimum(jnp.sum(blk_start[:, None] >= pend[None, :], axis=-1),
                             N_EXPERTS - 1).astype(I32)
    n_used = (pend[-1:] // MOE_BLOCK).astype(I32)
    pos = jnp.take(pstart, eid, axis=0).astype(I32) + rank
    pos_tiles = _tile_index_layout(pos, ROUTE_TILE)
    xb = moe_dispatch(h, pos_tiles, n_pad)
    yb = moe_experts(xb, norm_moe, w1.astype(BF16), w3.astype(BF16), w2.astype(BF16),
                     blk_expert, n_used)
    return moe_combine_ple(h, yb, pos_tiles, gate.T, p, norm_ple, wg.astype(BF16),
                           wp.astype(BF16), fw, final=final)


def _pad_cols(w, n):
    return jnp.pad(w, ((0, 0), (0, n - w.shape[1])))


def kernel(x, p, norm_mix, norm_moe, norm_ple, final_norm, m_in_w, m_conv_w, m_conv_b, m_dt_bias, m_A_log, m_D, m_norm_w, m_out_w, g_in_w, g_conv_w, g_dt_bias, g_A_log, g_norm_w, g_out_w, moe_w_group, moe_b_group, moe_w_expert, moe_b_expert, moe_w1, moe_w3, moe_w2, ple_w_proj, ple_w_gate):
    batch, seq, d = x.shape
    t = batch * seq
    depth = p.shape[0]
    pd = p.shape[-1]
    h = x.reshape(t, d)
    p2 = p.reshape(depth, t, pd)
    for i in range(depth):
        j = i // 2
        if i % 2 == 0:
            inner = m_out_w.shape[1]
            heads = m_dt_bias.shape[1]
            conv_dim = m_conv_w.shape[2]
            w_in = _pad_cols(m_in_w[j], inner + conv_dim + LANES).astype(BF16)
            proj = norm_matmul(h, norm_mix[i], w_in)
            y = ssd_mixer_core(proj, m_conv_w[j], m_conv_b[j], m_dt_bias[j], m_A_log[j], m_D[j],
                               m_norm_w[j], batch=batch, seq=seq, inner=inner, heads=heads)
            h = matmul_residual(y, m_out_w[j].astype(BF16), h)
        else:
            v_dim = g_out_w.shape[1]
            v_heads = g_dt_bias.shape[1]
            conv_dim = g_conv_w.shape[2]
            qk_dim = (conv_dim - v_dim) // 2
            w_in = _pad_cols(g_in_w[j], conv_dim + v_dim + LANES).astype(BF16)
            proj = norm_matmul(h, norm_mix[i], w_in)
            y = gdn_mixer_core(proj, g_conv_w[j], g_dt_bias[j], g_A_log[j], g_norm_w[j],
                               batch=batch, seq=seq, qk_dim=qk_dim, v_dim=v_dim, v_heads=v_heads)
            h = matmul_residual(y, g_out_w[j].astype(BF16), h)
        h = moe_ple_layer(h, p2[i], norm_moe[i], moe_w_group[i], moe_b_group[i], moe_w_expert[i],
                          moe_b_expert[i], moe_w1[i], moe_w3[i], moe_w2[i], norm_ple[i],
                          ple_w_gate[i], ple_w_proj[i], final_norm, final=(i == depth - 1))
    return h.reshape(batch, seq, d)
```

```python
import functools
import math

import jax
import jax.numpy as jnp
from jax import lax
from jax.experimental import pallas as pl
from jax.experimental.pallas import tpu as pltpu

F32 = jnp.float32
BF16 = jnp.bfloat16
I32 = jnp.int32
EPS = 1e-6
HIGHEST = lax.Precision.HIGHEST

LANES = 128
SUBLANES = 8
VMEM_LIMIT = 56 * 1024 * 1024

CONV_K = 4
SSD_CHUNK = 128
SSD_HEADDIM = 64
SSD_STATE = 128
SSD_GROUPS = 4
GDN_CHUNK = 64
GDN_HEAD = 128
GDN_STACK = 256
GDN_SUBBLOCK = 16
N_EGROUPS = 8
EXPERTS_PER_GROUP = 8
N_EXPERTS = N_EGROUPS * EXPERTS_PER_GROUP
MOE_BLOCK = 256
ROUTE_TILE = 512


def _cparams(*sem):
    return pltpu.CompilerParams(dimension_semantics=sem, vmem_limit_bytes=VMEM_LIMIT)


def _silu(x):
    return x * jax.nn.sigmoid(x)


def _softplus(x):
    return jnp.maximum(x, 0.0) + jnp.log1p(jnp.exp(-jnp.abs(x)))


def _rms(x, w):
    return x * lax.rsqrt(jnp.mean(x * x, axis=-1, keepdims=True) + EPS) * w


def _dot(a, b):
    return jnp.dot(a.astype(BF16), b.astype(BF16), preferred_element_type=F32)


def _split(x):
    hi = x.astype(BF16)
    return hi, (x - hi.astype(F32)).astype(BF16)


def _dot_f32(a, b):
    return jnp.dot(a, b, preferred_element_type=F32, precision=HIGHEST)


def _dot_tn(a, b):
    return lax.dot_general(a.astype(BF16), b.astype(BF16), (((0,), (0,)), ((), ())),
                           preferred_element_type=F32)


def _dot_nt(a, b):
    return lax.dot_general(a.astype(BF16), b.astype(BF16), (((1,), (1,)), ((), ())),
                           preferred_element_type=F32)


def _tri(n, strict=False):
    r = lax.broadcasted_iota(I32, (n, n), 0)
    c = lax.broadcasted_iota(I32, (n, n), 1)
    return (r > c) if strict else (r >= c)


def _norm_matmul_kernel(x_ref, nw_ref, w_ref, o_ref, *, n_chunk):
    xb = _rms(x_ref[...], nw_ref[...]).astype(BF16)
    n = o_ref.shape[1]
    for c0 in range(0, n, n_chunk):
        c1 = min(c0 + n_chunk, n)
        o_ref[:, c0:c1] = jnp.dot(xb, w_ref[:, c0:c1], preferred_element_type=F32)


def norm_matmul(x, nw, w, *, tm=256, n_chunk=512):
    t, d = x.shape
    n = w.shape[1]
    return pl.pallas_call(
        functools.partial(_norm_matmul_kernel, n_chunk=n_chunk),
        grid=(t // tm,),
        in_specs=[pl.BlockSpec((tm, d), lambda i: (i, 0)),
                  pl.BlockSpec((1, d), lambda i: (0, 0)),
                  pl.BlockSpec((d, n), lambda i: (0, 0))],
        out_specs=pl.BlockSpec((tm, n), lambda i: (i, 0)),
        out_shape=jax.ShapeDtypeStruct((t, n), F32),
        compiler_params=_cparams("arbitrary"),
        name="norm_matmul",
    )(x, nw.reshape(1, d), w)


def _matmul_residual_kernel(y_ref, w_ref, r_ref, o_ref):
    o_ref[...] = r_ref[...] + jnp.dot(y_ref[...].astype(BF16), w_ref[...],
                                      preferred_element_type=F32)


def matmul_residual(y, w, res, *, tm=512):
    t, k = y.shape
    d = w.shape[1]
    return pl.pallas_call(
        _matmul_residual_kernel,
        grid=(t // tm,),
        in_specs=[pl.BlockSpec((tm, k), lambda i: (i, 0)),
                  pl.BlockSpec((k, d), lambda i: (0, 0)),
                  pl.BlockSpec((tm, d), lambda i: (i, 0))],
        out_specs=pl.BlockSpec((tm, d), lambda i: (i, 0)),
        out_shape=jax.ShapeDtypeStruct((t, d), F32),
        compiler_params=_cparams("arbitrary"),
        name="matmul_residual",
    )(y, w, res)


def _conv_silu(x_ref, stage_ref, w_ref, bias, first):
    q = x_ref.shape[0]

    @pl.when(first)
    def _():
        stage_ref[0:SUBLANES, :] = jnp.zeros((SUBLANES, stage_ref.shape[1]), F32)

    stage_ref[SUBLANES:SUBLANES + q, :] = x_ref[...]
    acc = stage_ref[SUBLANES:SUBLANES + q, :] * w_ref[CONV_K - 1:CONV_K, :]
    for j in range(CONV_K - 1):
        off = SUBLANES - (CONV_K - 1) + j
        acc = acc + stage_ref[off:off + q, :] * w_ref[j:j + 1, :]
    if bias is not None:
        acc = acc + bias
    stage_ref[0:SUBLANES, :] = stage_ref[q:q + SUBLANES, :]
    return _silu(acc)


def _ssd_kernel(z_ref, x_ref, b_ref, c_ref, dt_ref,
                cwx_ref, cwb_ref, cwc_ref, cbx_ref, cbb_ref, cbc_ref,
                dtb_ref, alog_ref, dfull_ref, nw_ref,
                o_ref,
                sx_ref, sb_ref, sc_ref, y_ref, xw_ref, st_ref):
    q = x_ref.shape[0]
    hpg = x_ref.shape[1] // (SSD_GROUPS * SSD_HEADDIM)
    first = pl.program_id(1) == 0

    @pl.when(first)
    def _():
        st_ref[...] = jnp.zeros(st_ref.shape, F32)

    xs = _conv_silu(x_ref, sx_ref, cwx_ref, cbx_ref[...], first)
    bm = _conv_silu(b_ref, sb_ref, cwb_ref, cbb_ref[...], first)
    cm = _conv_silu(c_ref, sc_ref, cwc_ref, cbc_ref[...], first)

    dt = _softplus(dt_ref[...] + dtb_ref[...])
    da = dt * (-jnp.exp(alog_ref[...]))
    causal = _tri(q)
    a = _dot_f32(causal.astype(F32), da)
    a_t = a.T
    ea = jnp.exp(a)
    a_last = a[q - 1:q, :]
    to_end = jnp.exp(a_last - a) * dt
    ea_last = jnp.exp(a_last)
    dt_t = dt.T

    for g in range(SSD_GROUPS):
        bg = bm[:, g * SSD_STATE:(g + 1) * SSD_STATE]
        cg = cm[:, g * SSD_STATE:(g + 1) * SSD_STATE]
        cb = _dot_nt(cg, bg)
        gw = hpg * SSD_HEADDIM
        y_inter = _dot(cg, st_ref[g])
        for jj in range(hpg):
            j = g * hpg + jj
            lo, hi = j * SSD_HEADDIM, (j + 1) * SSD_HEADDIM
            xj = xs[:, lo:hi]
            diff = a[:, j:j + 1] - a_t[j:j + 1, :]
            seg = jnp.exp(jnp.where(causal, diff, -jnp.inf))
            m = cb * seg * dt_t[j:j + 1, :]
            yj = _dot(m, xj) + y_inter[:, jj * SSD_HEADDIM:(jj + 1) * SSD_HEADDIM] * ea[:, j:j + 1]
            y_ref[:, lo:hi] = yj
            xw_ref[:, lo:hi] = xj * to_end[:, j:j + 1]
        upd = _dot_tn(bg, xw_ref[:, g * gw:(g + 1) * gw])
        for jj in range(hpg):
            j = g * hpg + jj
            sl = slice(jj * SSD_HEADDIM, (jj + 1) * SSD_HEADDIM)
            st_ref[g, :, sl] = st_ref[g, :, sl] * ea_last[:, j:j + 1] + upd[:, sl]

    y = y_ref[...] + dfull_ref[...] * xs
    o_ref[...] = _rms(y * _silu(z_ref[...]), nw_ref[...])


def ssd_mixer_core(proj, conv_w, conv_b, dt_bias, a_log, d_skip, norm_w, *, batch, seq,
                   inner, heads):
    t = proj.shape[0]
    q = SSD_CHUNK
    gn = SSD_GROUPS * SSD_STATE
    nc = seq // q
    cwx, cwb, cwc = conv_w[:, :inner], conv_w[:, inner:inner + gn], conv_w[:, inner + gn:]
    cbx = conv_b[:inner].reshape(1, inner)
    cbb = conv_b[inner:inner + gn].reshape(1, gn)
    cbc = conv_b[inner + gn:].reshape(1, gn)
    pad = LANES - heads
    dtb = jnp.pad(dt_bias, (0, pad)).reshape(1, LANES)
    alog = jnp.pad(a_log, (0, pad)).reshape(1, LANES)
    dfull = jnp.repeat(d_skip, SSD_HEADDIM).reshape(1, inner)
    nw = norm_w.reshape(1, inner)

    def row(i, c):
        return i * nc + c

    const = lambda shape: pl.BlockSpec(shape, lambda i, c: (0, 0))
    x_blk = inner // inner
    b_blk = (2 * inner) // gn
    c_blk = b_blk + 1
    dt_blk = (2 * inner + 2 * gn) // LANES
    return pl.pallas_call(
        _ssd_kernel,
        grid=(batch, nc),
        in_specs=[pl.BlockSpec((q, inner), lambda i, c: (row(i, c), 0)),
                  pl.BlockSpec((q, inner), lambda i, c: (row(i, c), x_blk)),
                  pl.BlockSpec((q, gn), lambda i, c: (row(i, c), b_blk)),
                  pl.BlockSpec((q, gn), lambda i, c: (row(i, c), c_blk)),
                  pl.BlockSpec((q, LANES), lambda i, c: (row(i, c), dt_blk)),
                  const((CONV_K, inner)), const((CONV_K, gn)), const((CONV_K, gn)),
                  const((1, inner)), const((1, gn)), const((1, gn)),
                  const((1, LANES)), const((1, LANES)), const((1, inner)), const((1, inner))],
        out_specs=pl.BlockSpec((q, inner), lambda i, c: (row(i, c), 0)),
        out_shape=jax.ShapeDtypeStruct((t, inner), F32),
        scratch_shapes=[pltpu.VMEM((q + SUBLANES, inner), F32),
                        pltpu.VMEM((q + SUBLANES, gn), F32),
                        pltpu.VMEM((q + SUBLANES, gn), F32),
                        pltpu.VMEM((q, inner), F32),
                        pltpu.VMEM((q, inner), F32),
                        pltpu.VMEM((SSD_GROUPS, SSD_STATE, inner // SSD_GROUPS), F32)],
        compiler_params=_cparams("arbitrary", "arbitrary"),
        name="ssd_scan",
    )(proj, proj, proj, proj, proj, cwx, cwb, cwc, cbx, cbb, cbc, dtb, alog, dfull, nw)


def _l2n(x):
    return x * lax.rsqrt(jnp.sum(x * x, axis=-1, keepdims=True) + EPS)


def _pair_cols(x, i0, i1, half):
    c = x.shape[0]
    lane = lax.broadcasted_iota(I32, (c, 2 * half), 1)
    return jnp.where(lane < half, jnp.broadcast_to(x[:, i0:i0 + 1], (c, 2 * half)),
                     jnp.broadcast_to(x[:, i1:i1 + 1], (c, 2 * half)))


def _gdn_kernel(q_ref, k_ref, v_ref, z_ref, ba_ref,
                cwq_ref, cwk_ref, cwv_ref, dtb_ref, alog_ref, nw_ref,
                o_ref,
                sq_ref, sk_ref, sv_ref, s_ref, *, v_heads):
    c = q_ref.shape[0]
    qk_heads = q_ref.shape[1] // GDN_HEAD
    rep = v_heads // qk_heads
    gsz = GDN_STACK // c
    first = pl.program_id(1) == 0

    @pl.when(first)
    def _():
        s_ref[...] = jnp.zeros(s_ref.shape, F32)

    qc = _conv_silu(q_ref, sq_ref, cwq_ref, None, first)
    kc = _conv_silu(k_ref, sk_ref, cwk_ref, None, first)
    vc = _conv_silu(v_ref, sv_ref, cwv_ref, None, first)

    ba = ba_ref[...]
    beta = jax.nn.sigmoid(ba)
    gl = -jnp.exp(alog_ref[...]) * _softplus(ba + dtb_ref[...])
    gcs = _dot_f32(_tri(c).astype(F32), gl)
    eg = jnp.exp(gcs)
    g_last = gcs[c - 1:c, :]
    eg_last = jnp.exp(g_last)
    eg_end = jnp.exp(g_last - gcs)

    t_idx = lax.broadcasted_iota(I32, (c, GDN_STACK), 0)
    s_idx = lax.broadcasted_iota(I32, (c, GDN_STACK), 1) & (c - 1)
    causal = t_idx >= s_idx
    strict = t_idx > s_idx
    eye = t_idx == s_idx
    eye_f = eye.astype(F32)
    sb_shift = GDN_SUBBLOCK.bit_length() - 1
    same_blk = (t_idx >> sb_shift) == (s_idx >> sb_shift)
    c_shift = c.bit_length() - 1
    head_diag = ((lax.broadcasted_iota(I32, (GDN_STACK, GDN_STACK), 0) >> c_shift)
                 == (lax.broadcasted_iota(I32, (GDN_STACK, GDN_STACK), 1) >> c_shift))

    def bdiag(parts):
        zero = jnp.zeros((), BF16)
        return tuple(jnp.where(head_diag, jnp.concatenate([p] * gsz, axis=0), zero) for p in parts)

    def mm(lhs, rhs_bd):
        n = len(lhs) * c
        parts = [_split(x) for x in lhs]
        l_hi = jnp.concatenate([p[0] for p in parts], axis=0)
        l_lo = jnp.concatenate([p[1] for p in parts], axis=0)
        r = jnp.dot(jnp.concatenate([l_hi, l_lo], axis=0), rhs_bd[0], preferred_element_type=F32)
        out = r[:n] + r[n:] + jnp.dot(l_hi, rhs_bd[1], preferred_element_type=F32)
        return [out[i * c:(i + 1) * c] for i in range(len(lhs))]

    def neumann(p0s, n_sq):
        ps = list(p0s)
        ds = [eye_f + p for p in ps]
        for i in range(n_sq):
            for g in range(len(ps)):
                bd = bdiag(_split(ps[g]))
                if i == 0:
                    ps[g], = mm([ps[g]], bd)
                else:
                    ps[g], x = mm([ps[g], ds[g]], bd)
                    ds[g] = ds[g] + x
        if n_sq > 0:
            for g in range(len(ps)):
                x, = mm([ds[g]], bdiag(_split(ps[g])))
                ds[g] = ds[g] + x
        return ds

    qns, kns, kks, qks = [], [], [], []
    for hq in range(qk_heads):
        sl = slice(hq * GDN_HEAD, (hq + 1) * GDN_HEAD)
        qn = _l2n(qc[:, sl]) * (GDN_HEAD ** -0.5)
        kn = _l2n(kc[:, sl])
        k2 = jnp.concatenate([kn] * rep, axis=0)
        qns.append(qn)
        kns.append(kn)
        kks.append(_dot_nt(kn, k2))
        qks.append(_dot_nt(qn, k2))

    n_grp = v_heads // gsz
    a_kks, a_qks = [], []
    for grp in range(n_grp):
        h0 = grp * gsz
        hq0 = h0 // rep
        nq = gsz // rep
        kk = jnp.concatenate(kks[hq0:hq0 + nq], axis=1)
        qk = jnp.concatenate(qks[hq0:hq0 + nq], axis=1)
        gcol = jnp.concatenate([_pair_cols(gcs, v_heads + h0 + 2 * i, v_heads + h0 + 2 * i + 1, c)
                                for i in range(gsz // 2)], axis=1)
        bcol = jnp.concatenate([_pair_cols(beta, h0 + 2 * i, h0 + 2 * i + 1, c)
                                for i in range(gsz // 2)], axis=1)
        grow = jnp.sum(jnp.where(eye, gcol, 0.0), axis=0, keepdims=True)
        decay = jnp.exp(jnp.where(causal, gcol - grow, -jnp.inf))
        a_kks.append(jnp.where(strict, bcol * kk * decay, 0.0))
        a_qks.append(jnp.where(causal, qk * decay, 0.0))

    dms = neumann([jnp.where(same_blk, -a, 0.0) for a in a_kks], sb_shift - 1)
    ms = [-mm([dms[g]], bdiag(_split(jnp.where(same_blk, 0.0, a_kks[g]))))[0] for g in range(n_grp)]
    ws = neumann(ms, (c // GDN_SUBBLOCK).bit_length() - 2)
    tinvs = [mm([ws[g]], bdiag(_split(dms[g])))[0] for g in range(n_grp)]

    for grp in range(n_grp):
        h0 = grp * gsz
        tinv = tinvs[grp]
        a_qk = a_qks[grp]
        for gh in range(gsz):
            h = h0 + gh
            hq = h // rep
            hs = slice(h * GDN_HEAD, (h + 1) * GDN_HEAD)
            cs = slice(gh * c, (gh + 1) * c)
            gi = v_heads + h
            b1 = beta[:, h:h + 1]
            kn = kns[hq]
            rhs = jnp.concatenate([vc[:, hs] * b1, kn * (b1 * eg[:, gi:gi + 1])], axis=1)
            sol = _dot(tinv[:, cs], rhs)
            s = s_ref[h]
            v_new = sol[:, :GDN_HEAD] - _dot(sol[:, GDN_HEAD:], s)
            o = _dot(qns[hq] * eg[:, gi:gi + 1], s) + _dot(a_qk[:, cs], v_new)
            s_ref[h] = s * eg_last[:, gi:gi + 1] + _dot_tn(kn * eg_end[:, gi:gi + 1], v_new)
            o_ref[:, hs] = _rms(o, nw_ref[...]) * _silu(z_ref[:, hs])


def gdn_mixer_core(proj, conv_w, dt_bias, a_log, norm_w, *, batch, seq, qk_dim, v_dim, v_heads):
    t = proj.shape[0]
    c = GDN_CHUNK
    nc = seq // c
    cwq, cwk, cwv = conv_w[:, :qk_dim], conv_w[:, qk_dim:2 * qk_dim], conv_w[:, 2 * qk_dim:]
    dtb = jnp.pad(dt_bias, (v_heads, LANES - 2 * v_heads)).reshape(1, LANES)
    alog = jnp.pad(a_log, (v_heads, LANES - 2 * v_heads)).reshape(1, LANES)
    nw = norm_w.reshape(1, GDN_HEAD)

    def row(i, cc):
        return i * nc + cc

    const = lambda shape: pl.BlockSpec(shape, lambda i, cc: (0, 0))
    v_blk = (2 * qk_dim) // v_dim
    z_blk = v_blk + 1
    ba_blk = (2 * qk_dim + 2 * v_dim) // LANES
    return pl.pallas_call(
        functools.partial(_gdn_kernel, v_heads=v_heads),
        grid=(batch, nc),
        in_specs=[pl.BlockSpec((c, qk_dim), lambda i, cc: (row(i, cc), 0)),
                  pl.BlockSpec((c, qk_dim), lambda i, cc: (row(i, cc), 1)),
                  pl.BlockSpec((c, v_dim), lambda i, cc: (row(i, cc), v_blk)),
                  pl.BlockSpec((c, v_dim), lambda i, cc: (row(i, cc), z_blk)),
                  pl.BlockSpec((c, LANES), lambda i, cc: (row(i, cc), ba_blk)),
                  const((CONV_K, qk_dim)), const((CONV_K, qk_dim)), const((CONV_K, v_dim)),
                  const((1, LANES)), const((1, LANES)), const((1, GDN_HEAD))],
        out_specs=pl.BlockSpec((c, v_dim), lambda i, cc: (row(i, cc), 0)),
        out_shape=jax.ShapeDtypeStruct((t, v_dim), F32),
        scratch_shapes=[pltpu.VMEM((c + SUBLANES, qk_dim), F32),
                        pltpu.VMEM((c + SUBLANES, qk_dim), F32),
                        pltpu.VMEM((c + SUBLANES, v_dim), F32),
                        pltpu.VMEM((v_heads, GDN_HEAD, GDN_HEAD), F32)],
        compiler_params=_cparams("arbitrary", "arbitrary"),
        name="gdn_scan",
    )(proj, proj, proj, proj, proj, cwq, cwk, cwv, dtb, alog, nw)


def _first_index(mask, n):
    idx = lax.broadcasted_iota(I32, mask.shape, 0)
    return jnp.min(jnp.where(mask, idx, n), axis=0, keepdims=True)


def _router_kernel(h_ref, nw_ref, wr_ref, br_ref, upper_ref,
                   eid_ref, rank_ref, gate_ref, cnt_ref, base_ref):
    tq = h_ref.shape[0]
    i = pl.program_id(0)

    @pl.when(i == 0)
    def _():
        base_ref[...] = jnp.zeros(base_ref.shape, F32)

    xn = _rms(h_ref[...], nw_ref[...])
    logits = lax.dot_general(wr_ref[...], xn, (((1,), (1,)), ((), ())),
                             preferred_element_type=F32, precision=HIGHEST) + br_ref[:, 0:1]
    gl = logits[0:N_EGROUPS, :]
    gmax = jnp.max(gl, axis=0, keepdims=True)
    g_sel = _first_index(gl == gmax, N_EGROUPS)
    p_sel = 1.0 / jnp.sum(jnp.exp(gl - gmax), axis=0, keepdims=True)
    e_in = logits[N_EGROUPS:N_EGROUPS + EXPERTS_PER_GROUP, :]
    for g in range(1, N_EGROUPS):
        lo = N_EGROUPS + g * EXPERTS_PER_GROUP
        e_in = jnp.where(g_sel == g, logits[lo:lo + EXPERTS_PER_GROUP, :], e_in)
    m1 = jnp.max(e_in, axis=0, keepdims=True)
    i1 = _first_index(e_in == m1, EXPERTS_PER_GROUP)
    sub = lax.broadcasted_iota(I32, e_in.shape, 0)
    rest = jnp.where(sub == i1, -jnp.inf, e_in)
    m2 = jnp.max(rest, axis=0, keepdims=True)
    i2 = _first_index(rest == m2, EXPERTS_PER_GROUP)
    e2 = jnp.exp(m2 - m1)
    denom = 1.0 + e2
    gate_ref[0:1, :] = (1.0 / denom) * p_sel
    gate_ref[1:2, :] = (e2 / denom) * p_sel
    eid0 = g_sel * EXPERTS_PER_GROUP + i1
    eid1 = g_sel * EXPERTS_PER_GROUP + i2

    erow = lax.broadcasted_iota(I32, (N_EXPERTS, tq), 0)
    oh0 = erow == eid0
    oh1 = erow == eid1
    oh0f = oh0.astype(F32)
    oh1f = oh1.astype(F32)
    cum0 = jnp.dot(oh0f.astype(BF16), upper_ref[...], preferred_element_type=F32)
    cum1 = jnp.dot(oh1f.astype(BF16), upper_ref[...], preferred_element_type=F32)
    base = base_ref[:, 0:1]
    tot0 = jnp.sum(oh0f, axis=1, keepdims=True)
    tot1 = jnp.sum(oh1f, axis=1, keepdims=True)
    r0 = jnp.sum(jnp.where(oh0, base + cum0, 0.0), axis=0, keepdims=True)
    r1 = jnp.sum(jnp.where(oh1, base + tot0 + cum1, 0.0), axis=0, keepdims=True)
    sub = tq // LANES
    for k, (e, r) in enumerate(((eid0, r0.astype(I32)), (eid1, r1.astype(I32)))):
        for rr in range(sub):
            row = k * sub + rr
            eid_ref[0, row:row + 1, :] = e[:, rr * LANES:(rr + 1) * LANES]
            rank_ref[0, row:row + 1, :] = r[:, rr * LANES:(rr + 1) * LANES]
    new_base = base + tot0 + tot1
    base_ref[...] = jnp.broadcast_to(new_base, base_ref.shape)
    cnt_ref[...] = jnp.broadcast_to(new_base, cnt_ref.shape).astype(I32)


def moe_router(h, nw, w_group, b_group, w_expert, b_expert):
    t, d = h.shape
    tq = ROUTE_TILE
    nr = N_EGROUPS + N_EXPERTS
    wr = jnp.pad(jnp.concatenate([w_group, w_expert], axis=1).T, ((0, LANES - nr), (0, 0)))
    br = jnp.pad(jnp.concatenate([b_group, b_expert]), (0, LANES - nr))
    br = jnp.broadcast_to(br[:, None], (LANES, LANES))
    upper = (jnp.arange(tq)[:, None] < jnp.arange(tq)[None, :]).astype(BF16)
    rows = 2 * tq // LANES
    tile_spec = pl.BlockSpec((1, rows, LANES), lambda i: (i, 0, 0))
    tile_shape = jax.ShapeDtypeStruct((t // tq, rows, LANES), I32)
    eid, rank, gate, cnt = pl.pallas_call(
        _router_kernel,
        grid=(t // tq,),
        in_specs=[pl.BlockSpec((tq, d), lambda i: (i, 0)),
                  pl.BlockSpec((1, d), lambda i: (0, 0)),
                  pl.BlockSpec((LANES, d), lambda i: (0, 0)),
                  pl.BlockSpec((LANES, LANES), lambda i: (0, 0)),
                  pl.BlockSpec((tq, tq), lambda i: (0, 0))],
        out_specs=[tile_spec, tile_spec, pl.BlockSpec((2, tq), lambda i: (0, i)),
                   pl.BlockSpec((N_EXPERTS, LANES), lambda i: (0, 0))],
        out_shape=[tile_shape, tile_shape, jax.ShapeDtypeStruct((2, t), F32),
                   jax.ShapeDtypeStruct((N_EXPERTS, LANES), I32)],
        scratch_shapes=[pltpu.VMEM((N_EXPERTS, LANES), F32)],
        compiler_params=_cparams("arbitrary"),
        name="moe_router",
    )(h, nw.reshape(1, d), wr, br, upper)
    return eid, rank, gate, cnt[:, 0]


def _slot_kernel(pstart_ref, eid_ref, rank_ref, pos_ref):
    eid = eid_ref[...]

    def body(e, acc):
        return jnp.where(eid == e, pstart_ref[e], acc)

    pos_ref[...] = rank_ref[...] + lax.fori_loop(0, N_EXPERTS, body, jnp.zeros(eid.shape, I32))


def moe_slots(pstart, eid_tiles, rank_tiles):
    nt, rows, lanes = eid_tiles.shape
    tiles_per_step = math.gcd(nt, 16)
    blk = pl.BlockSpec((tiles_per_step, rows, lanes), lambda i, ps: (i, 0, 0))
    return pl.pallas_call(
        _slot_kernel,
        grid_spec=pltpu.PrefetchScalarGridSpec(num_scalar_prefetch=1, grid=(nt // tiles_per_step,),
                                               in_specs=[blk, blk], out_specs=blk),
        out_shape=jax.ShapeDtypeStruct(eid_tiles.shape, I32),
        compiler_params=_cparams("arbitrary"),
        name="moe_slots",
    )(pstart, eid_tiles, rank_tiles)


def _row_copy(src, src_row, dst, dst_row, sem):
    return pltpu.make_async_copy(src.at[pl.ds(src_row, 1)], dst.at[pl.ds(dst_row, 1)], sem)


def _dispatch_kernel(pos_hbm, h_ref, xb_in, xb_out, pos_smem, sem_idx, sem_rows):
    del xb_in
    tq = h_ref.shape[0]
    sub = tq // LANES
    i = pl.program_id(0)
    cp = pltpu.make_async_copy(pos_hbm.at[i], pos_smem, sem_idx)
    cp.start()
    cp.wait()
    for rr in range(sub):
        def body(cc, carry, rr=rr):
            r = rr * LANES + cc
            _row_copy(h_ref, r, xb_out, pos_smem[rr, cc], sem_rows).start()
            _row_copy(h_ref, r, xb_out, pos_smem[sub + rr, cc], sem_rows).start()
            return carry
        lax.fori_loop(0, LANES, body, 0)
    pltpu.make_async_copy(h_ref, xb_out.at[pl.ds(0, tq)], sem_rows).wait()
    pltpu.make_async_copy(h_ref, xb_out.at[pl.ds(0, tq)], sem_rows).wait()


def moe_dispatch(h, pos_tiles, n_pad):
    t, d = h.shape
    tq = ROUTE_TILE
    xb0 = jnp.zeros((n_pad, d), F32)
    return pl.pallas_call(
        _dispatch_kernel,
        grid=(t // tq,),
        in_specs=[pl.BlockSpec(memory_space=pl.ANY),
                  pl.BlockSpec((tq, d), lambda i: (i, 0)),
                  pl.BlockSpec(memory_space=pl.ANY)],
        out_specs=pl.BlockSpec(memory_space=pl.ANY),
        out_shape=jax.ShapeDtypeStruct((n_pad, d), F32),
        scratch_shapes=[pltpu.SMEM((2 * tq // LANES, LANES), I32),
                        pltpu.SemaphoreType.DMA, pltpu.SemaphoreType.DMA],
        input_output_aliases={2: 0},
        compiler_params=_cparams("arbitrary"),
        name="moe_dispatch",
    )(pos_tiles, h, xb0)


def _expert_kernel(be_ref, nu_ref, x_ref, nw_ref, w1_ref, w3_ref, w2_ref, o_ref):
    del be_ref
    b = pl.program_id(0)

    @pl.when(b < nu_ref[0])
    def _():
        xb = _rms(x_ref[...], nw_ref[...]).astype(BF16)
        h1 = jnp.dot(xb, w1_ref[0], preferred_element_type=F32)
        h3 = jnp.dot(xb, w3_ref[0], preferred_element_type=F32)
        hid = (_silu(h1) * h3).astype(BF16)
        o_ref[...] = jnp.dot(hid, w2_ref[0], preferred_element_type=F32)

    @pl.when(b >= nu_ref[0])
    def _():
        o_ref[...] = jnp.zeros(o_ref.shape, F32)


def moe_experts(xb, nw, w1, w3, w2, blk_expert, n_used):
    n_pad, d = xb.shape
    de = w1.shape[2]
    nb = n_pad // MOE_BLOCK
    grid_spec = pltpu.PrefetchScalarGridSpec(
        num_scalar_prefetch=2,
        grid=(nb,),
        in_specs=[pl.BlockSpec((MOE_BLOCK, d), lambda b, be, nu: (b, 0)),
                  pl.BlockSpec((1, d), lambda b, be, nu: (0, 0)),
                  pl.BlockSpec((1, d, de), lambda b, be, nu: (be[b], 0, 0)),
                  pl.BlockSpec((1, d, de), lambda b, be, nu: (be[b], 0, 0)),
                  pl.BlockSpec((1, de, d), lambda b, be, nu: (be[b], 0, 0))],
        out_specs=pl.BlockSpec((MOE_BLOCK, d), lambda b, be, nu: (b, 0)),
    )
    return pl.pallas_call(
        _expert_kernel,
        grid_spec=grid_spec,
        out_shape=jax.ShapeDtypeStruct((n_pad, d), F32),
        compiler_params=_cparams("arbitrary"),
        name="moe_experts",
    )(blk_expert, n_used, xb, nw.reshape(1, d), w1, w3, w2)


def _combine_ple_kernel(pos_hbm, yb_hbm, h_ref, gate_ref, p_ref, nw_ref, wg_ref, wp_ref, fw_ref,
                        o_ref, pos_smem, buf0_ref, buf1_ref, sem_idx, sem_rows, *, final):
    tq = h_ref.shape[0]
    sub = tq // LANES
    i = pl.program_id(0)
    cp = pltpu.make_async_copy(pos_hbm.at[i], pos_smem, sem_idx)
    cp.start()
    cp.wait()
    for rr in range(sub):
        def body(cc, carry, rr=rr):
            r = rr * LANES + cc
            _row_copy(yb_hbm, pos_smem[rr, cc], buf0_ref, r, sem_rows).start()
            _row_copy(yb_hbm, pos_smem[sub + rr, cc], buf1_ref, r, sem_rows).start()
            return carry
        lax.fori_loop(0, LANES, body, 0)
    pltpu.make_async_copy(yb_hbm.at[pl.ds(0, tq)], buf0_ref, sem_rows).wait()
    pltpu.make_async_copy(yb_hbm.at[pl.ds(0, tq)], buf1_ref, sem_rows).wait()

    gate = gate_ref[...]
    h2 = h_ref[...] + gate[:, 0:1] * buf0_ref[...] + gate[:, 1:2] * buf1_ref[...]
    hn = _rms(h2, nw_ref[...]).astype(BF16)
    pg = jax.nn.sigmoid(jnp.dot(hn, wg_ref[...], preferred_element_type=F32))
    pp = jnp.dot(p_ref[...].astype(BF16), wp_ref[...], preferred_element_type=F32)
    out = h2 + pg * pp
    if final:
        out = _rms(out, fw_ref[...])
    o_ref[...] = out


def moe_combine_ple(h, yb, pos_tiles, gate_cols, p, nw, wg, wp, fw, *, final):
    t, d = h.shape
    tq = ROUTE_TILE
    pd = p.shape[1]
    return pl.pallas_call(
        functools.partial(_combine_ple_kernel, final=final),
        grid=(t // tq,),
        in_specs=[pl.BlockSpec(memory_space=pl.ANY),
                  pl.BlockSpec(memory_space=pl.ANY),
                  pl.BlockSpec((tq, d), lambda i: (i, 0)),
                  pl.BlockSpec((tq, 2), lambda i: (i, 0)),
                  pl.BlockSpec((tq, pd), lambda i: (i, 0)),
                  pl.BlockSpec((1, d), lambda i: (0, 0)),
                  pl.BlockSpec((d, d), lambda i: (0, 0)),
                  pl.BlockSpec((pd, d), lambda i: (0, 0)),
                  pl.BlockSpec((1, d), lambda i: (0, 0))],
        out_specs=pl.BlockSpec((tq, d), lambda i: (i, 0)),
        out_shape=jax.ShapeDtypeStruct((t, d), F32),
        scratch_shapes=[pltpu.SMEM((2 * tq // LANES, LANES), I32),
                        pltpu.VMEM((tq, d), F32), pltpu.VMEM((tq, d), F32),
                        pltpu.SemaphoreType.DMA, pltpu.SemaphoreType.DMA],
        compiler_params=_cparams("arbitrary"),
        name="moe_combine_ple",
    )(pos_tiles, yb, h, gate_cols, p, nw.reshape(1, d), wg, wp, fw.reshape(1, d))


def moe_ple_layer(h, p, norm_moe, w_group, b_group, w_expert, b_expert, w1, w3, w2,
                  norm_ple, wg, wp, fw, *, final):
    t, d = h.shape
    eid_tiles, rank_tiles, gate, counts = moe_router(h, norm_moe, w_group, b_group, w_expert,
                                                     b_expert)
    padded = (counts + MOE_BLOCK - 1) // MOE_BLOCK * MOE_BLOCK
    pend = jnp.cumsum(padded)
    pstart = pend - padded
    na = 2 * t
    n_pad = (na + MOE_BLOCK - 1) // MOE_BLOCK * MOE_BLOCK + N_EXPERTS * MOE_BLOCK
    nb = n_pad // MOE_BLOCK
    blk_start = jnp.arange(nb, dtype=I32) * MOE_BLOCK
    blk_expert = jnp.minimum(jnp.sum(blk_start[:, None] >= pend[None, :], axis=-1),
                             N_EXPERTS - 1).astype(I32)
    n_used = (pend[-1:] // MOE_BLOCK).astype(I32)
    pos_tiles = moe_slots(pstart.astype(I32), eid_tiles, rank_tiles)
    xb = moe_dispatch(h, pos_tiles, n_pad)
    yb = moe_experts(xb, norm_moe, w1.astype(BF16), w3.astype(BF16), w2.astype(BF16),
                     blk_expert, n_used)
    return moe_combine_ple(h, yb, pos_tiles, gate.T, p, norm_ple, wg.astype(BF16),
                           wp.astype(BF16), fw, final=final)


def _pad_cols(w, n):
    return jnp.pad(w, ((0, 0), (0, n - w.shape[1])))


def kernel(x, p, norm_mix, norm_moe, norm_ple, final_norm, m_in_w, m_conv_w, m_conv_b, m_dt_bias, m_A_log, m_D, m_norm_w, m_out_w, g_in_w, g_conv_w, g_dt_bias, g_A_log, g_norm_w, g_out_w, moe_w_group, moe_b_group, moe_w_expert, moe_b_expert, moe_w1, moe_w3, moe_w2, ple_w_proj, ple_w_gate):
    batch, seq, d = x.shape
    t = batch * seq
    depth = p.shape[0]
    pd = p.shape[-1]
    h = x.reshape(t, d)
    p2 = p.reshape(depth, t, pd)
    for i in range(depth):
        j = i // 2
        if i % 2 == 0:
            inner = m_out_w.shape[1]
            heads = m_dt_bias.shape[1]
            conv_dim = m_conv_w.shape[2]
            w_in = _pad_cols(m_in_w[j], inner + conv_dim + LANES).astype(BF16)
            proj = norm_matmul(h, norm_mix[i], w_in)
            y = ssd_mixer_core(proj, m_conv_w[j], m_conv_b[j], m_dt_bias[j], m_A_log[j], m_D[j],
                               m_norm_w[j], batch=batch, seq=seq, inner=inner, heads=heads)
            h = matmul_residual(y, m_out_w[j].astype(BF16), h)
        else:
            v_dim = g_out_w.shape[1]
            v_heads = g_dt_bias.shape[1]
            conv_dim = g_conv_w.shape[2]
            qk_dim = (conv_dim - v_dim) // 2
            w_in = _pad_cols(g_in_w[j], conv_dim + v_dim + LANES).astype(BF16)
            proj = norm_matmul(h, norm_mix[i], w_in)
            y = gdn_mixer_core(proj, g_conv_w[j], g_dt_bias[j], g_A_log[j], g_norm_w[j],
                               batch=batch, seq=seq, qk_dim=qk_dim, v_dim=v_dim, v_heads=v_heads)
            h = matmul_residual(y, g_out_w[j].astype(BF16), h)
        h = moe_ple_layer(h, p2[i], norm_moe[i], moe_w_group[i], moe_b_group[i], moe_w_expert[i],
                          moe_b_expert[i], moe_w1[i], moe_w3[i], moe_w2[i], norm_ple[i],
                          ple_w_gate[i], ple_w_proj[i], final_norm, final=(i == depth - 1))
    return h.reshape(batch, seq, d)
```

```python
import functools
import math

import jax
import jax.numpy as jnp
from jax import lax
from jax.experimental import pallas as pl
from jax.experimental.pallas import tpu as pltpu

F32 = jnp.float32
BF16 = jnp.bfloat16
I32 = jnp.int32
EPS = 1e-6
HIGHEST = lax.Precision.HIGHEST

LANES = 128
SUBLANES = 8
VMEM_LIMIT = 56 * 1024 * 1024

CONV_K = 4
SSD_CHUNK = 128
SSD_HEADDIM = 64
SSD_STATE = 128
SSD_GROUPS = 4
GDN_CHUNK = 64
GDN_HEAD = 128
GDN_STACK = 256
GDN_SUBBLOCK = 16
GDN_BATCH_PER_STEP = 1
N_EGROUPS = 8
EXPERTS_PER_GROUP = 8
N_EXPERTS = N_EGROUPS * EXPERTS_PER_GROUP
MOE_BLOCK = 256
ROUTE_TILE = 512


def _cparams(*sem):
    return pltpu.CompilerParams(dimension_semantics=sem, vmem_limit_bytes=VMEM_LIMIT)


def _silu(x):
    return x * jax.nn.sigmoid(x)


def _softplus(x):
    return jnp.maximum(x, 0.0) + jnp.log1p(jnp.exp(-jnp.abs(x)))


def _rms(x, w):
    return x * lax.rsqrt(jnp.mean(x * x, axis=-1, keepdims=True) + EPS) * w


def _dot(a, b):
    return jnp.dot(a.astype(BF16), b.astype(BF16), preferred_element_type=F32)


def _split(x):
    hi = x.astype(BF16)
    return hi, (x - hi.astype(F32)).astype(BF16)


def _dot_f32(a, b):
    return jnp.dot(a, b, preferred_element_type=F32, precision=HIGHEST)


def _dot_tn(a, b):
    return lax.dot_general(a.astype(BF16), b.astype(BF16), (((0,), (0,)), ((), ())),
                           preferred_element_type=F32)


def _dot_nt(a, b):
    return lax.dot_general(a.astype(BF16), b.astype(BF16), (((1,), (1,)), ((), ())),
                           preferred_element_type=F32)


def _tri(n, strict=False):
    r = lax.broadcasted_iota(I32, (n, n), 0)
    c = lax.broadcasted_iota(I32, (n, n), 1)
    return (r > c) if strict else (r >= c)


def _norm_matmul_kernel(x_ref, nw_ref, w_ref, o_ref, *, n_chunk):
    xb = _rms(x_ref[...], nw_ref[...]).astype(BF16)
    n = o_ref.shape[1]
    for c0 in range(0, n, n_chunk):
        c1 = min(c0 + n_chunk, n)
        o_ref[:, c0:c1] = jnp.dot(xb, w_ref[:, c0:c1], preferred_element_type=F32)


def norm_matmul(x, nw, w, *, tm=256, n_chunk=512):
    t, d = x.shape
    n = w.shape[1]
    return pl.pallas_call(
        functools.partial(_norm_matmul_kernel, n_chunk=n_chunk),
        grid=(t // tm,),
        in_specs=[pl.BlockSpec((tm, d), lambda i: (i, 0)),
                  pl.BlockSpec((1, d), lambda i: (0, 0)),
                  pl.BlockSpec((d, n), lambda i: (0, 0))],
        out_specs=pl.BlockSpec((tm, n), lambda i: (i, 0)),
        out_shape=jax.ShapeDtypeStruct((t, n), F32),
        compiler_params=_cparams("arbitrary"),
        name="norm_matmul",
    )(x, nw.reshape(1, d), w)


def _matmul_residual_kernel(y_ref, w_ref, r_ref, o_ref):
    o_ref[...] = r_ref[...] + jnp.dot(y_ref[...].astype(BF16), w_ref[...],
                                      preferred_element_type=F32)


def matmul_residual(y, w, res, *, tm=512):
    t, k = y.shape
    d = w.shape[1]
    return pl.pallas_call(
        _matmul_residual_kernel,
        grid=(t // tm,),
        in_specs=[pl.BlockSpec((tm, k), lambda i: (i, 0)),
                  pl.BlockSpec((k, d), lambda i: (0, 0)),
                  pl.BlockSpec((tm, d), lambda i: (i, 0))],
        out_specs=pl.BlockSpec((tm, d), lambda i: (i, 0)),
        out_shape=jax.ShapeDtypeStruct((t, d), F32),
        compiler_params=_cparams("arbitrary"),
        name="matmul_residual",
    )(y, w, res)


def _conv_silu(x_ref, stage_ref, w_ref, bias, first):
    q = x_ref.shape[0]

    @pl.when(first)
    def _():
        stage_ref[0:SUBLANES, :] = jnp.zeros((SUBLANES, stage_ref.shape[1]), F32)

    stage_ref[SUBLANES:SUBLANES + q, :] = x_ref[...]
    acc = stage_ref[SUBLANES:SUBLANES + q, :] * w_ref[CONV_K - 1:CONV_K, :]
    for j in range(CONV_K - 1):
        off = SUBLANES - (CONV_K - 1) + j
        acc = acc + stage_ref[off:off + q, :] * w_ref[j:j + 1, :]
    if bias is not None:
        acc = acc + bias
    stage_ref[0:SUBLANES, :] = stage_ref[q:q + SUBLANES, :]
    return _silu(acc)


def _ssd_kernel(z_ref, x_ref, b_ref, c_ref, dt_ref,
                cwx_ref, cwb_ref, cwc_ref, cbx_ref, cbb_ref, cbc_ref,
                dtb_ref, alog_ref, dfull_ref, nw_ref,
                o_ref,
                sx_ref, sb_ref, sc_ref, y_ref, xw_ref, st_ref):
    q = x_ref.shape[0]
    hpg = x_ref.shape[1] // (SSD_GROUPS * SSD_HEADDIM)
    first = pl.program_id(1) == 0

    @pl.when(first)
    def _():
        st_ref[...] = jnp.zeros(st_ref.shape, F32)

    xs = _conv_silu(x_ref, sx_ref, cwx_ref, cbx_ref[...], first)
    bm = _conv_silu(b_ref, sb_ref, cwb_ref, cbb_ref[...], first)
    cm = _conv_silu(c_ref, sc_ref, cwc_ref, cbc_ref[...], first)

    dt = _softplus(dt_ref[...] + dtb_ref[...])
    da = dt * (-jnp.exp(alog_ref[...]))
    causal = _tri(q)
    a = _dot_f32(causal.astype(F32), da)
    a_t = a.T
    ea = jnp.exp(a)
    a_last = a[q - 1:q, :]
    to_end = jnp.exp(a_last - a) * dt
    ea_last = jnp.exp(a_last)
    dt_t = dt.T

    for g in range(SSD_GROUPS):
        bg = bm[:, g * SSD_STATE:(g + 1) * SSD_STATE]
        cg = cm[:, g * SSD_STATE:(g + 1) * SSD_STATE]
        cb = _dot_nt(cg, bg)
        gw = hpg * SSD_HEADDIM
        y_inter = _dot(cg, st_ref[g])
        for jj in range(hpg):
            j = g * hpg + jj
            lo, hi = j * SSD_HEADDIM, (j + 1) * SSD_HEADDIM
            xj = xs[:, lo:hi]
            diff = a[:, j:j + 1] - a_t[j:j + 1, :]
            seg = jnp.exp(jnp.where(causal, diff, -jnp.inf))
            m = cb * seg * dt_t[j:j + 1, :]
            yj = _dot(m, xj) + y_inter[:, jj * SSD_HEADDIM:(jj + 1) * SSD_HEADDIM] * ea[:, j:j + 1]
            y_ref[:, lo:hi] = yj
            xw_ref[:, lo:hi] = xj * to_end[:, j:j + 1]
        upd = _dot_tn(bg, xw_ref[:, g * gw:(g + 1) * gw])
        for jj in range(hpg):
            j = g * hpg + jj
            sl = slice(jj * SSD_HEADDIM, (jj + 1) * SSD_HEADDIM)
            st_ref[g, :, sl] = st_ref[g, :, sl] * ea_last[:, j:j + 1] + upd[:, sl]

    y = y_ref[...] + dfull_ref[...] * xs
    o_ref[...] = _rms(y * _silu(z_ref[...]), nw_ref[...])


def ssd_mixer_core(proj, conv_w, conv_b, dt_bias, a_log, d_skip, norm_w, *, batch, seq,
                   inner, heads):
    t = proj.shape[0]
    q = SSD_CHUNK
    gn = SSD_GROUPS * SSD_STATE
    nc = seq // q
    cwx, cwb, cwc = conv_w[:, :inner], conv_w[:, inner:inner + gn], conv_w[:, inner + gn:]
    cbx = conv_b[:inner].reshape(1, inner)
    cbb = conv_b[inner:inner + gn].reshape(1, gn)
    cbc = conv_b[inner + gn:].reshape(1, gn)
    pad = LANES - heads
    dtb = jnp.pad(dt_bias, (0, pad)).reshape(1, LANES)
    alog = jnp.pad(a_log, (0, pad)).reshape(1, LANES)
    dfull = jnp.repeat(d_skip, SSD_HEADDIM).reshape(1, inner)
    nw = norm_w.reshape(1, inner)

    def row(i, c):
        return i * nc + c

    const = lambda shape: pl.BlockSpec(shape, lambda i, c: (0, 0))
    x_blk = inner // inner
    b_blk = (2 * inner) // gn
    c_blk = b_blk + 1
    dt_blk = (2 * inner + 2 * gn) // LANES
    return pl.pallas_call(
        _ssd_kernel,
        grid=(batch, nc),
        in_specs=[pl.BlockSpec((q, inner), lambda i, c: (row(i, c), 0)),
                  pl.BlockSpec((q, inner), lambda i, c: (row(i, c), x_blk)),
                  pl.BlockSpec((q, gn), lambda i, c: (row(i, c), b_blk)),
                  pl.BlockSpec((q, gn), lambda i, c: (row(i, c), c_blk)),
                  pl.BlockSpec((q, LANES), lambda i, c: (row(i, c), dt_blk)),
                  const((CONV_K, inner)), const((CONV_K, gn)), const((CONV_K, gn)),
                  const((1, inner)), const((1, gn)), const((1, gn)),
                  const((1, LANES)), const((1, LANES)), const((1, inner)), const((1, inner))],
        out_specs=pl.BlockSpec((q, inner), lambda i, c: (row(i, c), 0)),
        out_shape=jax.ShapeDtypeStruct((t, inner), F32),
        scratch_shapes=[pltpu.VMEM((q + SUBLANES, inner), F32),
                        pltpu.VMEM((q + SUBLANES, gn), F32),
                        pltpu.VMEM((q + SUBLANES, gn), F32),
                        pltpu.VMEM((q, inner), F32),
                        pltpu.VMEM((q, inner), F32),
                        pltpu.VMEM((SSD_GROUPS, SSD_STATE, inner // SSD_GROUPS), F32)],
        compiler_params=_cparams("arbitrary", "arbitrary"),
        name="ssd_scan",
    )(proj, proj, proj, proj, proj, cwx, cwb, cwc, cbx, cbb, cbc, dtb, alog, dfull, nw)


def _l2n(x):
    return x * lax.rsqrt(jnp.sum(x * x, axis=-1, keepdims=True) + EPS)


def _pair_cols(x, i0, i1, half):
    c = x.shape[0]
    lane = lax.broadcasted_iota(I32, (c, 2 * half), 1)
    return jnp.where(lane < half, jnp.broadcast_to(x[:, i0:i0 + 1], (c, 2 * half)),
                     jnp.broadcast_to(x[:, i1:i1 + 1], (c, 2 * half)))


def _gdn_kernel(q_ref, k_ref, v_ref, z_ref, ba_ref,
                cwq_ref, cwk_ref, cwv_ref, dtb_ref, alog_ref, nw_ref,
                o_ref,
                sq_ref, sk_ref, sv_ref, s_ref, *, v_heads):
    nb, c = q_ref.shape[0], q_ref.shape[1]
    qk_heads = q_ref.shape[2] // GDN_HEAD
    rep = v_heads // qk_heads
    gsz = GDN_STACK // c
    n_grp = v_heads // gsz
    first = pl.program_id(1) == 0

    @pl.when(first)
    def _():
        s_ref[...] = jnp.zeros(s_ref.shape, F32)

    t_idx = lax.broadcasted_iota(I32, (c, GDN_STACK), 0)
    s_idx = lax.broadcasted_iota(I32, (c, GDN_STACK), 1) & (c - 1)
    causal = t_idx >= s_idx
    strict = t_idx > s_idx
    eye = t_idx == s_idx
    eye_f = eye.astype(F32)
    sb_shift = GDN_SUBBLOCK.bit_length() - 1
    same_blk = (t_idx >> sb_shift) == (s_idx >> sb_shift)
    c_shift = c.bit_length() - 1
    head_diag = ((lax.broadcasted_iota(I32, (GDN_STACK, GDN_STACK), 0) >> c_shift)
                 == (lax.broadcasted_iota(I32, (GDN_STACK, GDN_STACK), 1) >> c_shift))

    def bdiag(parts):
        zero = jnp.zeros((), BF16)
        return tuple(jnp.where(head_diag, jnp.concatenate([p] * gsz, axis=0), zero) for p in parts)

    def mm(lhs, rhs_bd):
        n = len(lhs) * c
        parts = [_split(x) for x in lhs]
        l_hi = jnp.concatenate([p[0] for p in parts], axis=0)
        l_lo = jnp.concatenate([p[1] for p in parts], axis=0)
        r = jnp.dot(jnp.concatenate([l_hi, l_lo], axis=0), rhs_bd[0], preferred_element_type=F32)
        out = r[:n] + r[n:] + jnp.dot(l_hi, rhs_bd[1], preferred_element_type=F32)
        return [out[i * c:(i + 1) * c] for i in range(len(lhs))]

    def neumann(p0s, n_sq):
        ps = list(p0s)
        ds = [eye_f + p for p in ps]
        for i in range(n_sq):
            for g in range(len(ps)):
                bd = bdiag(_split(ps[g]))
                if i == 0:
                    ps[g], = mm([ps[g]], bd)
                else:
                    ps[g], x = mm([ps[g], ds[g]], bd)
                    ds[g] = ds[g] + x
        if n_sq > 0:
            for g in range(len(ps)):
                x, = mm([ds[g]], bdiag(_split(ps[g])))
                ds[g] = ds[g] + x
        return ds

    def prepare(b):
        qc = _conv_silu(q_ref.at[b], sq_ref.at[b], cwq_ref, None, first)
        kc = _conv_silu(k_ref.at[b], sk_ref.at[b], cwk_ref, None, first)
        vc = _conv_silu(v_ref.at[b], sv_ref.at[b], cwv_ref, None, first)
        ba = ba_ref[b]
        beta = jax.nn.sigmoid(ba)
        gl = -jnp.exp(alog_ref[...]) * _softplus(ba + dtb_ref[...])
        gcs = _dot_f32(_tri(c).astype(F32), gl)
        g_last = gcs[c - 1:c, :]
        st = dict(vc=vc, eg_last=jnp.exp(g_last), qns=[], kns=[], kts=[], a_kks=[], a_qks=[],
                  gcols=[], brows=[], grows=[])
        kks, qks = [], []
        for hq in range(qk_heads):
            sl = slice(hq * GDN_HEAD, (hq + 1) * GDN_HEAD)
            qn = _l2n(qc[:, sl]) * (GDN_HEAD ** -0.5)
            kn = _l2n(kc[:, sl])
            k2 = jnp.concatenate([kn] * rep, axis=0)
            st["qns"].append(qn)
            st["kns"].append(kn)
            st["kts"].append(kn.T.astype(BF16))
            kks.append(_dot_nt(kn, k2))
            qks.append(_dot_nt(qn, k2))
        for grp in range(n_grp):
            h0 = grp * gsz
            hq0 = h0 // rep
            nq = gsz // rep
            kk = jnp.concatenate(kks[hq0:hq0 + nq], axis=1)
            qk = jnp.concatenate(qks[hq0:hq0 + nq], axis=1)
            gcol = jnp.concatenate(
                [_pair_cols(gcs, v_heads + h0 + 2 * i, v_heads + h0 + 2 * i + 1, c)
                 for i in range(gsz // 2)], axis=1)
            bcol = jnp.concatenate([_pair_cols(beta, h0 + 2 * i, h0 + 2 * i + 1, c)
                                    for i in range(gsz // 2)], axis=1)
            grow = jnp.sum(jnp.where(eye, gcol, 0.0), axis=0, keepdims=True)
            decay = jnp.exp(jnp.where(causal, gcol - grow, -jnp.inf))
            st["a_kks"].append(jnp.where(strict, bcol * kk * decay, 0.0))
            st["a_qks"].append(jnp.where(causal, qk * decay, 0.0))
            st["gcols"].append(gcol)
            st["grows"].append(grow)
            st["brows"].append(jnp.sum(jnp.where(eye, bcol, 0.0), axis=0, keepdims=True))
        return st

    def bdiag_blocks(blocks):
        z = jnp.zeros((c, GDN_HEAD), BF16)
        rows = [jnp.concatenate([blocks[i].astype(BF16) if j == i else z for j in range(gsz)],
                                axis=1) for i in range(gsz)]
        return jnp.concatenate(rows, axis=0)

    def heads(b, st, grp, tinv):
        h0 = grp * gsz
        gcol, grow, brow = st["gcols"][grp], st["grows"][grp], st["brows"][grp]
        hd = lambda x, i: x[:, i * GDN_HEAD:(i + 1) * GDN_HEAD]
        u = _dot(tinv * brow, bdiag_blocks([hd(st["vc"], h0 + i) for i in range(gsz)]))
        w = _dot(tinv * (brow * jnp.exp(grow)),
                 bdiag_blocks([st["kns"][(h0 + i) // rep] for i in range(gsz)]))
        v_news, q_states = [], []
        for i in range(gsz):
            hq = (h0 + i) // rep
            ws = _dot(jnp.concatenate([hd(w, i), st["qns"][hq]], axis=0), s_ref[b, h0 + i])
            v_news.append(hd(u, i) - ws[:c])
            q_states.append(ws[c:])
        eg_diag = jnp.where(eye, jnp.exp(gcol), 0.0)
        eg_end_diag = jnp.where(eye, jnp.exp(gcol[c - 1:c, :] - gcol), 0.0)
        bd_v = bdiag_blocks(v_news)
        o = _dot(jnp.concatenate([st["a_qks"][grp], eg_diag], axis=1),
                 jnp.concatenate([bd_v, bdiag_blocks(q_states)], axis=0))
        v_scaled = _dot(eg_end_diag, bd_v)
        for i in range(gsz):
            h = h0 + i
            gi = v_heads + h
            hs = slice(h * GDN_HEAD, (h + 1) * GDN_HEAD)
            s_ref[b, h] = (s_ref[b, h] * st["eg_last"][:, gi:gi + 1]
                           + _dot(st["kts"][h // rep], hd(v_scaled, i)))
            o_ref[b, :, hs] = _rms(hd(o, i), nw_ref[...]) * _silu(z_ref[b, :, hs])

    sts = [prepare(b) for b in range(nb)]

    a_all = [a for st in sts for a in st["a_kks"]]
    dms = neumann([jnp.where(same_blk, -a, 0.0) for a in a_all], sb_shift - 1)
    ms = [-mm([d], bdiag(_split(jnp.where(same_blk, 0.0, a))))[0] for d, a in zip(dms, a_all)]
    ws = neumann(ms, (c // GDN_SUBBLOCK).bit_length() - 2)
    tinvs = [mm([w], bdiag(_split(d)))[0] for w, d in zip(ws, dms)]

    for grp in range(n_grp):
        for b in range(nb):
            heads(b, sts[b], grp, tinvs[b * n_grp + grp])


def gdn_mixer_core(proj, conv_w, dt_bias, a_log, norm_w, *, batch, seq, qk_dim, v_dim, v_heads):
    t, n = proj.shape
    c = GDN_CHUNK
    nb = GDN_BATCH_PER_STEP
    cwq, cwk, cwv = conv_w[:, :qk_dim], conv_w[:, qk_dim:2 * qk_dim], conv_w[:, 2 * qk_dim:]
    dtb = jnp.pad(dt_bias, (v_heads, LANES - 2 * v_heads)).reshape(1, LANES)
    alog = jnp.pad(a_log, (v_heads, LANES - 2 * v_heads)).reshape(1, LANES)
    nw = norm_w.reshape(1, GDN_HEAD)
    proj3 = proj.reshape(batch, seq, n)

    const = lambda shape: pl.BlockSpec(shape, lambda i, cc: (0, 0))
    blk = lambda width, col: pl.BlockSpec((nb, c, width), lambda i, cc: (i, cc, col))
    v_blk = (2 * qk_dim) // v_dim
    z_blk = v_blk + 1
    ba_blk = (2 * qk_dim + 2 * v_dim) // LANES
    out = pl.pallas_call(
        functools.partial(_gdn_kernel, v_heads=v_heads),
        grid=(batch // nb, seq // c),
        in_specs=[blk(qk_dim, 0), blk(qk_dim, 1), blk(v_dim, v_blk), blk(v_dim, z_blk),
                  blk(LANES, ba_blk),
                  const((CONV_K, qk_dim)), const((CONV_K, qk_dim)), const((CONV_K, v_dim)),
                  const((1, LANES)), const((1, LANES)), const((1, GDN_HEAD))],
        out_specs=blk(v_dim, 0),
        out_shape=jax.ShapeDtypeStruct((batch, seq, v_dim), F32),
        scratch_shapes=[pltpu.VMEM((nb, c + SUBLANES, qk_dim), F32),
                        pltpu.VMEM((nb, c + SUBLANES, qk_dim), F32),
                        pltpu.VMEM((nb, c + SUBLANES, v_dim), F32),
                        pltpu.VMEM((nb, v_heads, GDN_HEAD, GDN_HEAD), F32)],
        compiler_params=_cparams("arbitrary", "arbitrary"),
        name="gdn_scan",
    )(proj3, proj3, proj3, proj3, proj3, cwq, cwk, cwv, dtb, alog, nw)
    return out.reshape(t, v_dim)


def _first_index(mask, n):
    idx = lax.broadcasted_iota(I32, mask.shape, 0)
    return jnp.min(jnp.where(mask, idx, n), axis=0, keepdims=True)


def _router_kernel(h_ref, nw_ref, wr_ref, br_ref, upper_ref,
                   eid_ref, rank_ref, gate_ref, cnt_ref, base_ref):
    tq = h_ref.shape[0]
    i = pl.program_id(0)

    @pl.when(i == 0)
    def _():
        base_ref[...] = jnp.zeros(base_ref.shape, F32)

    xn = _rms(h_ref[...], nw_ref[...])
    logits = lax.dot_general(wr_ref[...], xn, (((1,), (1,)), ((), ())),
                             preferred_element_type=F32, precision=HIGHEST) + br_ref[:, 0:1]
    gl = logits[0:N_EGROUPS, :]
    gmax = jnp.max(gl, axis=0, keepdims=True)
    g_sel = _first_index(gl == gmax, N_EGROUPS)
    p_sel = 1.0 / jnp.sum(jnp.exp(gl - gmax), axis=0, keepdims=True)
    e_in = logits[N_EGROUPS:N_EGROUPS + EXPERTS_PER_GROUP, :]
    for g in range(1, N_EGROUPS):
        lo = N_EGROUPS + g * EXPERTS_PER_GROUP
        e_in = jnp.where(g_sel == g, logits[lo:lo + EXPERTS_PER_GROUP, :], e_in)
    m1 = jnp.max(e_in, axis=0, keepdims=True)
    i1 = _first_index(e_in == m1, EXPERTS_PER_GROUP)
    sub = lax.broadcasted_iota(I32, e_in.shape, 0)
    rest = jnp.where(sub == i1, -jnp.inf, e_in)
    m2 = jnp.max(rest, axis=0, keepdims=True)
    i2 = _first_index(rest == m2, EXPERTS_PER_GROUP)
    e2 = jnp.exp(m2 - m1)
    denom = 1.0 + e2
    gate_ref[0:1, :] = (1.0 / denom) * p_sel
    gate_ref[1:2, :] = (e2 / denom) * p_sel
    eid0 = g_sel * EXPERTS_PER_GROUP + i1
    eid1 = g_sel * EXPERTS_PER_GROUP + i2

    erow = lax.broadcasted_iota(I32, (N_EXPERTS, tq), 0)
    oh0 = erow == eid0
    oh1 = erow == eid1
    oh0f = oh0.astype(F32)
    oh1f = oh1.astype(F32)
    cum0 = jnp.dot(oh0f.astype(BF16), upper_ref[...], preferred_element_type=F32)
    cum1 = jnp.dot(oh1f.astype(BF16), upper_ref[...], preferred_element_type=F32)
    base = base_ref[:, 0:1]
    tot0 = jnp.sum(oh0f, axis=1, keepdims=True)
    tot1 = jnp.sum(oh1f, axis=1, keepdims=True)
    r0 = jnp.sum(jnp.where(oh0, base + cum0, 0.0), axis=0, keepdims=True)
    r1 = jnp.sum(jnp.where(oh1, base + tot0 + cum1, 0.0), axis=0, keepdims=True)
    sub = tq // LANES
    for k, (e, r) in enumerate(((eid0, r0.astype(I32)), (eid1, r1.astype(I32)))):
        for rr in range(sub):
            row = k * sub + rr
            eid_ref[0, row:row + 1, :] = e[:, rr * LANES:(rr + 1) * LANES]
            rank_ref[0, row:row + 1, :] = r[:, rr * LANES:(rr + 1) * LANES]
    new_base = base + tot0 + tot1
    base_ref[...] = jnp.broadcast_to(new_base, base_ref.shape)
    cnt_ref[...] = jnp.broadcast_to(new_base, cnt_ref.shape).astype(I32)


def moe_router(h, nw, w_group, b_group, w_expert, b_expert):
    t, d = h.shape
    tq = ROUTE_TILE
    nr = N_EGROUPS + N_EXPERTS
    wr = jnp.pad(jnp.concatenate([w_group, w_expert], axis=1).T, ((0, LANES - nr), (0, 0)))
    br = jnp.pad(jnp.concatenate([b_group, b_expert]), (0, LANES - nr))
    br = jnp.broadcast_to(br[:, None], (LANES, LANES))
    upper = (jnp.arange(tq)[:, None] < jnp.arange(tq)[None, :]).astype(BF16)
    rows = 2 * tq // LANES
    tile_spec = pl.BlockSpec((1, rows, LANES), lambda i: (i, 0, 0))
    tile_shape = jax.ShapeDtypeStruct((t // tq, rows, LANES), I32)
    eid, rank, gate, cnt = pl.pallas_call(
        _router_kernel,
        grid=(t // tq,),
        in_specs=[pl.BlockSpec((tq, d), lambda i: (i, 0)),
                  pl.BlockSpec((1, d), lambda i: (0, 0)),
                  pl.BlockSpec((LANES, d), lambda i: (0, 0)),
                  pl.BlockSpec((LANES, LANES), lambda i: (0, 0)),
                  pl.BlockSpec((tq, tq), lambda i: (0, 0))],
        out_specs=[tile_spec, tile_spec, pl.BlockSpec((2, tq), lambda i: (0, i)),
                   pl.BlockSpec((N_EXPERTS, LANES), lambda i: (0, 0))],
        out_shape=[tile_shape, tile_shape, jax.ShapeDtypeStruct((2, t), F32),
                   jax.ShapeDtypeStruct((N_EXPERTS, LANES), I32)],
        scratch_shapes=[pltpu.VMEM((N_EXPERTS, LANES), F32)],
        compiler_params=_cparams("arbitrary"),
        name="moe_router",
    )(h, nw.reshape(1, d), wr, br, upper)
    return eid, rank, gate, cnt[:, 0]


def _slot_kernel(pstart_ref, eid_ref, rank_ref, pos_ref):
    eid = eid_ref[...]

    def body(e, acc):
        return jnp.where(eid == e, pstart_ref[e], acc)

    pos_ref[...] = rank_ref[...] + lax.fori_loop(0, N_EXPERTS, body, jnp.zeros(eid.shape, I32))


def moe_slots(pstart, eid_tiles, rank_tiles):
    nt, rows, lanes = eid_tiles.shape
    tiles_per_step = math.gcd(nt, 16)
    blk = pl.BlockSpec((tiles_per_step, rows, lanes), lambda i, ps: (i, 0, 0))
    return pl.pallas_call(
        _slot_kernel,
        grid_spec=pltpu.PrefetchScalarGridSpec(num_scalar_prefetch=1, grid=(nt // tiles_per_step,),
                                               in_specs=[blk, blk], out_specs=blk),
        out_shape=jax.ShapeDtypeStruct(eid_tiles.shape, I32),
        compiler_params=_cparams("arbitrary"),
        name="moe_slots",
    )(pstart, eid_tiles, rank_tiles)


def _row_copy(src, src_row, dst, dst_row, sem):
    return pltpu.make_async_copy(src.at[pl.ds(src_row, 1)], dst.at[pl.ds(dst_row, 1)], sem)


def _dispatch_kernel(pos_hbm, h_ref, xb_in, xb_out, pos_smem, sem_idx, sem_rows):
    del xb_in
    tq = h_ref.shape[0]
    sub = tq // LANES
    i = pl.program_id(0)
    cp = pltpu.make_async_copy(pos_hbm.at[i], pos_smem, sem_idx)
    cp.start()
    cp.wait()
    for rr in range(sub):
        def body(cc, carry, rr=rr):
            r = rr * LANES + cc
            _row_copy(h_ref, r, xb_out, pos_smem[rr, cc], sem_rows).start()
            _row_copy(h_ref, r, xb_out, pos_smem[sub + rr, cc], sem_rows).start()
            return carry
        lax.fori_loop(0, LANES, body, 0)
    pltpu.make_async_copy(h_ref, xb_out.at[pl.ds(0, tq)], sem_rows).wait()
    pltpu.make_async_copy(h_ref, xb_out.at[pl.ds(0, tq)], sem_rows).wait()


def moe_dispatch(h, pos_tiles, n_pad):
    t, d = h.shape
    tq = ROUTE_TILE
    xb0 = jnp.zeros((n_pad, d), F32)
    return pl.pallas_call(
        _dispatch_kernel,
        grid=(t // tq,),
        in_specs=[pl.BlockSpec(memory_space=pl.ANY),
                  pl.BlockSpec((tq, d), lambda i: (i, 0)),
                  pl.BlockSpec(memory_space=pl.ANY)],
        out_specs=pl.BlockSpec(memory_space=pl.ANY),
        out_shape=jax.ShapeDtypeStruct((n_pad, d), F32),
        scratch_shapes=[pltpu.SMEM((2 * tq // LANES, LANES), I32),
                        pltpu.SemaphoreType.DMA, pltpu.SemaphoreType.DMA],
        input_output_aliases={2: 0},
        compiler_params=_cparams("arbitrary"),
        name="moe_dispatch",
    )(pos_tiles, h, xb0)


def _expert_kernel(be_ref, nu_ref, x_ref, nw_ref, w1_ref, w3_ref, w2_ref, o_ref):
    del be_ref
    b = pl.program_id(0)

    @pl.when(b < nu_ref[0])
    def _():
        xb = _rms(x_ref[...], nw_ref[...]).astype(BF16)
        h1 = jnp.dot(xb, w1_ref[0], preferred_element_type=F32)
        h3 = jnp.dot(xb, w3_ref[0], preferred_element_type=F32)
        hid = (_silu(h1) * h3).astype(BF16)
        o_ref[...] = jnp.dot(hid, w2_ref[0], preferred_element_type=F32)

    @pl.when(b >= nu_ref[0])
    def _():
        o_ref[...] = jnp.zeros(o_ref.shape, F32)


def moe_experts(xb, nw, w1, w3, w2, blk_expert, n_used):
    n_pad, d = xb.shape
    de = w1.shape[2]
    nb = n_pad // MOE_BLOCK
    grid_spec = pltpu.PrefetchScalarGridSpec(
        num_scalar_prefetch=2,
        grid=(nb,),
        in_specs=[pl.BlockSpec((MOE_BLOCK, d), lambda b, be, nu: (b, 0)),
                  pl.BlockSpec((1, d), lambda b, be, nu: (0, 0)),
                  pl.BlockSpec((1, d, de), lambda b, be, nu: (be[b], 0, 0)),
                  pl.BlockSpec((1, d, de), lambda b, be, nu: (be[b], 0, 0)),
                  pl.BlockSpec((1, de, d), lambda b, be, nu: (be[b], 0, 0))],
        out_specs=pl.BlockSpec((MOE_BLOCK, d), lambda b, be, nu: (b, 0)),
    )
    return pl.pallas_call(
        _expert_kernel,
        grid_spec=grid_spec,
        out_shape=jax.ShapeDtypeStruct((n_pad, d), F32),
        compiler_params=_cparams("arbitrary"),
        name="moe_experts",
    )(blk_expert, n_used, xb, nw.reshape(1, d), w1, w3, w2)


def _combine_ple_kernel(pos_hbm, yb_hbm, h_ref, gate_ref, p_ref, nw_ref, wg_ref, wp_ref, fw_ref,
                        o_ref, pos_smem, buf0_ref, buf1_ref, sem_idx, sem_rows, *, final):
    tq = h_ref.shape[0]
    sub = tq // LANES
    i = pl.program_id(0)
    cp = pltpu.make_async_copy(pos_hbm.at[i], pos_smem, sem_idx)
    cp.start()
    cp.wait()
    for rr in range(sub):
        def body(cc, carry, rr=rr):
            r = rr * LANES + cc
            _row_copy(yb_hbm, pos_smem[rr, cc], buf0_ref, r, sem_rows).start()
            _row_copy(yb_hbm, pos_smem[sub + rr, cc], buf1_ref, r, sem_rows).start()
            return carry
        lax.fori_loop(0, LANES, body, 0)
    pltpu.make_async_copy(yb_hbm.at[pl.ds(0, tq)], buf0_ref, sem_rows).wait()
    pltpu.make_async_copy(yb_hbm.at[pl.ds(0, tq)], buf1_ref, sem_rows).wait()

    gate = gate_ref[...]
    h2 = h_ref[...] + gate[:, 0:1] * buf0_ref[...] + gate[:, 1:2] * buf1_ref[...]
    hn = _rms(h2, nw_ref[...]).astype(BF16)
    pg = jax.nn.sigmoid(jnp.dot(hn, wg_ref[...], preferred_element_type=F32))
    pp = jnp.dot(p_ref[...].astype(BF16), wp_ref[...], preferred_element_type=F32)
    out = h2 + pg * pp
    if final:
        out = _rms(out, fw_ref[...])
    o_ref[...] = out


def moe_combine_ple(h, yb, pos_tiles, gate_cols, p, nw, wg, wp, fw, *, final):
    t, d = h.shape
    tq = ROUTE_TILE
    pd = p.shape[1]
    return pl.pallas_call(
        functools.partial(_combine_ple_kernel, final=final),
        grid=(t // tq,),
        in_specs=[pl.BlockSpec(memory_space=pl.ANY),
                  pl.BlockSpec(memory_space=pl.ANY),
                  pl.BlockSpec((tq, d), lambda i: (i, 0)),
                  pl.BlockSpec((tq, 2), lambda i: (i, 0)),
                  pl.BlockSpec((tq, pd), lambda i: (i, 0)),
                  pl.BlockSpec((1, d), lambda i: (0, 0)),
                  pl.BlockSpec((d, d), lambda i: (0, 0)),
                  pl.BlockSpec((pd, d), lambda i: (0, 0)),
                  pl.BlockSpec((1, d), lambda i: (0, 0))],
        out_specs=pl.BlockSpec((tq, d), lambda i: (i, 0)),
        out_shape=jax.ShapeDtypeStruct((t, d), F32),
        scratch_shapes=[pltpu.SMEM((2 * tq // LANES, LANES), I32),
                        pltpu.VMEM((tq, d), F32), pltpu.VMEM((tq, d), F32),
                        pltpu.SemaphoreType.DMA, pltpu.SemaphoreType.DMA],
        compiler_params=_cparams("arbitrary"),
        name="moe_combine_ple",
    )(pos_tiles, yb, h, gate_cols, p, nw.reshape(1, d), wg, wp, fw.reshape(1, d))


def moe_ple_layer(h, p, norm_moe, w_group, b_group, w_expert, b_expert, w1, w3, w2,
                  norm_ple, wg, wp, fw, *, final):
    t, d = h.shape
    eid_tiles, rank_tiles, gate, counts = moe_router(h, norm_moe, w_group, b_group, w_expert,
                                                     b_expert)
    padded = (counts + MOE_BLOCK - 1) // MOE_BLOCK * MOE_BLOCK
    pend = jnp.cumsum(padded)
    pstart = pend - padded
    na = 2 * t
    n_pad = (na + MOE_BLOCK - 1) // MOE_BLOCK * MOE_BLOCK + N_EXPERTS * MOE_BLOCK
    nb = n_pad // MOE_BLOCK
    blk_start = jnp.arange(nb, dtype=I32) * MOE_BLOCK
    blk_expert = jnp.minimum(jnp.sum(blk_start[:, None] >= pend[None, :], axis=-1),
                             N_EXPERTS - 1).astype(I32)
    n_used = (pend[-1:] // MOE_BLOCK).astype(I32)
    pos_tiles = moe_slots(pstart.astype(I32), eid_tiles, rank_tiles)
    xb = moe_dispatch(h, pos_tiles, n_pad)
    yb = moe_experts(xb, norm_moe, w1.astype(BF16), w3.astype(BF16), w2.astype(BF16),
                     blk_expert, n_used)
    return moe_combine_ple(h, yb, pos_tiles, gate.T, p, norm_ple, wg.astype(BF16),
                           wp.astype(BF16), fw, final=final)


def _pad_cols(w, n):
    return jnp.pad(w, ((0, 0), (0, n - w.shape[1])))


def kernel(x, p, norm_mix, norm_moe, norm_ple, final_norm, m_in_w, m_conv_w, m_conv_b, m_dt_bias, m_A_log, m_D, m_norm_w, m_out_w, g_in_w, g_conv_w, g_dt_bias, g_A_log, g_norm_w, g_out_w, moe_w_group, moe_b_group, moe_w_expert, moe_b_expert, moe_w1, moe_w3, moe_w2, ple_w_proj, ple_w_gate):
    batch, seq, d = x.shape
    t = batch * seq
    depth = p.shape[0]
    pd = p.shape[-1]
    h = x.reshape(t, d)
    p2 = p.reshape(depth, t, pd)
    for i in range(depth):
        j = i // 2
        if i % 2 == 0:
            inner = m_out_w.shape[1]
            heads = m_dt_bias.shape[1]
            conv_dim = m_conv_w.shape[2]
            w_in = _pad_cols(m_in_w[j], inner + conv_dim + LANES).astype(BF16)
            proj = norm_matmul(h, norm_mix[i], w_in)
            y = ssd_mixer_core(proj, m_conv_w[j], m_conv_b[j], m_dt_bias[j], m_A_log[j], m_D[j],
                               m_norm_w[j], batch=batch, seq=seq, inner=inner, heads=heads)
            h = matmul_residual(y, m_out_w[j].astype(BF16), h)
        else:
            v_dim = g_out_w.shape[1]
            v_heads = g_dt_bias.shape[1]
            conv_dim = g_conv_w.shape[2]
            qk_dim = (conv_dim - v_dim) // 2
            w_in = _pad_cols(g_in_w[j], conv_dim + v_dim + LANES).astype(BF16)
            proj = norm_matmul(h, norm_mix[i], w_in)
            y = gdn_mixer_core(proj, g_conv_w[j], g_dt_bias[j], g_A_log[j], g_norm_w[j],
                               batch=batch, seq=seq, qk_dim=qk_dim, v_dim=v_dim, v_heads=v_heads)
            h = matmul_residual(y, g_out_w[j].astype(BF16), h)
        h = moe_ple_layer(h, p2[i], norm_moe[i], moe_w_group[i], moe_b_group[i], moe_w_expert[i],
                          moe_b_expert[i], moe_w1[i], moe_w3[i], moe_w2[i], norm_ple[i],
                          ple_w_gate[i], ple_w_proj[i], final_norm, final=(i == depth - 1))
    return h.reshape(batch, seq, d)
```

```python
import functools
import math

import jax
import jax.numpy as jnp
from jax import lax
from jax.experimental import pallas as pl
from jax.experimental.pallas import tpu as pltpu

F32 = jnp.float32
BF16 = jnp.bfloat16
I32 = jnp.int32
EPS = 1e-6
HIGHEST = lax.Precision.HIGHEST

LANES = 128
SUBLANES = 8
VMEM_LIMIT = 56 * 1024 * 1024

CONV_K = 4
SSD_CHUNK = 128
SSD_HEADDIM = 64
SSD_STATE = 128
SSD_GROUPS = 4
GDN_CHUNK = 64
GDN_HEAD = 128
GDN_STACK = 256
GDN_SUBBLOCK = 16
GDN_BATCH_PER_STEP = 1
N_EGROUPS = 8
EXPERTS_PER_GROUP = 8
N_EXPERTS = N_EGROUPS * EXPERTS_PER_GROUP
MOE_BLOCK = 256
ROUTE_TILE = 512
DMA_UNROLL = 8


def _cparams(*sem):
    return pltpu.CompilerParams(dimension_semantics=sem, vmem_limit_bytes=VMEM_LIMIT)


def _silu(x):
    return x * jax.nn.sigmoid(x)


def _softplus(x):
    return jnp.maximum(x, 0.0) + jnp.log1p(jnp.exp(-jnp.abs(x)))


def _rms(x, w):
    return x * lax.rsqrt(jnp.mean(x * x, axis=-1, keepdims=True) + EPS) * w


def _dot(a, b):
    return jnp.dot(a.astype(BF16), b.astype(BF16), preferred_element_type=F32)


def _split(x):
    hi = x.astype(BF16)
    return hi, (x - hi.astype(F32)).astype(BF16)


def _dot_f32(a, b):
    return jnp.dot(a, b, preferred_element_type=F32, precision=HIGHEST)


def _dot_tn(a, b):
    return lax.dot_general(a.astype(BF16), b.astype(BF16), (((0,), (0,)), ((), ())),
                           preferred_element_type=F32)


def _dot_nt(a, b):
    return lax.dot_general(a.astype(BF16), b.astype(BF16), (((1,), (1,)), ((), ())),
                           preferred_element_type=F32)


def _tri(n, strict=False):
    r = lax.broadcasted_iota(I32, (n, n), 0)
    c = lax.broadcasted_iota(I32, (n, n), 1)
    return (r > c) if strict else (r >= c)


def _norm_matmul_kernel(x_ref, nw_ref, w_ref, o_ref, *, n_chunk):
    xb = _rms(x_ref[...], nw_ref[...]).astype(BF16)
    n = o_ref.shape[1]
    for c0 in range(0, n, n_chunk):
        c1 = min(c0 + n_chunk, n)
        o_ref[:, c0:c1] = jnp.dot(xb, w_ref[:, c0:c1], preferred_element_type=F32)


def norm_matmul(x, nw, w, *, tm=256, n_chunk=512):
    t, d = x.shape
    n = w.shape[1]
    return pl.pallas_call(
        functools.partial(_norm_matmul_kernel, n_chunk=n_chunk),
        grid=(t // tm,),
        in_specs=[pl.BlockSpec((tm, d), lambda i: (i, 0)),
                  pl.BlockSpec((1, d), lambda i: (0, 0)),
                  pl.BlockSpec((d, n), lambda i: (0, 0))],
        out_specs=pl.BlockSpec((tm, n), lambda i: (i, 0)),
        out_shape=jax.ShapeDtypeStruct((t, n), F32),
        compiler_params=_cparams("arbitrary"),
        name="norm_matmul",
    )(x, nw.reshape(1, d), w)


def _matmul_residual_kernel(y_ref, w_ref, r_ref, o_ref):
    o_ref[...] = r_ref[...] + jnp.dot(y_ref[...].astype(BF16), w_ref[...],
                                      preferred_element_type=F32)


def matmul_residual(y, w, res, *, tm=512):
    t, k = y.shape
    d = w.shape[1]
    return pl.pallas_call(
        _matmul_residual_kernel,
        grid=(t // tm,),
        in_specs=[pl.BlockSpec((tm, k), lambda i: (i, 0)),
                  pl.BlockSpec((k, d), lambda i: (0, 0)),
                  pl.BlockSpec((tm, d), lambda i: (i, 0))],
        out_specs=pl.BlockSpec((tm, d), lambda i: (i, 0)),
        out_shape=jax.ShapeDtypeStruct((t, d), F32),
        compiler_params=_cparams("arbitrary"),
        name="matmul_residual",
    )(y, w, res)


def _conv_silu(x_ref, stage_ref, w_ref, bias, first):
    q = x_ref.shape[0]

    @pl.when(first)
    def _():
        stage_ref[0:SUBLANES, :] = jnp.zeros((SUBLANES, stage_ref.shape[1]), F32)

    stage_ref[SUBLANES:SUBLANES + q, :] = x_ref[...]
    acc = stage_ref[SUBLANES:SUBLANES + q, :] * w_ref[CONV_K - 1:CONV_K, :]
    for j in range(CONV_K - 1):
        off = SUBLANES - (CONV_K - 1) + j
        acc = acc + stage_ref[off:off + q, :] * w_ref[j:j + 1, :]
    if bias is not None:
        acc = acc + bias
    stage_ref[0:SUBLANES, :] = stage_ref[q:q + SUBLANES, :]
    return _silu(acc)


def _ssd_kernel(z_ref, x_ref, b_ref, c_ref, dt_ref,
                cwx_ref, cwb_ref, cwc_ref, cbx_ref, cbb_ref, cbc_ref,
                dtb_ref, alog_ref, dfull_ref, nw_ref,
                o_ref,
                sx_ref, sb_ref, sc_ref, y_ref, xw_ref, st_ref):
    q = x_ref.shape[0]
    hpg = x_ref.shape[1] // (SSD_GROUPS * SSD_HEADDIM)
    first = pl.program_id(1) == 0

    @pl.when(first)
    def _():
        st_ref[...] = jnp.zeros(st_ref.shape, F32)

    xs = _conv_silu(x_ref, sx_ref, cwx_ref, cbx_ref[...], first)
    bm = _conv_silu(b_ref, sb_ref, cwb_ref, cbb_ref[...], first)
    cm = _conv_silu(c_ref, sc_ref, cwc_ref, cbc_ref[...], first)

    dt = _softplus(dt_ref[...] + dtb_ref[...])
    da = dt * (-jnp.exp(alog_ref[...]))
    causal = _tri(q)
    a = _dot_f32(causal.astype(F32), da)
    a_t = a.T
    ea_t = jnp.exp(a_t)
    a_last = a[q - 1:q, :]
    to_end_t = (jnp.exp(a_last - a) * dt).T
    ea_last = jnp.exp(a_last)
    dt_t = dt.T
    eye = (lax.broadcasted_iota(I32, (q, q), 0) == lax.broadcasted_iota(I32, (q, q), 1))
    lane_lo = lax.broadcasted_iota(I32, (q, 2 * SSD_HEADDIM), 1) < SSD_HEADDIM

    for g in range(SSD_GROUPS):
        bg = bm[:, g * SSD_STATE:(g + 1) * SSD_STATE]
        cg = cm[:, g * SSD_STATE:(g + 1) * SSD_STATE]
        cb = _dot_nt(cg, bg)
        gw = hpg * SSD_HEADDIM
        cs = _dot(cg, st_ref[g])
        for pr in range(hpg // 2):
            j0 = g * hpg + 2 * pr
            lo, hi = j0 * SSD_HEADDIM, (j0 + 2) * SSD_HEADDIM
            xp = xs[:, lo:hi]
            cp = cs[:, 2 * pr * SSD_HEADDIM:(2 * pr + 2) * SSD_HEADDIM]
            x_sel = [jnp.where(lane_lo, xp, 0.0), jnp.where(lane_lo, 0.0, xp)]
            c_sel = [jnp.where(lane_lo, cp, 0.0), jnp.where(lane_lo, 0.0, cp)]
            lhs, rhs, te = [], [], []
            for k in range(2):
                j = j0 + k
                diff = a[:, j:j + 1] - a_t[j:j + 1, :]
                seg = jnp.exp(jnp.where(causal, diff, -jnp.inf))
                lhs += [cb * seg * dt_t[j:j + 1, :], jnp.where(eye, ea_t[j:j + 1, :], 0.0)]
                rhs += [x_sel[k], c_sel[k]]
                te.append(jnp.where(eye, to_end_t[j:j + 1, :], 0.0))
            y_ref[:, lo:hi] = _dot(jnp.concatenate(lhs, axis=1), jnp.concatenate(rhs, axis=0))
            xw_ref[:, lo:hi] = _dot(jnp.concatenate(te, axis=1), jnp.concatenate(x_sel, axis=0))
        upd = _dot_tn(bg, xw_ref[:, g * gw:(g + 1) * gw])
        for jj in range(hpg):
            j = g * hpg + jj
            sl = slice(jj * SSD_HEADDIM, (jj + 1) * SSD_HEADDIM)
            st_ref[g, :, sl] = st_ref[g, :, sl] * ea_last[:, j:j + 1] + upd[:, sl]

    y = y_ref[...] + dfull_ref[...] * xs
    o_ref[...] = _rms(y * _silu(z_ref[...]), nw_ref[...])


def ssd_mixer_core(proj, conv_w, conv_b, dt_bias, a_log, d_skip, norm_w, *, batch, seq,
                   inner, heads):
    t = proj.shape[0]
    q = SSD_CHUNK
    gn = SSD_GROUPS * SSD_STATE
    nc = seq // q
    cwx, cwb, cwc = conv_w[:, :inner], conv_w[:, inner:inner + gn], conv_w[:, inner + gn:]
    cbx = conv_b[:inner].reshape(1, inner)
    cbb = conv_b[inner:inner + gn].reshape(1, gn)
    cbc = conv_b[inner + gn:].reshape(1, gn)
    pad = LANES - heads
    dtb = jnp.pad(dt_bias, (0, pad)).reshape(1, LANES)
    alog = jnp.pad(a_log, (0, pad)).reshape(1, LANES)
    dfull = jnp.repeat(d_skip, SSD_HEADDIM).reshape(1, inner)
    nw = norm_w.reshape(1, inner)

    def row(i, c):
        return i * nc + c

    const = lambda shape: pl.BlockSpec(shape, lambda i, c: (0, 0))
    x_blk = inner // inner
    b_blk = (2 * inner) // gn
    c_blk = b_blk + 1
    dt_blk = (2 * inner + 2 * gn) // LANES
    return pl.pallas_call(
        _ssd_kernel,
        grid=(batch, nc),
        in_specs=[pl.BlockSpec((q, inner), lambda i, c: (row(i, c), 0)),
                  pl.BlockSpec((q, inner), lambda i, c: (row(i, c), x_blk)),
                  pl.BlockSpec((q, gn), lambda i, c: (row(i, c), b_blk)),
                  pl.BlockSpec((q, gn), lambda i, c: (row(i, c), c_blk)),
                  pl.BlockSpec((q, LANES), lambda i, c: (row(i, c), dt_blk)),
                  const((CONV_K, inner)), const((CONV_K, gn)), const((CONV_K, gn)),
                  const((1, inner)), const((1, gn)), const((1, gn)),
                  const((1, LANES)), const((1, LANES)), const((1, inner)), const((1, inner))],
        out_specs=pl.BlockSpec((q, inner), lambda i, c: (row(i, c), 0)),
        out_shape=jax.ShapeDtypeStruct((t, inner), F32),
        scratch_shapes=[pltpu.VMEM((q + SUBLANES, inner), F32),
                        pltpu.VMEM((q + SUBLANES, gn), F32),
                        pltpu.VMEM((q + SUBLANES, gn), F32),
                        pltpu.VMEM((q, inner), F32),
                        pltpu.VMEM((q, inner), F32),
                        pltpu.VMEM((SSD_GROUPS, SSD_STATE, inner // SSD_GROUPS), F32)],
        compiler_params=_cparams("arbitrary", "arbitrary"),
        name="ssd_scan",
    )(proj, proj, proj, proj, proj, cwx, cwb, cwc, cbx, cbb, cbc, dtb, alog, dfull, nw)


def _l2n(x):
    return x * lax.rsqrt(jnp.sum(x * x, axis=-1, keepdims=True) + EPS)


def _pair_cols(x, i0, i1, half):
    c = x.shape[0]
    lane = lax.broadcasted_iota(I32, (c, 2 * half), 1)
    return jnp.where(lane < half, jnp.broadcast_to(x[:, i0:i0 + 1], (c, 2 * half)),
                     jnp.broadcast_to(x[:, i1:i1 + 1], (c, 2 * half)))


def _gdn_kernel(q_ref, k_ref, v_ref, z_ref, ba_ref,
                cwq_ref, cwk_ref, cwv_ref, dtb_ref, alog_ref, nw_ref,
                o_ref,
                sq_ref, sk_ref, sv_ref, s_ref, *, v_heads):
    nb, c = q_ref.shape[0], q_ref.shape[1]
    qk_heads = q_ref.shape[2] // GDN_HEAD
    rep = v_heads // qk_heads
    gsz = GDN_STACK // c
    n_grp = v_heads // gsz
    first = pl.program_id(1) == 0

    @pl.when(first)
    def _():
        s_ref[...] = jnp.zeros(s_ref.shape, F32)

    t_idx = lax.broadcasted_iota(I32, (c, GDN_STACK), 0)
    s_idx = lax.broadcasted_iota(I32, (c, GDN_STACK), 1) & (c - 1)
    causal = t_idx >= s_idx
    strict = t_idx > s_idx
    eye = t_idx == s_idx
    eye_f = eye.astype(F32)
    sb_shift = GDN_SUBBLOCK.bit_length() - 1
    same_blk = (t_idx >> sb_shift) == (s_idx >> sb_shift)
    c_shift = c.bit_length() - 1
    head_diag = ((lax.broadcasted_iota(I32, (GDN_STACK, GDN_STACK), 0) >> c_shift)
                 == (lax.broadcasted_iota(I32, (GDN_STACK, GDN_STACK), 1) >> c_shift))

    def bdiag(parts):
        zero = jnp.zeros((), BF16)
        return tuple(jnp.where(head_diag, jnp.concatenate([p] * gsz, axis=0), zero) for p in parts)

    def mm(lhs, rhs_bd):
        n = len(lhs) * c
        parts = [_split(x) for x in lhs]
        l_hi = jnp.concatenate([p[0] for p in parts], axis=0)
        l_lo = jnp.concatenate([p[1] for p in parts], axis=0)
        r = jnp.dot(jnp.concatenate([l_hi, l_lo], axis=0), rhs_bd[0], preferred_element_type=F32)
        out = r[:n] + r[n:] + jnp.dot(l_hi, rhs_bd[1], preferred_element_type=F32)
        return [out[i * c:(i + 1) * c] for i in range(len(lhs))]

    def neumann(p0s, n_sq):
        ps = list(p0s)
        ds = [eye_f + p for p in ps]
        for i in range(n_sq):
            for g in range(len(ps)):
                bd = bdiag(_split(ps[g]))
                if i == 0:
                    ps[g], = mm([ps[g]], bd)
                else:
                    ps[g], x = mm([ps[g], ds[g]], bd)
                    ds[g] = ds[g] + x
        if n_sq > 0:
            for g in range(len(ps)):
                x, = mm([ds[g]], bdiag(_split(ps[g])))
                ds[g] = ds[g] + x
        return ds

    def prepare(b):
        qc = _conv_silu(q_ref.at[b], sq_ref.at[b], cwq_ref, None, first)
        kc = _conv_silu(k_ref.at[b], sk_ref.at[b], cwk_ref, None, first)
        vc = _conv_silu(v_ref.at[b], sv_ref.at[b], cwv_ref, None, first)
        ba = ba_ref[b]
        beta = jax.nn.sigmoid(ba)
        gl = -jnp.exp(alog_ref[...]) * _softplus(ba + dtb_ref[...])
        gcs = _dot_f32(_tri(c).astype(F32), gl)
        g_last = gcs[c - 1:c, :]
        st = dict(vc=vc, eg_last=jnp.exp(g_last), qns=[], kns=[], kts=[], a_kks=[], a_qks=[],
                  gcols=[], brows=[], grows=[])
        kks, qks = [], []
        for hq in range(qk_heads):
            sl = slice(hq * GDN_HEAD, (hq + 1) * GDN_HEAD)
            qn = _l2n(qc[:, sl]) * (GDN_HEAD ** -0.5)
            kn = _l2n(kc[:, sl])
            k2 = jnp.concatenate([kn] * rep, axis=0)
            st["qns"].append(qn)
            st["kns"].append(kn)
            st["kts"].append(kn.T.astype(BF16))
            kks.append(_dot_nt(kn, k2))
            qks.append(_dot_nt(qn, k2))
        for grp in range(n_grp):
            h0 = grp * gsz
            hq0 = h0 // rep
            nq = gsz // rep
            kk = jnp.concatenate(kks[hq0:hq0 + nq], axis=1)
            qk = jnp.concatenate(qks[hq0:hq0 + nq], axis=1)
            gcol = jnp.concatenate(
                [_pair_cols(gcs, v_heads + h0 + 2 * i, v_heads + h0 + 2 * i + 1, c)
                 for i in range(gsz // 2)], axis=1)
            bcol = jnp.concatenate([_pair_cols(beta, h0 + 2 * i, h0 + 2 * i + 1, c)
                                    for i in range(gsz // 2)], axis=1)
            grow = jnp.sum(jnp.where(eye, gcol, 0.0), axis=0, keepdims=True)
            decay = jnp.exp(jnp.where(causal, gcol - grow, -jnp.inf))
            st["a_kks"].append(jnp.where(strict, bcol * kk * decay, 0.0))
            st["a_qks"].append(jnp.where(causal, qk * decay, 0.0))
            st["gcols"].append(gcol)
            st["grows"].append(grow)
            st["brows"].append(jnp.sum(jnp.where(eye, bcol, 0.0), axis=0, keepdims=True))
        return st

    def bdiag_blocks(blocks):
        z = jnp.zeros((c, GDN_HEAD), BF16)
        rows = [jnp.concatenate([blocks[i].astype(BF16) if j == i else z for j in range(gsz)],
                                axis=1) for i in range(gsz)]
        return jnp.concatenate(rows, axis=0)

    def heads(b, st, grp, tinv):
        h0 = grp * gsz
        gcol, grow, brow = st["gcols"][grp], st["grows"][grp], st["brows"][grp]
        hd = lambda x, i: x[:, i * GDN_HEAD:(i + 1) * GDN_HEAD]
        u = _dot(tinv * brow, bdiag_blocks([hd(st["vc"], h0 + i) for i in range(gsz)]))
        w = _dot(tinv * (brow * jnp.exp(grow)),
                 bdiag_blocks([st["kns"][(h0 + i) // rep] for i in range(gsz)]))
        v_news, q_states = [], []
        for i in range(gsz):
            hq = (h0 + i) // rep
            ws = _dot(jnp.concatenate([hd(w, i), st["qns"][hq]], axis=0), s_ref[b, h0 + i])
            v_news.append(hd(u, i) - ws[:c])
            q_states.append(ws[c:])
        eg_diag = jnp.where(eye, jnp.exp(gcol), 0.0)
        eg_end_diag = jnp.where(eye, jnp.exp(gcol[c - 1:c, :] - gcol), 0.0)
        bd_v = bdiag_blocks(v_news)
        o = _dot(jnp.concatenate([st["a_qks"][grp], eg_diag], axis=1),
                 jnp.concatenate([bd_v, bdiag_blocks(q_states)], axis=0))
        v_scaled = _dot(eg_end_diag, bd_v)
        for i in range(gsz):
            h = h0 + i
            gi = v_heads + h
            hs = slice(h * GDN_HEAD, (h + 1) * GDN_HEAD)
            s_ref[b, h] = (s_ref[b, h] * st["eg_last"][:, gi:gi + 1]
                           + _dot(st["kts"][h // rep], hd(v_scaled, i)))
            o_ref[b, :, hs] = _rms(hd(o, i), nw_ref[...]) * _silu(z_ref[b, :, hs])

    sts = [prepare(b) for b in range(nb)]

    a_all = [a for st in sts for a in st["a_kks"]]
    dms = neumann([jnp.where(same_blk, -a, 0.0) for a in a_all], sb_shift - 1)
    ms = [-mm([d], bdiag(_split(jnp.where(same_blk, 0.0, a))))[0] for d, a in zip(dms, a_all)]
    ws = neumann(ms, (c // GDN_SUBBLOCK).bit_length() - 2)
    tinvs = [mm([w], bdiag(_split(d)))[0] for w, d in zip(ws, dms)]

    for grp in range(n_grp):
        for b in range(nb):
            heads(b, sts[b], grp, tinvs[b * n_grp + grp])


def gdn_mixer_core(proj, conv_w, dt_bias, a_log, norm_w, *, batch, seq, qk_dim, v_dim, v_heads):
    t, n = proj.shape
    c = GDN_CHUNK
    nb = GDN_BATCH_PER_STEP
    cwq, cwk, cwv = conv_w[:, :qk_dim], conv_w[:, qk_dim:2 * qk_dim], conv_w[:, 2 * qk_dim:]
    dtb = jnp.pad(dt_bias, (v_heads, LANES - 2 * v_heads)).reshape(1, LANES)
    alog = jnp.pad(a_log, (v_heads, LANES - 2 * v_heads)).reshape(1, LANES)
    nw = norm_w.reshape(1, GDN_HEAD)
    proj3 = proj.reshape(batch, seq, n)

    const = lambda shape: pl.BlockSpec(shape, lambda i, cc: (0, 0))
    blk = lambda width, col: pl.BlockSpec((nb, c, width), lambda i, cc: (i, cc, col))
    v_blk = (2 * qk_dim) // v_dim
    z_blk = v_blk + 1
    ba_blk = (2 * qk_dim + 2 * v_dim) // LANES
    out = pl.pallas_call(
        functools.partial(_gdn_kernel, v_heads=v_heads),
        grid=(batch // nb, seq // c),
        in_specs=[blk(qk_dim, 0), blk(qk_dim, 1), blk(v_dim, v_blk), blk(v_dim, z_blk),
                  blk(LANES, ba_blk),
                  const((CONV_K, qk_dim)), const((CONV_K, qk_dim)), const((CONV_K, v_dim)),
                  const((1, LANES)), const((1, LANES)), const((1, GDN_HEAD))],
        out_specs=blk(v_dim, 0),
        out_shape=jax.ShapeDtypeStruct((batch, seq, v_dim), F32),
        scratch_shapes=[pltpu.VMEM((nb, c + SUBLANES, qk_dim), F32),
                        pltpu.VMEM((nb, c + SUBLANES, qk_dim), F32),
                        pltpu.VMEM((nb, c + SUBLANES, v_dim), F32),
                        pltpu.VMEM((nb, v_heads, GDN_HEAD, GDN_HEAD), F32)],
        compiler_params=_cparams("arbitrary", "arbitrary"),
        name="gdn_scan",
    )(proj3, proj3, proj3, proj3, proj3, cwq, cwk, cwv, dtb, alog, nw)
    return out.reshape(t, v_dim)


def _first_index(mask, n):
    idx = lax.broadcasted_iota(I32, mask.shape, 0)
    return jnp.min(jnp.where(mask, idx, n), axis=0, keepdims=True)


def _router_kernel(h_ref, nw_ref, wr_ref, br_ref, upper_ref,
                   eid_ref, rank_ref, gate_ref, cnt_ref, base_ref):
    tq = h_ref.shape[0]
    i = pl.program_id(0)

    @pl.when(i == 0)
    def _():
        base_ref[...] = jnp.zeros(base_ref.shape, F32)

    xn = _rms(h_ref[...], nw_ref[...])
    logits = lax.dot_general(wr_ref[...], xn, (((1,), (1,)), ((), ())),
                             preferred_element_type=F32, precision=HIGHEST) + br_ref[:, 0:1]
    gl = logits[0:N_EGROUPS, :]
    gmax = jnp.max(gl, axis=0, keepdims=True)
    g_sel = _first_index(gl == gmax, N_EGROUPS)
    p_sel = 1.0 / jnp.sum(jnp.exp(gl - gmax), axis=0, keepdims=True)
    e_in = logits[N_EGROUPS:N_EGROUPS + EXPERTS_PER_GROUP, :]
    for g in range(1, N_EGROUPS):
        lo = N_EGROUPS + g * EXPERTS_PER_GROUP
        e_in = jnp.where(g_sel == g, logits[lo:lo + EXPERTS_PER_GROUP, :], e_in)
    m1 = jnp.max(e_in, axis=0, keepdims=True)
    i1 = _first_index(e_in == m1, EXPERTS_PER_GROUP)
    sub = lax.broadcasted_iota(I32, e_in.shape, 0)
    rest = jnp.where(sub == i1, -jnp.inf, e_in)
    m2 = jnp.max(rest, axis=0, keepdims=True)
    i2 = _first_index(rest == m2, EXPERTS_PER_GROUP)
    e2 = jnp.exp(m2 - m1)
    denom = 1.0 + e2
    gate_ref[0:1, :] = (1.0 / denom) * p_sel
    gate_ref[1:2, :] = (e2 / denom) * p_sel
    eid0 = g_sel * EXPERTS_PER_GROUP + i1
    eid1 = g_sel * EXPERTS_PER_GROUP + i2

    erow = lax.broadcasted_iota(I32, (N_EXPERTS, tq), 0)
    oh0 = erow == eid0
    oh1 = erow == eid1
    oh0f = oh0.astype(F32)
    oh1f = oh1.astype(F32)
    cum0 = jnp.dot(oh0f.astype(BF16), upper_ref[...], preferred_element_type=F32)
    cum1 = jnp.dot(oh1f.astype(BF16), upper_ref[...], preferred_element_type=F32)
    base = base_ref[:, 0:1]
    tot0 = jnp.sum(oh0f, axis=1, keepdims=True)
    tot1 = jnp.sum(oh1f, axis=1, keepdims=True)
    r0 = jnp.sum(jnp.where(oh0, base + cum0, 0.0), axis=0, keepdims=True)
    r1 = jnp.sum(jnp.where(oh1, base + tot0 + cum1, 0.0), axis=0, keepdims=True)
    sub = tq // LANES
    for k, (e, r) in enumerate(((eid0, r0.astype(I32)), (eid1, r1.astype(I32)))):
        for rr in range(sub):
            row = k * sub + rr
            eid_ref[0, row:row + 1, :] = e[:, rr * LANES:(rr + 1) * LANES]
            rank_ref[0, row:row + 1, :] = r[:, rr * LANES:(rr + 1) * LANES]
    new_base = base + tot0 + tot1
    base_ref[...] = jnp.broadcast_to(new_base, base_ref.shape)
    cnt_ref[...] = jnp.broadcast_to(new_base, cnt_ref.shape).astype(I32)


def moe_router(h, nw, w_group, b_group, w_expert, b_expert):
    t, d = h.shape
    tq = ROUTE_TILE
    nr = N_EGROUPS + N_EXPERTS
    wr = jnp.pad(jnp.concatenate([w_group, w_expert], axis=1).T, ((0, LANES - nr), (0, 0)))
    br = jnp.pad(jnp.concatenate([b_group, b_expert]), (0, LANES - nr))
    br = jnp.broadcast_to(br[:, None], (LANES, LANES))
    upper = (jnp.arange(tq)[:, None] < jnp.arange(tq)[None, :]).astype(BF16)
    rows = 2 * tq // LANES
    tile_spec = pl.BlockSpec((1, rows, LANES), lambda i: (i, 0, 0))
    tile_shape = jax.ShapeDtypeStruct((t // tq, rows, LANES), I32)
    eid, rank, gate, cnt = pl.pallas_call(
        _router_kernel,
        grid=(t // tq,),
        in_specs=[pl.BlockSpec((tq, d), lambda i: (i, 0)),
                  pl.BlockSpec((1, d), lambda i: (0, 0)),
                  pl.BlockSpec((LANES, d), lambda i: (0, 0)),
                  pl.BlockSpec((LANES, LANES), lambda i: (0, 0)),
                  pl.BlockSpec((tq, tq), lambda i: (0, 0))],
        out_specs=[tile_spec, tile_spec, pl.BlockSpec((2, tq), lambda i: (0, i)),
                   pl.BlockSpec((N_EXPERTS, LANES), lambda i: (0, 0))],
        out_shape=[tile_shape, tile_shape, jax.ShapeDtypeStruct((2, t), F32),
                   jax.ShapeDtypeStruct((N_EXPERTS, LANES), I32)],
        scratch_shapes=[pltpu.VMEM((N_EXPERTS, LANES), F32)],
        compiler_params=_cparams("arbitrary"),
        name="moe_router",
    )(h, nw.reshape(1, d), wr, br, upper)
    return eid, rank, gate, cnt[:, 0]


def _slot_kernel(pstart_ref, eid_ref, rank_ref, pos_ref):
    eid = eid_ref[...]

    def body(e, acc):
        return jnp.where(eid == e, pstart_ref[e], acc)

    pos_ref[...] = rank_ref[...] + lax.fori_loop(0, N_EXPERTS, body, jnp.zeros(eid.shape, I32))


def moe_slots(pstart, eid_tiles, rank_tiles):
    nt, rows, lanes = eid_tiles.shape
    tiles_per_step = math.gcd(nt, 16)
    blk = pl.BlockSpec((tiles_per_step, rows, lanes), lambda i, ps: (i, 0, 0))
    return pl.pallas_call(
        _slot_kernel,
        grid_spec=pltpu.PrefetchScalarGridSpec(num_scalar_prefetch=1, grid=(nt // tiles_per_step,),
                                               in_specs=[blk, blk], out_specs=blk),
        out_shape=jax.ShapeDtypeStruct(eid_tiles.shape, I32),
        compiler_params=_cparams("arbitrary"),
        name="moe_slots",
    )(pstart, eid_tiles, rank_tiles)


def _row_copy(src, src_row, dst, dst_row, sem):
    return pltpu.make_async_copy(src.at[pl.ds(src_row, 1)], dst.at[pl.ds(dst_row, 1)], sem)


def _dispatch_kernel(pos_hbm, h_ref, xb_in, xb_out, pos_smem, sem_idx, sem_rows):
    del xb_in
    tq = h_ref.shape[0]
    sub = tq // LANES
    i = pl.program_id(0)
    cp = pltpu.make_async_copy(pos_hbm.at[i], pos_smem, sem_idx)
    cp.start()
    cp.wait()
    for rr in range(sub):
        def body(cu, carry, rr=rr):
            for u in range(DMA_UNROLL):
                cc = cu * DMA_UNROLL + u
                r = rr * LANES + cc
                _row_copy(h_ref, r, xb_out, pos_smem[rr, cc], sem_rows).start()
                _row_copy(h_ref, r, xb_out, pos_smem[sub + rr, cc], sem_rows).start()
            return carry
        lax.fori_loop(0, LANES // DMA_UNROLL, body, 0)
    pltpu.make_async_copy(h_ref, xb_out.at[pl.ds(0, tq)], sem_rows).wait()
    pltpu.make_async_copy(h_ref, xb_out.at[pl.ds(0, tq)], sem_rows).wait()


def moe_dispatch(h, pos_tiles, n_pad):
    t, d = h.shape
    tq = ROUTE_TILE
    xb0 = jnp.zeros((n_pad, d), F32)
    return pl.pallas_call(
        _dispatch_kernel,
        grid=(t // tq,),
        in_specs=[pl.BlockSpec(memory_space=pl.ANY),
                  pl.BlockSpec((tq, d), lambda i: (i, 0)),
                  pl.BlockSpec(memory_space=pl.ANY)],
        out_specs=pl.BlockSpec(memory_space=pl.ANY),
        out_shape=jax.ShapeDtypeStruct((n_pad, d), F32),
        scratch_shapes=[pltpu.SMEM((2 * tq // LANES, LANES), I32),
                        pltpu.SemaphoreType.DMA, pltpu.SemaphoreType.DMA],
        input_output_aliases={2: 0},
        compiler_params=_cparams("arbitrary"),
        name="moe_dispatch",
    )(pos_tiles, h, xb0)


def _expert_kernel(be_ref, nu_ref, x_ref, nw_ref, w1_ref, w3_ref, w2_ref, o_ref,
                   w1b_ref, w3b_ref, w2b_ref):
    b = pl.program_id(0)
    used = b < nu_ref[0]
    new_expert = (b == 0) | (be_ref[b] != be_ref[jnp.maximum(b - 1, 0)])

    @pl.when(used & new_expert)
    def _():
        w1b_ref[...] = w1_ref[0].astype(BF16)
        w3b_ref[...] = w3_ref[0].astype(BF16)
        w2b_ref[...] = w2_ref[0].astype(BF16)

    @pl.when(used)
    def _():
        xb = _rms(x_ref[...], nw_ref[...]).astype(BF16)
        h1 = jnp.dot(xb, w1b_ref[...], preferred_element_type=F32)
        h3 = jnp.dot(xb, w3b_ref[...], preferred_element_type=F32)
        hid = (_silu(h1) * h3).astype(BF16)
        o_ref[...] = jnp.dot(hid, w2b_ref[...], preferred_element_type=F32)

    @pl.when(b >= nu_ref[0])
    def _():
        o_ref[...] = jnp.zeros(o_ref.shape, F32)


def moe_experts(xb, nw, w1, w3, w2, blk_expert, n_used):
    n_pad, d = xb.shape
    de = w1.shape[2]
    nb = n_pad // MOE_BLOCK
    grid_spec = pltpu.PrefetchScalarGridSpec(
        num_scalar_prefetch=2,
        grid=(nb,),
        in_specs=[pl.BlockSpec((MOE_BLOCK, d), lambda b, be, nu: (b, 0)),
                  pl.BlockSpec((1, d), lambda b, be, nu: (0, 0)),
                  pl.BlockSpec((1, d, de), lambda b, be, nu: (be[b], 0, 0)),
                  pl.BlockSpec((1, d, de), lambda b, be, nu: (be[b], 0, 0)),
                  pl.BlockSpec((1, de, d), lambda b, be, nu: (be[b], 0, 0))],
        out_specs=pl.BlockSpec((MOE_BLOCK, d), lambda b, be, nu: (b, 0)),
        scratch_shapes=[pltpu.VMEM((d, de), BF16), pltpu.VMEM((d, de), BF16),
                        pltpu.VMEM((de, d), BF16)],
    )
    return pl.pallas_call(
        _expert_kernel,
        grid_spec=grid_spec,
        out_shape=jax.ShapeDtypeStruct((n_pad, d), F32),
        compiler_params=_cparams("arbitrary"),
        name="moe_experts",
    )(blk_expert, n_used, xb, nw.reshape(1, d), w1, w3, w2)


def _combine_ple_kernel(pos_hbm, yb_hbm, h_ref, gate_ref, p_ref, nw_ref, wg_ref, wp_ref, fw_ref,
                        o_ref, pos_smem, buf0_ref, buf1_ref, sem_idx, sem_rows, *, final):
    tq = h_ref.shape[0]
    sub = tq // LANES
    i = pl.program_id(0)
    cp = pltpu.make_async_copy(pos_hbm.at[i], pos_smem, sem_idx)
    cp.start()
    cp.wait()
    for rr in range(sub):
        def body(cu, carry, rr=rr):
            for u in range(DMA_UNROLL):
                cc = cu * DMA_UNROLL + u
                r = rr * LANES + cc
                _row_copy(yb_hbm, pos_smem[rr, cc], buf0_ref, r, sem_rows).start()
                _row_copy(yb_hbm, pos_smem[sub + rr, cc], buf1_ref, r, sem_rows).start()
            return carry
        lax.fori_loop(0, LANES // DMA_UNROLL, body, 0)
    pltpu.make_async_copy(yb_hbm.at[pl.ds(0, tq)], buf0_ref, sem_rows).wait()
    pltpu.make_async_copy(yb_hbm.at[pl.ds(0, tq)], buf1_ref, sem_rows).wait()

    gate = gate_ref[...]
    h2 = h_ref[...] + gate[:, 0:1] * buf0_ref[...] + gate[:, 1:2] * buf1_ref[...]
    hn = _rms(h2, nw_ref[...]).astype(BF16)
    pg = jax.nn.sigmoid(jnp.dot(hn, wg_ref[...], preferred_element_type=F32))
    pp = jnp.dot(p_ref[...].astype(BF16), wp_ref[...], preferred_element_type=F32)
    out = h2 + pg * pp
    if final:
        out = _rms(out, fw_ref[...])
    o_ref[...] = out


def moe_combine_ple(h, yb, pos_tiles, gate_cols, p, nw, wg, wp, fw, *, final):
    t, d = h.shape
    tq = ROUTE_TILE
    pd = p.shape[1]
    return pl.pallas_call(
        functools.partial(_combine_ple_kernel, final=final),
        grid=(t // tq,),
        in_specs=[pl.BlockSpec(memory_space=pl.ANY),
                  pl.BlockSpec(memory_space=pl.ANY),
                  pl.BlockSpec((tq, d), lambda i: (i, 0)),
                  pl.BlockSpec((tq, 2), lambda i: (i, 0)),
                  pl.BlockSpec((tq, pd), lambda i: (i, 0)),
                  pl.BlockSpec((1, d), lambda i: (0, 0)),
                  pl.BlockSpec((d, d), lambda i: (0, 0)),
                  pl.BlockSpec((pd, d), lambda i: (0, 0)),
                  pl.BlockSpec((1, d), lambda i: (0, 0))],
        out_specs=pl.BlockSpec((tq, d), lambda i: (i, 0)),
        out_shape=jax.ShapeDtypeStruct((t, d), F32),
        scratch_shapes=[pltpu.SMEM((2 * tq // LANES, LANES), I32),
                        pltpu.VMEM((tq, d), F32), pltpu.VMEM((tq, d), F32),
                        pltpu.SemaphoreType.DMA, pltpu.SemaphoreType.DMA],
        compiler_params=_cparams("arbitrary"),
        name="moe_combine_ple",
    )(pos_tiles, yb, h, gate_cols, p, nw.reshape(1, d), wg, wp, fw.reshape(1, d))


def moe_ple_layer(h, p, norm_moe, w_group, b_group, w_expert, b_expert, w1, w3, w2,
                  norm_ple, wg, wp, fw, *, layer, final):
    t, d = h.shape
    eid_tiles, rank_tiles, gate, counts = moe_router(h, norm_moe, w_group, b_group, w_expert,
                                                     b_expert)
    padded = (counts + MOE_BLOCK - 1) // MOE_BLOCK * MOE_BLOCK
    pend = jnp.cumsum(padded)
    pstart = pend - padded
    na = 2 * t
    n_pad = (na + MOE_BLOCK - 1) // MOE_BLOCK * MOE_BLOCK + N_EXPERTS * MOE_BLOCK
    nb = n_pad // MOE_BLOCK
    blk_start = jnp.arange(nb, dtype=I32) * MOE_BLOCK
    blk_expert = jnp.minimum(jnp.sum(blk_start[:, None] >= pend[None, :], axis=-1),
                             N_EXPERTS - 1).astype(I32)
    n_used = (pend[-1:] // MOE_BLOCK).astype(I32)
    pos_tiles = moe_slots(pstart.astype(I32), eid_tiles, rank_tiles)
    xb = moe_dispatch(h, pos_tiles, n_pad)
    yb = moe_experts(xb, norm_moe, w1, w3, w2, blk_expert + layer * N_EXPERTS, n_used)
    return moe_combine_ple(h, yb, pos_tiles, gate.T, p, norm_ple, wg.astype(BF16),
                           wp.astype(BF16), fw, final=final)


def _pad_cols(w, n):
    return jnp.pad(w, ((0, 0), (0, n - w.shape[1])))


def kernel(x, p, norm_mix, norm_moe, norm_ple, final_norm, m_in_w, m_conv_w, m_conv_b, m_dt_bias, m_A_log, m_D, m_norm_w, m_out_w, g_in_w, g_conv_w, g_dt_bias, g_A_log, g_norm_w, g_out_w, moe_w_group, moe_b_group, moe_w_expert, moe_b_expert, moe_w1, moe_w3, moe_w2, ple_w_proj, ple_w_gate):
    batch, seq, d = x.shape
    t = batch * seq
    depth = p.shape[0]
    pd = p.shape[-1]
    h = x.reshape(t, d)
    p2 = p.reshape(depth, t, pd)
    w1_all = moe_w1.reshape((-1,) + moe_w1.shape[2:])
    w3_all = moe_w3.reshape((-1,) + moe_w3.shape[2:])
    w2_all = moe_w2.reshape((-1,) + moe_w2.shape[2:])
    for i in range(depth):
        j = i // 2
        if i % 2 == 0:
            inner = m_out_w.shape[1]
            heads = m_dt_bias.shape[1]
            conv_dim = m_conv_w.shape[2]
            w_in = _pad_cols(m_in_w[j], inner + conv_dim + LANES).astype(BF16)
            proj = norm_matmul(h, norm_mix[i], w_in)
            y = ssd_mixer_core(proj, m_conv_w[j], m_conv_b[j], m_dt_bias[j], m_A_log[j], m_D[j],
                               m_norm_w[j], batch=batch, seq=seq, inner=inner, heads=heads)
            h = matmul_residual(y, m_out_w[j].astype(BF16), h)
        else:
            v_dim = g_out_w.shape[1]
            v_heads = g_dt_bias.shape[1]
            conv_dim = g_conv_w.shape[2]
            qk_dim = (conv_dim - v_dim) // 2
            w_in = _pad_cols(g_in_w[j], conv_dim + v_dim + LANES).astype(BF16)
            proj = norm_matmul(h, norm_mix[i], w_in)
            y = gdn_mixer_core(proj, g_conv_w[j], g_dt_bias[j], g_A_log[j], g_norm_w[j],
                               batch=batch, seq=seq, qk_dim=qk_dim, v_dim=v_dim, v_heads=v_heads)
            h = matmul_residual(y, g_out_w[j].astype(BF16), h)
        h = moe_ple_layer(h, p2[i], norm_moe[i], moe_w_group[i], moe_b_group[i], moe_w_expert[i],
                          moe_b_expert[i], w1_all, w3_all, w2_all, norm_ple[i],
                          ple_w_gate[i], ple_w_proj[i], final_norm, layer=i,
                          final=(i == depth - 1))
    return h.reshape(batch, seq, d)
```

```python
import functools
import math

import jax
import jax.numpy as jnp
from jax import lax
from jax.experimental import pallas as pl
from jax.experimental.pallas import tpu as pltpu

F32 = jnp.float32
BF16 = jnp.bfloat16
I32 = jnp.int32
EPS = 1e-6
HIGHEST = lax.Precision.HIGHEST

LANES = 128
SUBLANES = 8
VMEM_LIMIT = 56 * 1024 * 1024

CONV_K = 4
SSD_CHUNK = 128
SSD_HEADDIM = 64
SSD_STATE = 128
SSD_GROUPS = 4
GDN_CHUNK = 64
GDN_HEAD = 128
GDN_STACK = 256
GDN_SUBBLOCK = 16
GDN_BATCH_PER_STEP = 1
N_EGROUPS = 8
EXPERTS_PER_GROUP = 8
N_EXPERTS = N_EGROUPS * EXPERTS_PER_GROUP
MOE_BLOCK = 256
ROUTE_TILE = 512
DMA_UNROLL = 8


def _cparams(*sem):
    return pltpu.CompilerParams(dimension_semantics=sem, vmem_limit_bytes=VMEM_LIMIT)


def _silu(x):
    return x * jax.nn.sigmoid(x)


def _softplus(x):
    return jnp.maximum(x, 0.0) + jnp.log1p(jnp.exp(-jnp.abs(x)))


def _rms(x, w):
    return x * lax.rsqrt(jnp.mean(x * x, axis=-1, keepdims=True) + EPS) * w


def _dot(a, b):
    return jnp.dot(a.astype(BF16), b.astype(BF16), preferred_element_type=F32)


def _split(x):
    hi = x.astype(BF16)
    return hi, (x - hi.astype(F32)).astype(BF16)


def _dot_f32(a, b):
    return jnp.dot(a, b, preferred_element_type=F32, precision=HIGHEST)


def _dot_tn(a, b):
    return lax.dot_general(a.astype(BF16), b.astype(BF16), (((0,), (0,)), ((), ())),
                           preferred_element_type=F32)


def _dot_nt(a, b):
    return lax.dot_general(a.astype(BF16), b.astype(BF16), (((1,), (1,)), ((), ())),
                           preferred_element_type=F32)


def _tri(n, strict=False):
    r = lax.broadcasted_iota(I32, (n, n), 0)
    c = lax.broadcasted_iota(I32, (n, n), 1)
    return (r > c) if strict else (r >= c)


def _norm_matmul_kernel(x_ref, nw_ref, w_ref, o_ref, *, n_chunk):
    xb = _rms(x_ref[...], nw_ref[...]).astype(BF16)
    n = o_ref.shape[1]
    for c0 in range(0, n, n_chunk):
        c1 = min(c0 + n_chunk, n)
        o_ref[:, c0:c1] = jnp.dot(xb, w_ref[:, c0:c1], preferred_element_type=F32)


def norm_matmul(x, nw, w, *, tm=256, n_chunk=512):
    t, d = x.shape
    n = w.shape[1]
    return pl.pallas_call(
        functools.partial(_norm_matmul_kernel, n_chunk=n_chunk),
        grid=(t // tm,),
        in_specs=[pl.BlockSpec((tm, d), lambda i: (i, 0)),
                  pl.BlockSpec((1, d), lambda i: (0, 0)),
                  pl.BlockSpec((d, n), lambda i: (0, 0))],
        out_specs=pl.BlockSpec((tm, n), lambda i: (i, 0)),
        out_shape=jax.ShapeDtypeStruct((t, n), F32),
        compiler_params=_cparams("arbitrary"),
        name="norm_matmul",
    )(x, nw.reshape(1, d), w)


def _matmul_residual_kernel(y_ref, w_ref, r_ref, o_ref):
    o_ref[...] = r_ref[...] + jnp.dot(y_ref[...].astype(BF16), w_ref[...],
                                      preferred_element_type=F32)


def matmul_residual(y, w, res, *, tm=512):
    t, k = y.shape
    d = w.shape[1]
    return pl.pallas_call(
        _matmul_residual_kernel,
        grid=(t // tm,),
        in_specs=[pl.BlockSpec((tm, k), lambda i: (i, 0)),
                  pl.BlockSpec((k, d), lambda i: (0, 0)),
                  pl.BlockSpec((tm, d), lambda i: (i, 0))],
        out_specs=pl.BlockSpec((tm, d), lambda i: (i, 0)),
        out_shape=jax.ShapeDtypeStruct((t, d), F32),
        compiler_params=_cparams("arbitrary"),
        name="matmul_residual",
    )(y, w, res)


def _conv_silu(x_ref, stage_ref, w_ref, bias, first):
    q = x_ref.shape[0]

    @pl.when(first)
    def _():
        stage_ref[0:SUBLANES, :] = jnp.zeros((SUBLANES, stage_ref.shape[1]), F32)

    stage_ref[SUBLANES:SUBLANES + q, :] = x_ref[...]
    acc = stage_ref[SUBLANES:SUBLANES + q, :] * w_ref[CONV_K - 1:CONV_K, :]
    for j in range(CONV_K - 1):
        off = SUBLANES - (CONV_K - 1) + j
        acc = acc + stage_ref[off:off + q, :] * w_ref[j:j + 1, :]
    if bias is not None:
        acc = acc + bias
    stage_ref[0:SUBLANES, :] = stage_ref[q:q + SUBLANES, :]
    return _silu(acc)


def _ssd_kernel(z_ref, x_ref, b_ref, c_ref, dt_ref,
                cwx_ref, cwb_ref, cwc_ref, cbx_ref, cbb_ref, cbc_ref,
                dtb_ref, alog_ref, dfull_ref, nw_ref,
                o_ref,
                sx_ref, sb_ref, sc_ref, y_ref, xw_ref, st_ref):
    q = x_ref.shape[0]
    hpg = x_ref.shape[1] // (SSD_GROUPS * SSD_HEADDIM)
    first = pl.program_id(1) == 0

    @pl.when(first)
    def _():
        st_ref[...] = jnp.zeros(st_ref.shape, F32)

    xs = _conv_silu(x_ref, sx_ref, cwx_ref, cbx_ref[...], first)
    bm = _conv_silu(b_ref, sb_ref, cwb_ref, cbb_ref[...], first)
    cm = _conv_silu(c_ref, sc_ref, cwc_ref, cbc_ref[...], first)

    dt = _softplus(dt_ref[...] + dtb_ref[...])
    da = dt * (-jnp.exp(alog_ref[...]))
    causal = _tri(q)
    a = _dot_f32(causal.astype(F32), da)
    a_t = a.T
    ea_t = jnp.exp(a_t)
    a_last = a[q - 1:q, :]
    to_end_t = (jnp.exp(a_last - a) * dt).T
    ea_last = jnp.exp(a_last)
    dt_t = dt.T
    eye = (lax.broadcasted_iota(I32, (q, q), 0) == lax.broadcasted_iota(I32, (q, q), 1))
    lane_lo = lax.broadcasted_iota(I32, (q, 2 * SSD_HEADDIM), 1) < SSD_HEADDIM

    for g in range(SSD_GROUPS):
        bg = bm[:, g * SSD_STATE:(g + 1) * SSD_STATE]
        cg = cm[:, g * SSD_STATE:(g + 1) * SSD_STATE]
        cb = _dot_nt(cg, bg)
        gw = hpg * SSD_HEADDIM
        cs = _dot(cg, st_ref[g])
        for pr in range(hpg // 2):
            j0 = g * hpg + 2 * pr
            lo, hi = j0 * SSD_HEADDIM, (j0 + 2) * SSD_HEADDIM
            xp = xs[:, lo:hi]
            cp = cs[:, 2 * pr * SSD_HEADDIM:(2 * pr + 2) * SSD_HEADDIM]
            x_sel = [jnp.where(lane_lo, xp, 0.0), jnp.where(lane_lo, 0.0, xp)]
            c_sel = [jnp.where(lane_lo, cp, 0.0), jnp.where(lane_lo, 0.0, cp)]
            lhs, rhs, te = [], [], []
            for k in range(2):
                j = j0 + k
                diff = a[:, j:j + 1] - a_t[j:j + 1, :]
                seg = jnp.exp(jnp.where(causal, diff, -jnp.inf))
                lhs += [cb * seg * dt_t[j:j + 1, :], jnp.where(eye, ea_t[j:j + 1, :], 0.0)]
                rhs += [x_sel[k], c_sel[k]]
                te.append(jnp.where(eye, to_end_t[j:j + 1, :], 0.0))
            y_ref[:, lo:hi] = _dot(jnp.concatenate(lhs, axis=1), jnp.concatenate(rhs, axis=0))
            xw_ref[:, lo:hi] = _dot(jnp.concatenate(te, axis=1), jnp.concatenate(x_sel, axis=0))
        upd = _dot_tn(bg, xw_ref[:, g * gw:(g + 1) * gw])
        for jj in range(hpg):
            j = g * hpg + jj
            sl = slice(jj * SSD_HEADDIM, (jj + 1) * SSD_HEADDIM)
            st_ref[g, :, sl] = st_ref[g, :, sl] * ea_last[:, j:j + 1] + upd[:, sl]

    y = y_ref[...] + dfull_ref[...] * xs
    o_ref[...] = _rms(y * _silu(z_ref[...]), nw_ref[...])


def ssd_mixer_core(proj, conv_w, conv_b, dt_bias, a_log, d_skip, norm_w, *, batch, seq,
                   inner, heads):
    t = proj.shape[0]
    q = SSD_CHUNK
    gn = SSD_GROUPS * SSD_STATE
    nc = seq // q
    cwx, cwb, cwc = conv_w[:, :inner], conv_w[:, inner:inner + gn], conv_w[:, inner + gn:]
    cbx = conv_b[:inner].reshape(1, inner)
    cbb = conv_b[inner:inner + gn].reshape(1, gn)
    cbc = conv_b[inner + gn:].reshape(1, gn)
    pad = LANES - heads
    dtb = jnp.pad(dt_bias, (0, pad)).reshape(1, LANES)
    alog = jnp.pad(a_log, (0, pad)).reshape(1, LANES)
    dfull = jnp.repeat(d_skip, SSD_HEADDIM).reshape(1, inner)
    nw = norm_w.reshape(1, inner)

    def row(i, c):
        return i * nc + c

    const = lambda shape: pl.BlockSpec(shape, lambda i, c: (0, 0))
    x_blk = inner // inner
    b_blk = (2 * inner) // gn
    c_blk = b_blk + 1
    dt_blk = (2 * inner + 2 * gn) // LANES
    return pl.pallas_call(
        _ssd_kernel,
        grid=(batch, nc),
        in_specs=[pl.BlockSpec((q, inner), lambda i, c: (row(i, c), 0)),
                  pl.BlockSpec((q, inner), lambda i, c: (row(i, c), x_blk)),
                  pl.BlockSpec((q, gn), lambda i, c: (row(i, c), b_blk)),
                  pl.BlockSpec((q, gn), lambda i, c: (row(i, c), c_blk)),
                  pl.BlockSpec((q, LANES), lambda i, c: (row(i, c), dt_blk)),
                  const((CONV_K, inner)), const((CONV_K, gn)), const((CONV_K, gn)),
                  const((1, inner)), const((1, gn)), const((1, gn)),
                  const((1, LANES)), const((1, LANES)), const((1, inner)), const((1, inner))],
        out_specs=pl.BlockSpec((q, inner), lambda i, c: (row(i, c), 0)),
        out_shape=jax.ShapeDtypeStruct((t, inner), F32),
        scratch_shapes=[pltpu.VMEM((q + SUBLANES, inner), F32),
                        pltpu.VMEM((q + SUBLANES, gn), F32),
                        pltpu.VMEM((q + SUBLANES, gn), F32),
                        pltpu.VMEM((q, inner), F32),
                        pltpu.VMEM((q, inner), F32),
                        pltpu.VMEM((SSD_GROUPS, SSD_STATE, inner // SSD_GROUPS), F32)],
        compiler_params=_cparams("arbitrary", "arbitrary"),
        name="ssd_scan",
    )(proj, proj, proj, proj, proj, cwx, cwb, cwc, cbx, cbb, cbc, dtb, alog, dfull, nw)


def _l2n(x):
    return x * lax.rsqrt(jnp.sum(x * x, axis=-1, keepdims=True) + EPS)


def _pair_cols(x, i0, i1, half):
    c = x.shape[0]
    lane = lax.broadcasted_iota(I32, (c, 2 * half), 1)
    return jnp.where(lane < half, jnp.broadcast_to(x[:, i0:i0 + 1], (c, 2 * half)),
                     jnp.broadcast_to(x[:, i1:i1 + 1], (c, 2 * half)))


def _gdn_kernel(q_ref, k_ref, v_ref, z_ref, ba_ref,
                cwq_ref, cwk_ref, cwv_ref, dtb_ref, alog_ref, nw_ref,
                o_ref,
                sq_ref, sk_ref, sv_ref, s_ref, *, v_heads):
    nb, c = q_ref.shape[0], q_ref.shape[1]
    qk_heads = q_ref.shape[2] // GDN_HEAD
    rep = v_heads // qk_heads
    gsz = GDN_STACK // c
    n_grp = v_heads // gsz
    first = pl.program_id(1) == 0

    @pl.when(first)
    def _():
        s_ref[...] = jnp.zeros(s_ref.shape, F32)

    t_idx = lax.broadcasted_iota(I32, (c, GDN_STACK), 0)
    s_idx = lax.broadcasted_iota(I32, (c, GDN_STACK), 1) & (c - 1)
    causal = t_idx >= s_idx
    strict = t_idx > s_idx
    eye = t_idx == s_idx
    eye_f = eye.astype(F32)
    sb_shift = GDN_SUBBLOCK.bit_length() - 1
    same_blk = (t_idx >> sb_shift) == (s_idx >> sb_shift)
    n_col = GDN_STACK // LANES
    lane_lo = lax.broadcasted_iota(I32, (c, LANES), 1) < c

    def bdiag(parts):
        zero = jnp.zeros((), BF16)
        cols = lambda p: [p[:, k * LANES:(k + 1) * LANES] for k in range(n_col)]
        return tuple([jnp.concatenate([jnp.where(lane_lo, y, zero), jnp.where(lane_lo, zero, y)],
                                      axis=0) for y in cols(p)] for p in parts)

    def mm(lhs, rhs_bd):
        n = len(lhs) * c
        parts = [_split(x) for x in lhs]
        l_hi = jnp.concatenate([p[0] for p in parts], axis=0)
        l_both = jnp.concatenate([l_hi] + [p[1] for p in parts], axis=0)
        outs = []
        for k in range(n_col):
            sl = slice(k * LANES, (k + 1) * LANES)
            r = jnp.dot(l_both[:, sl], rhs_bd[0][k], preferred_element_type=F32)
            outs.append(r[:n] + r[n:] + jnp.dot(l_hi[:, sl], rhs_bd[1][k],
                                                preferred_element_type=F32))
        out = jnp.concatenate(outs, axis=1)
        return [out[i * c:(i + 1) * c] for i in range(len(lhs))]

    def neumann(p0s, n_sq):
        ps = list(p0s)
        ds = [eye_f + p for p in ps]
        for i in range(n_sq):
            for g in range(len(ps)):
                bd = bdiag(_split(ps[g]))
                if i == 0:
                    ps[g], = mm([ps[g]], bd)
                else:
                    ps[g], x = mm([ps[g], ds[g]], bd)
                    ds[g] = ds[g] + x
        if n_sq > 0:
            for g in range(len(ps)):
                x, = mm([ds[g]], bdiag(_split(ps[g])))
                ds[g] = ds[g] + x
        return ds

    def prepare(b):
        qc = _conv_silu(q_ref.at[b], sq_ref.at[b], cwq_ref, None, first)
        kc = _conv_silu(k_ref.at[b], sk_ref.at[b], cwk_ref, None, first)
        vc = _conv_silu(v_ref.at[b], sv_ref.at[b], cwv_ref, None, first)
        ba = ba_ref[b]
        beta = jax.nn.sigmoid(ba)
        gl = -jnp.exp(alog_ref[...]) * _softplus(ba + dtb_ref[...])
        gcs = _dot_f32(_tri(c).astype(F32), gl)
        g_last = gcs[c - 1:c, :]
        st = dict(vc=vc, eg_last=jnp.exp(g_last), qns=[], kns=[], kts=[], a_kks=[], a_qks=[],
                  gcols=[], brows=[], grows=[])
        kks, qks = [], []
        for hq in range(qk_heads):
            sl = slice(hq * GDN_HEAD, (hq + 1) * GDN_HEAD)
            qn = _l2n(qc[:, sl]) * (GDN_HEAD ** -0.5)
            kn = _l2n(kc[:, sl])
            k2 = jnp.concatenate([kn] * rep, axis=0)
            st["qns"].append(qn)
            st["kns"].append(kn)
            st["kts"].append(kn.T.astype(BF16))
            kks.append(_dot_nt(kn, k2))
            qks.append(_dot_nt(qn, k2))
        for grp in range(n_grp):
            h0 = grp * gsz
            hq0 = h0 // rep
            nq = gsz // rep
            kk = jnp.concatenate(kks[hq0:hq0 + nq], axis=1)
            qk = jnp.concatenate(qks[hq0:hq0 + nq], axis=1)
            gcol = jnp.concatenate(
                [_pair_cols(gcs, v_heads + h0 + 2 * i, v_heads + h0 + 2 * i + 1, c)
                 for i in range(gsz // 2)], axis=1)
            bcol = jnp.concatenate([_pair_cols(beta, h0 + 2 * i, h0 + 2 * i + 1, c)
                                    for i in range(gsz // 2)], axis=1)
            grow = jnp.sum(jnp.where(eye, gcol, 0.0), axis=0, keepdims=True)
            decay = jnp.exp(jnp.where(causal, gcol - grow, -jnp.inf))
            st["a_kks"].append(jnp.where(strict, bcol * kk * decay, 0.0))
            st["a_qks"].append(jnp.where(causal, qk * decay, 0.0))
            st["gcols"].append(gcol)
            st["grows"].append(grow)
            st["brows"].append(jnp.sum(jnp.where(eye, bcol, 0.0), axis=0, keepdims=True))
        return st

    def pair_blocks(b0, b1):
        z = jnp.zeros((c, GDN_HEAD), BF16)
        return jnp.concatenate([jnp.concatenate([b0.astype(BF16), z], axis=1),
                                jnp.concatenate([z, b1.astype(BF16)], axis=1)], axis=0)

    def heads(b, st, grp, tinv):
        h0 = grp * gsz
        gcol, grow, brow = st["gcols"][grp], st["grows"][grp], st["brows"][grp]
        hd = lambda x, i: x[:, i * GDN_HEAD:(i + 1) * GDN_HEAD]
        col = lambda x, k: x[:, k * LANES:(k + 1) * LANES]
        tb = tinv * brow
        tbe = tinv * (brow * jnp.exp(grow))
        us, ws_ = [], []
        for k in range(n_col):
            ha = h0 + 2 * k
            us.append(_dot(col(tb, k), pair_blocks(hd(st["vc"], ha), hd(st["vc"], ha + 1))))
            ws_.append(_dot(col(tbe, k), pair_blocks(st["kns"][ha // rep],
                                                      st["kns"][(ha + 1) // rep])))
        u = jnp.concatenate(us, axis=1)
        w = jnp.concatenate(ws_, axis=1)
        yield
        v_news, q_states = [], []
        for i in range(gsz):
            hq = (h0 + i) // rep
            ws = _dot(jnp.concatenate([hd(w, i), st["qns"][hq]], axis=0), s_ref[b, h0 + i])
            v_news.append(hd(u, i) - ws[:c])
            q_states.append(ws[c:])
        yield
        eg_diag =jnp.where(eye, jnp.exp(gcol), 0.0)
        eg_end_diag = jnp.where(eye, jnp.exp(gcol[c - 1:c, :] - gcol), 0.0)
        os_, vss = [], []
        for k in range(n_col):
            pv = pair_blocks(v_news[2 * k], v_news[2 * k + 1])
            pq = pair_blocks(q_states[2 * k], q_states[2 * k + 1])
            os_.append(_dot(jnp.concatenate([col(st["a_qks"][grp], k), col(eg_diag, k)], axis=1),
                            jnp.concatenate([pv, pq], axis=0)))
            vss.append(_dot(col(eg_end_diag, k), pv))
        o = jnp.concatenate(os_, axis=1)
        v_scaled = jnp.concatenate(vss, axis=1)
        yield
        for i in range(gsz):
            h = h0 + i
            gi = v_heads + h
            hs = slice(h * GDN_HEAD, (h + 1) * GDN_HEAD)
            s_ref[b, h] = (s_ref[b, h] * st["eg_last"][:, gi:gi + 1]
                           + _dot(st["kts"][h // rep], hd(v_scaled, i)))
            o_ref[b, :, hs] = _rms(hd(o, i), nw_ref[...]) * _silu(z_ref[b, :, hs])

    sts = [prepare(b) for b in range(nb)]

    a_all = [a for st in sts for a in st["a_kks"]]
    dms = neumann([jnp.where(same_blk, -a, 0.0) for a in a_all], sb_shift - 1)
    ms = [-mm([d], bdiag(_split(jnp.where(same_blk, 0.0, a))))[0] for d, a in zip(dms, a_all)]
    ws = neumann(ms, (c // GDN_SUBBLOCK).bit_length() - 2)
    tinvs = [mm([w], bdiag(_split(d)))[0] for w, d in zip(ws, dms)]

    gens = [heads(b, sts[b], grp, tinvs[b * n_grp + grp]) for grp in range(n_grp)
            for b in range(nb)]
    for _ in range(4):
        for gen in gens:
            next(gen, None)


def gdn_mixer_core(proj, conv_w, dt_bias, a_log, norm_w, *, batch, seq, qk_dim, v_dim, v_heads):
    t, n = proj.shape
    c = GDN_CHUNK
    nb = GDN_BATCH_PER_STEP
    cwq, cwk, cwv = conv_w[:, :qk_dim], conv_w[:, qk_dim:2 * qk_dim], conv_w[:, 2 * qk_dim:]
    dtb = jnp.pad(dt_bias, (v_heads, LANES - 2 * v_heads)).reshape(1, LANES)
    alog = jnp.pad(a_log, (v_heads, LANES - 2 * v_heads)).reshape(1, LANES)
    nw = norm_w.reshape(1, GDN_HEAD)
    proj3 = proj.reshape(batch, seq, n)

    const = lambda shape: pl.BlockSpec(shape, lambda i, cc: (0, 0))
    blk = lambda width, col: pl.BlockSpec((nb, c, width), lambda i, cc: (i, cc, col))
    v_blk = (2 * qk_dim) // v_dim
    z_blk = v_blk + 1
    ba_blk = (2 * qk_dim + 2 * v_dim) // LANES
    out = pl.pallas_call(
        functools.partial(_gdn_kernel, v_heads=v_heads),
        grid=(batch // nb, seq // c),
        in_specs=[blk(qk_dim, 0), blk(qk_dim, 1), blk(v_dim, v_blk), blk(v_dim, z_blk),
                  blk(LANES, ba_blk),
                  const((CONV_K, qk_dim)), const((CONV_K, qk_dim)), const((CONV_K, v_dim)),
                  const((1, LANES)), const((1, LANES)), const((1, GDN_HEAD))],
        out_specs=blk(v_dim, 0),
        out_shape=jax.ShapeDtypeStruct((batch, seq, v_dim), F32),
        scratch_shapes=[pltpu.VMEM((nb, c + SUBLANES, qk_dim), F32),
                        pltpu.VMEM((nb, c + SUBLANES, qk_dim), F32),
                        pltpu.VMEM((nb, c + SUBLANES, v_dim), F32),
                        pltpu.VMEM((nb, v_heads, GDN_HEAD, GDN_HEAD), F32)],
        compiler_params=_cparams("arbitrary", "arbitrary"),
        name="gdn_scan",
    )(proj3, proj3, proj3, proj3, proj3, cwq, cwk, cwv, dtb, alog, nw)
    return out.reshape(t, v_dim)


def _first_index(mask, n):
    idx = lax.broadcasted_iota(I32, mask.shape, 0)
    return jnp.min(jnp.where(mask, idx, n), axis=0, keepdims=True)


def _router_kernel(h_ref, nw_ref, wr_ref, br_ref, upper_ref,
                   eid_ref, rank_ref, gate_ref, cnt_ref, base_ref):
    tq = h_ref.shape[0]
    i = pl.program_id(0)

    @pl.when(i == 0)
    def _():
        base_ref[...] = jnp.zeros(base_ref.shape, F32)

    xn = _rms(h_ref[...], nw_ref[...])
    logits = lax.dot_general(wr_ref[...], xn, (((1,), (1,)), ((), ())),
                             preferred_element_type=F32, precision=HIGHEST) + br_ref[:, 0:1]
    gl = logits[0:N_EGROUPS, :]
    gmax = jnp.max(gl, axis=0, keepdims=True)
    g_sel = _first_index(gl == gmax, N_EGROUPS)
    p_sel = 1.0 / jnp.sum(jnp.exp(gl - gmax), axis=0, keepdims=True)
    e_in = logits[N_EGROUPS:N_EGROUPS + EXPERTS_PER_GROUP, :]
    for g in range(1, N_EGROUPS):
        lo = N_EGROUPS + g * EXPERTS_PER_GROUP
        e_in = jnp.where(g_sel == g, logits[lo:lo + EXPERTS_PER_GROUP, :], e_in)
    m1 = jnp.max(e_in, axis=0, keepdims=True)
    i1 = _first_index(e_in == m1, EXPERTS_PER_GROUP)
    sub = lax.broadcasted_iota(I32, e_in.shape, 0)
    rest = jnp.where(sub == i1, -jnp.inf, e_in)
    m2 = jnp.max(rest, axis=0, keepdims=True)
    i2 = _first_index(rest == m2, EXPERTS_PER_GROUP)
    e2 = jnp.exp(m2 - m1)
    denom = 1.0 + e2
    gate_ref[0:1, :] = (1.0 / denom) * p_sel
    gate_ref[1:2, :] = (e2 / denom) * p_sel
    eid0 = g_sel * EXPERTS_PER_GROUP + i1
    eid1 = g_sel * EXPERTS_PER_GROUP + i2

    erow = lax.broadcasted_iota(I32, (N_EXPERTS, tq), 0)
    oh0 = erow == eid0
    oh1 = erow == eid1
    oh0f = oh0.astype(F32)
    oh1f = oh1.astype(F32)
    cum0 = jnp.dot(oh0f.astype(BF16), upper_ref[...], preferred_element_type=F32)
    cum1 = jnp.dot(oh1f.astype(BF16), upper_ref[...], preferred_element_type=F32)
    base = base_ref[:, 0:1]
    tot0 = jnp.sum(oh0f, axis=1, keepdims=True)
    tot1 = jnp.sum(oh1f, axis=1, keepdims=True)
    r0 = jnp.sum(jnp.where(oh0, base + cum0, 0.0), axis=0, keepdims=True)
    r1 = jnp.sum(jnp.where(oh1, base + tot0 + cum1, 0.0), axis=0, keepdims=True)
    sub = tq // LANES
    for k, (e, r) in enumerate(((eid0, r0.astype(I32)), (eid1, r1.astype(I32)))):
        for rr in range(sub):
            row = k * sub + rr
            eid_ref[0, row:row + 1, :] = e[:, rr * LANES:(rr + 1) * LANES]
            rank_ref[0, row:row + 1, :] = r[:, rr * LANES:(rr + 1) * LANES]
    new_base = base + tot0 + tot1
    base_ref[...] = jnp.broadcast_to(new_base, base_ref.shape)
    cnt_ref[...] = jnp.broadcast_to(new_base, cnt_ref.shape).astype(I32)


def moe_router(h, nw, w_group, b_group, w_expert, b_expert):
    t, d = h.shape
    tq = ROUTE_TILE
    nr = N_EGROUPS + N_EXPERTS
    wr = jnp.pad(jnp.concatenate([w_group, w_expert], axis=1).T, ((0, LANES - nr), (0, 0)))
    br = jnp.pad(jnp.concatenate([b_group, b_expert]), (0, LANES - nr))
    br = jnp.broadcast_to(br[:, None], (LANES, LANES))
    upper = (jnp.arange(tq)[:, None] < jnp.arange(tq)[None, :]).astype(BF16)
    rows = 2 * tq // LANES
    tile_spec = pl.BlockSpec((1, rows, LANES), lambda i: (i, 0, 0))
    tile_shape = jax.ShapeDtypeStruct((t // tq, rows, LANES), I32)
    eid, rank, gate, cnt = pl.pallas_call(
        _router_kernel,
        grid=(t // tq,),
        in_specs=[pl.BlockSpec((tq, d), lambda i: (i, 0)),
                  pl.BlockSpec((1, d), lambda i: (0, 0)),
                  pl.BlockSpec((LANES, d), lambda i: (0, 0)),
                  pl.BlockSpec((LANES, LANES), lambda i: (0, 0)),
                  pl.BlockSpec((tq, tq), lambda i: (0, 0))],
        out_specs=[tile_spec, tile_spec, pl.BlockSpec((2, tq), lambda i: (0, i)),
                   pl.BlockSpec((N_EXPERTS, LANES), lambda i: (0, 0))],
        out_shape=[tile_shape, tile_shape, jax.ShapeDtypeStruct((2, t), F32),
                   jax.ShapeDtypeStruct((N_EXPERTS, LANES), I32)],
        scratch_shapes=[pltpu.VMEM((N_EXPERTS, LANES), F32)],
        compiler_params=_cparams("arbitrary"),
        name="moe_router",
    )(h, nw.reshape(1, d), wr, br, upper)
    return eid, rank, gate, cnt[:, 0]


def _slot_kernel(pstart_ref, eid_ref, rank_ref, pos_ref):
    eid = eid_ref[...]

    def body(e, acc):
        return jnp.where(eid == e, pstart_ref[e], acc)

    pos_ref[...] = rank_ref[...] + lax.fori_loop(0, N_EXPERTS, body, jnp.zeros(eid.shape, I32))


def moe_slots(pstart, eid_tiles, rank_tiles):
    nt, rows, lanes = eid_tiles.shape
    tiles_per_step = math.gcd(nt, 16)
    blk = pl.BlockSpec((tiles_per_step, rows, lanes), lambda i, ps: (i, 0, 0))
    return pl.pallas_call(
        _slot_kernel,
        grid_spec=pltpu.PrefetchScalarGridSpec(num_scalar_prefetch=1, grid=(nt // tiles_per_step,),
                                               in_specs=[blk, blk], out_specs=blk),
        out_shape=jax.ShapeDtypeStruct(eid_tiles.shape, I32),
        compiler_params=_cparams("arbitrary"),
        name="moe_slots",
    )(pstart, eid_tiles, rank_tiles)


def _row_copy(src, src_row, dst, dst_row, sem):
    return pltpu.make_async_copy(src.at[pl.ds(src_row, 1)], dst.at[pl.ds(dst_row, 1)], sem)


def _dispatch_kernel(lo_ref, hi_ref, nu_ref, pos_hbm, h_ref, xb_out, pos_smem, zero_ref, sem_idx,
                     sem_rows, sem_fill):
    tq = h_ref.shape[0]
    sub = tq // LANES
    i = pl.program_id(0)
    n_blocks = xb_out.shape[0] // MOE_BLOCK

    def zero_rows(start, size, wait):
        cp = pltpu.make_async_copy(zero_ref.at[pl.ds(0, size)], xb_out.at[pl.ds(start, size)],
                                   sem_fill)
        cp.wait() if wait else cp.start()

    def fill_padding(e, wait):
        lo, hi = lo_ref[e], hi_ref[e]
        lo8 = jnp.minimum((lo + (SUBLANES - 1)) & -SUBLANES, hi)
        for u in range(SUBLANES - 1):
            @pl.when(lo + u < lo8)
            def _(u=u):
                zero_rows(lo + u, 1, wait)
        cur = lo8
        size = MOE_BLOCK // 2
        while size >= SUBLANES:
            take = ((hi - lo8) & size) != 0

            @pl.when(take)
            def _(cur=cur, size=size):
                zero_rows(pl.multiple_of(cur, SUBLANES), size, wait)
            cur = cur + jnp.where(take, size, 0)
            size //= 2

    def fill_tail(blk, wait):
        zero_rows(pl.multiple_of(blk * MOE_BLOCK, MOE_BLOCK), MOE_BLOCK, wait)

    @pl.when(i == 0)
    def _():
        zero_ref[...] = jnp.zeros(zero_ref.shape, F32)
        for wait in (False, True):
            def per_expert(e, carry, wait=wait):
                fill_padding(e, wait)
                return carry

            def per_block(blk, carry, wait=wait):
                fill_tail(blk, wait)
                return carry

            lax.fori_loop(0, N_EXPERTS, per_expert, 0)
            lax.fori_loop(nu_ref[0], n_blocks, per_block, 0)

    cp = pltpu.make_async_copy(pos_hbm.at[i], pos_smem, sem_idx)
    cp.start()
    cp.wait()
    for rr in range(sub):
        def body(cu, carry, rr=rr):
            for u in range(DMA_UNROLL):
                cc = cu * DMA_UNROLL + u
                r = rr * LANES + cc
                _row_copy(h_ref, r, xb_out, pos_smem[rr, cc], sem_rows).start()
                _row_copy(h_ref, r, xb_out, pos_smem[sub + rr, cc], sem_rows).start()
            return carry
        lax.fori_loop(0, LANES // DMA_UNROLL, body, 0)
    pltpu.make_async_copy(h_ref, xb_out.at[pl.ds(0, tq)], sem_rows).wait()
    pltpu.make_async_copy(h_ref, xb_out.at[pl.ds(0, tq)], sem_rows).wait()


def moe_dispatch(h, pos_tiles, pad_lo, pad_hi, n_used, n_pad):
    t, d = h.shape
    tq = ROUTE_TILE
    grid_spec = pltpu.PrefetchScalarGridSpec(
        num_scalar_prefetch=3,
        grid=(t // tq,),
        in_specs=[pl.BlockSpec(memory_space=pl.ANY),
                  pl.BlockSpec((tq, d), lambda i, lo, hi, nu: (i, 0))],
        out_specs=pl.BlockSpec(memory_space=pl.ANY),
        scratch_shapes=[pltpu.SMEM((2 * tq // LANES, LANES), I32),
                        pltpu.VMEM((MOE_BLOCK, d), F32),
                        pltpu.SemaphoreType.DMA, pltpu.SemaphoreType.DMA,
                        pltpu.SemaphoreType.DMA],
    )
    return pl.pallas_call(
        _dispatch_kernel,
        grid_spec=grid_spec,
        out_shape=jax.ShapeDtypeStruct((n_pad, d), F32),
        compiler_params=_cparams("arbitrary"),
        name="moe_dispatch",
    )(pad_lo, pad_hi, n_used, pos_tiles, h)


def _expert_kernel(be_ref, nu_ref, x_ref, nw_ref, w1_ref, w3_ref, w2_ref, o_ref,
                   w1b_ref, w3b_ref, w2b_ref):
    b = pl.program_id(0)
    used = b < nu_ref[0]
    new_expert = (b == 0) | (be_ref[b] != be_ref[jnp.maximum(b - 1, 0)])

    @pl.when(used & new_expert)
    def _():
        w1b_ref[...] = w1_ref[0].astype(BF16)
        w3b_ref[...] = w3_ref[0].astype(BF16)
        w2b_ref[...] = w2_ref[0].astype(BF16)

    @pl.when(used)
    def _():
        xb = _rms(x_ref[...], nw_ref[...]).astype(BF16)
        h1 = jnp.dot(xb, w1b_ref[...], preferred_element_type=F32)
        h3 = jnp.dot(xb, w3b_ref[...], preferred_element_type=F32)
        hid = (_silu(h1) * h3).astype(BF16)
        o_ref[...] = jnp.dot(hid, w2b_ref[...], preferred_element_type=F32)

    @pl.when(b >= nu_ref[0])
    def _():
        o_ref[...] = jnp.zeros(o_ref.shape, F32)


def moe_experts(xb, nw, w1, w3, w2, blk_expert, n_used):
    n_pad, d = xb.shape
    de = w1.shape[2]
    nb = n_pad // MOE_BLOCK
    grid_spec = pltpu.PrefetchScalarGridSpec(
        num_scalar_prefetch=2,
        grid=(nb,),
        in_specs=[pl.BlockSpec((MOE_BLOCK, d), lambda b, be, nu: (jnp.minimum(b, nu[0] - 1), 0)),
                  pl.BlockSpec((1, d), lambda b, be, nu: (0, 0)),
                  pl.BlockSpec((1, d, de), lambda b, be, nu: (be[b], 0, 0)),
                  pl.BlockSpec((1, d, de), lambda b, be, nu: (be[b], 0, 0)),
                  pl.BlockSpec((1, de, d), lambda b, be, nu: (be[b], 0, 0))],
        out_specs=pl.BlockSpec((MOE_BLOCK, d), lambda b, be, nu: (b, 0)),
        scratch_shapes=[pltpu.VMEM((d, de), BF16), pltpu.VMEM((d, de), BF16),
                        pltpu.VMEM((de, d), BF16)],
    )
    return pl.pallas_call(
        _expert_kernel,
        grid_spec=grid_spec,
        out_shape=jax.ShapeDtypeStruct((n_pad, d), F32),
        compiler_params=_cparams("arbitrary"),
        name="moe_experts",
    )(blk_expert, n_used, xb, nw.reshape(1, d), w1, w3, w2)


def _combine_ple_kernel(pos_hbm, yb_hbm, h_ref, gate_ref, p_ref, nw_ref, wg_ref, wp_ref, fw_ref,
                        o_ref, pos_smem, buf0_ref, buf1_ref, sem_idx, sem_rows, *, final):
    tq = h_ref.shape[0]
    sub = tq // LANES
    i = pl.program_id(0)
    cp = pltpu.make_async_copy(pos_hbm.at[i], pos_smem, sem_idx)
    cp.start()
    cp.wait()
    for rr in range(sub):
        def body(cu, carry, rr=rr):
            for u in range(DMA_UNROLL):
                cc = cu * DMA_UNROLL + u
                r = rr * LANES + cc
                _row_copy(yb_hbm, pos_smem[rr, cc], buf0_ref, r, sem_rows).start()
                _row_copy(yb_hbm, pos_smem[sub + rr, cc], buf1_ref, r, sem_rows).start()
            return carry
        lax.fori_loop(0, LANES // DMA_UNROLL, body, 0)
    pltpu.make_async_copy(yb_hbm.at[pl.ds(0, tq)], buf0_ref, sem_rows).wait()
    pltpu.make_async_copy(yb_hbm.at[pl.ds(0, tq)], buf1_ref, sem_rows).wait()

    gate = gate_ref[...]
    h2 = h_ref[...] + gate[:, 0:1] * buf0_ref[...] + gate[:, 1:2] * buf1_ref[...]
    hn = _rms(h2, nw_ref[...]).astype(BF16)
    pg = jax.nn.sigmoid(jnp.dot(hn, wg_ref[...], preferred_element_type=F32))
    pp = jnp.dot(p_ref[...].astype(BF16), wp_ref[...], preferred_element_type=F32)
    out = h2 + pg * pp
    if final:
        out = _rms(out, fw_ref[...])
    o_ref[...] = out


def moe_combine_ple(h, yb, pos_tiles, gate_cols, p, nw, wg, wp, fw, *, final):
    t, d = h.shape
    tq = ROUTE_TILE
    pd = p.shape[1]
    return pl.pallas_call(
        functools.partial(_combine_ple_kernel, final=final),
        grid=(t // tq,),
        in_specs=[pl.BlockSpec(memory_space=pl.ANY),
                  pl.BlockSpec(memory_space=pl.ANY),
                  pl.BlockSpec((tq, d), lambda i: (i, 0)),
                  pl.BlockSpec((tq, 2), lambda i: (i, 0)),
                  pl.BlockSpec((tq, pd), lambda i: (i, 0)),
                  pl.BlockSpec((1, d), lambda i: (0, 0)),
                  pl.BlockSpec((d, d), lambda i: (0, 0)),
                  pl.BlockSpec((pd, d), lambda i: (0, 0)),
                  pl.BlockSpec((1, d), lambda i: (0, 0))],
        out_specs=pl.BlockSpec((tq, d), lambda i: (i, 0)),
        out_shape=jax.ShapeDtypeStruct((t, d), F32),
        scratch_shapes=[pltpu.SMEM((2 * tq // LANES, LANES), I32),
                        pltpu.VMEM((tq, d), F32), pltpu.VMEM((tq, d), F32),
                        pltpu.SemaphoreType.DMA, pltpu.SemaphoreType.DMA],
        compiler_params=_cparams("arbitrary"),
        name="moe_combine_ple",
    )(pos_tiles, yb, h, gate_cols, p, nw.reshape(1, d), wg, wp, fw.reshape(1, d))


def moe_ple_layer(h, p, norm_moe, w_group, b_group, w_expert, b_expert, w1, w3, w2,
                  norm_ple, wg, wp, fw, *, layer, final):
    t, d = h.shape
    eid_tiles, rank_tiles, gate, counts = moe_router(h, norm_moe, w_group, b_group, w_expert,
                                                     b_expert)
    padded = (counts + MOE_BLOCK - 1) // MOE_BLOCK * MOE_BLOCK
    pend = jnp.cumsum(padded)
    pstart = pend - padded
    na = 2 * t
    n_pad = (na + MOE_BLOCK - 1) // MOE_BLOCK * MOE_BLOCK + N_EXPERTS * MOE_BLOCK
    nb = n_pad // MOE_BLOCK
    blk_start = jnp.arange(nb, dtype=I32) * MOE_BLOCK
    blk_expert = jnp.minimum(jnp.sum(blk_start[:, None] >= pend[None, :], axis=-1),
                             N_EXPERTS - 1).astype(I32)
    n_used = (pend[-1:] // MOE_BLOCK).astype(I32)
    pos_tiles = moe_slots(pstart.astype(I32), eid_tiles, rank_tiles)
    xb = moe_dispatch(h, pos_tiles, (pstart + counts).astype(I32), pend.astype(I32), n_used, n_pad)
    yb = moe_experts(xb, norm_moe, w1, w3, w2, blk_expert + layer * N_EXPERTS, n_used)
    return moe_combine_ple(h, yb, pos_tiles, gate.T, p, norm_ple, wg.astype(BF16),
                           wp.astype(BF16), fw, final=final)


def _pad_cols(w, n):
    return jnp.pad(w, ((0, 0), (0, n - w.shape[1])))


def kernel(x, p, norm_mix, norm_moe, norm_ple, final_norm, m_in_w, m_conv_w, m_conv_b, m_dt_bias, m_A_log, m_D, m_norm_w, m_out_w, g_in_w, g_conv_w, g_dt_bias, g_A_log, g_norm_w, g_out_w, moe_w_group, moe_b_group, moe_w_expert, moe_b_expert, moe_w1, moe_w3, moe_w2, ple_w_proj, ple_w_gate):
    batch, seq, d = x.shape
    t = batch * seq
    depth = p.shape[0]
    pd = p.shape[-1]
    h = x.reshape(t, d)
    p2 = p.reshape(depth, t, pd)
    w1_all = moe_w1.reshape((-1,) + moe_w1.shape[2:])
    w3_all = moe_w3.reshape((-1,) + moe_w3.shape[2:])
    w2_all = moe_w2.reshape((-1,) + moe_w2.shape[2:])
    for i in range(depth):
        j = i // 2
        if i % 2 == 0:
            inner = m_out_w.shape[1]
            heads = m_dt_bias.shape[1]
            conv_dim = m_conv_w.shape[2]
            w_in = _pad_cols(m_in_w[j], inner + conv_dim + LANES).astype(BF16)
            proj = norm_matmul(h, norm_mix[i], w_in)
            y = ssd_mixer_core(proj, m_conv_w[j], m_conv_b[j], m_dt_bias[j], m_A_log[j], m_D[j],
                               m_norm_w[j], batch=batch, seq=seq, inner=inner, heads=heads)
            h = matmul_residual(y, m_out_w[j].astype(BF16), h)
        else:
            v_dim = g_out_w.shape[1]
            v_heads = g_dt_bias.shape[1]
            conv_dim = g_conv_w.shape[2]
            qk_dim = (conv_dim - v_dim) // 2
            w_in = _pad_cols(g_in_w[j], conv_dim + v_dim + LANES).astype(BF16)
            proj = norm_matmul(h, norm_mix[i], w_in)
            y = gdn_mixer_core(proj, g_conv_w[j], g_dt_bias[j], g_A_log[j], g_norm_w[j],
                               batch=batch, seq=seq, qk_dim=qk_dim, v_dim=v_dim, v_heads=v_heads)
            h = matmul_residual(y, g_out_w[j].astype(BF16), h)
        h = moe_ple_layer(h, p2[i], norm_moe[i], moe_w_group[i], moe_b_group[i], moe_w_expert[i],
                          moe_b_expert[i], w1_all, w3_all, w2_all, norm_ple[i],
                          ple_w_gate[i], ple_w_proj[i], final_norm, layer=i,
                          final=(i == depth - 1))
    return h.reshape(batch, seq, d)
```

```python
import functools
import math

import jax
import jax.numpy as jnp
from jax import lax
from jax.experimental import pallas as pl
from jax.experimental.pallas import tpu as pltpu

F32 = jnp.float32
BF16 = jnp.bfloat16
I32 = jnp.int32
EPS = 1e-6
HIGHEST = lax.Precision.HIGHEST

LANES = 128
SUBLANES = 8
VMEM_LIMIT = 56 * 1024 * 1024

CONV_K = 4
SSD_CHUNK = 128
SSD_HEADDIM = 64
SSD_STATE = 128
SSD_GROUPS = 4
GDN_CHUNK = 64
GDN_HEAD = 128
GDN_STACK = 256
GDN_SUBBLOCK = 16
GDN_BATCH_PER_STEP = 1
N_EGROUPS = 8
EXPERTS_PER_GROUP = 8
N_EXPERTS = N_EGROUPS * EXPERTS_PER_GROUP
MOE_BLOCK = 512
ROUTE_TILE = 512
DMA_UNROLL = 8


def _cparams(*sem):
    return pltpu.CompilerParams(dimension_semantics=sem, vmem_limit_bytes=VMEM_LIMIT)


def _silu(x):
    return x * jax.nn.sigmoid(x)


def _softplus(x):
    return jnp.maximum(x, 0.0) + jnp.log1p(jnp.exp(-jnp.abs(x)))


def _rms(x, w):
    return x * lax.rsqrt(jnp.mean(x * x, axis=-1, keepdims=True) + EPS) * w


def _dot(a, b):
    return jnp.dot(a.astype(BF16), b.astype(BF16), preferred_element_type=F32)


def _split(x):
    hi = x.astype(BF16)
    return hi, (x - hi.astype(F32)).astype(BF16)


def _dot_f32(a, b):
    return jnp.dot(a, b, preferred_element_type=F32, precision=HIGHEST)


def _dot_tn(a, b):
    return lax.dot_general(a.astype(BF16), b.astype(BF16), (((0,), (0,)), ((), ())),
                           preferred_element_type=F32)


def _dot_nt(a, b):
    return lax.dot_general(a.astype(BF16), b.astype(BF16), (((1,), (1,)), ((), ())),
                           preferred_element_type=F32)


def _tri(n, strict=False):
    r = lax.broadcasted_iota(I32, (n, n), 0)
    c = lax.broadcasted_iota(I32, (n, n), 1)
    return (r > c) if strict else (r >= c)


def _norm_matmul_kernel(x_ref, nw_ref, w_ref, o_ref, *, n_chunk):
    xb = _rms(x_ref[...], nw_ref[...]).astype(BF16)
    n = o_ref.shape[1]
    for c0 in range(0, n, n_chunk):
        c1 = min(c0 + n_chunk, n)
        o_ref[:, c0:c1] = jnp.dot(xb, w_ref[:, c0:c1], preferred_element_type=F32)


def norm_matmul(x, nw, w, *, tm=256, n_chunk=512):
    t, d = x.shape
    n = w.shape[1]
    return pl.pallas_call(
        functools.partial(_norm_matmul_kernel, n_chunk=n_chunk),
        grid=(t // tm,),
        in_specs=[pl.BlockSpec((tm, d), lambda i: (i, 0)),
                  pl.BlockSpec((1, d), lambda i: (0, 0)),
                  pl.BlockSpec((d, n), lambda i: (0, 0))],
        out_specs=pl.BlockSpec((tm, n), lambda i: (i, 0)),
        out_shape=jax.ShapeDtypeStruct((t, n), F32),
        compiler_params=_cparams("arbitrary"),
        name="norm_matmul",
    )(x, nw.reshape(1, d), w)


def _matmul_residual_kernel(y_ref, w_ref, r_ref, o_ref):
    o_ref[...] = r_ref[...] + jnp.dot(y_ref[...].astype(BF16), w_ref[...],
                                      preferred_element_type=F32)


def matmul_residual(y, w, res, *, tm=512):
    t, k = y.shape
    d = w.shape[1]
    return pl.pallas_call(
        _matmul_residual_kernel,
        grid=(t // tm,),
        in_specs=[pl.BlockSpec((tm, k), lambda i: (i, 0)),
                  pl.BlockSpec((k, d), lambda i: (0, 0)),
                  pl.BlockSpec((tm, d), lambda i: (i, 0))],
        out_specs=pl.BlockSpec((tm, d), lambda i: (i, 0)),
        out_shape=jax.ShapeDtypeStruct((t, d), F32),
        compiler_params=_cparams("arbitrary"),
        name="matmul_residual",
    )(y, w, res)


def _conv_silu(x_ref, stage_ref, w_ref, bias, first):
    q = x_ref.shape[0]

    @pl.when(first)
    def _():
        stage_ref[0:SUBLANES, :] = jnp.zeros((SUBLANES, stage_ref.shape[1]), F32)

    stage_ref[SUBLANES:SUBLANES + q, :] = x_ref[...]
    acc = stage_ref[SUBLANES:SUBLANES + q, :] * w_ref[CONV_K - 1:CONV_K, :]
    for j in range(CONV_K - 1):
        off = SUBLANES - (CONV_K - 1) + j
        acc = acc + stage_ref[off:off + q, :] * w_ref[j:j + 1, :]
    if bias is not None:
        acc = acc + bias
    stage_ref[0:SUBLANES, :] = stage_ref[q:q + SUBLANES, :]
    return _silu(acc)


def _ssd_kernel(z_ref, x_ref, b_ref, c_ref, dt_ref,
                cwx_ref, cwb_ref, cwc_ref, cbx_ref, cbb_ref, cbc_ref,
                dtb_ref, alog_ref, dfull_ref, nw_ref,
                o_ref,
                sx_ref, sb_ref, sc_ref, y_ref, xw_ref, st_ref):
    q = x_ref.shape[0]
    hpg = x_ref.shape[1] // (SSD_GROUPS * SSD_HEADDIM)
    first = pl.program_id(1) == 0

    @pl.when(first)
    def _():
        st_ref[...] = jnp.zeros(st_ref.shape, F32)

    xs = _conv_silu(x_ref, sx_ref, cwx_ref, cbx_ref[...], first)
    bm = _conv_silu(b_ref, sb_ref, cwb_ref, cbb_ref[...], first)
    cm = _conv_silu(c_ref, sc_ref, cwc_ref, cbc_ref[...], first)

    dt = _softplus(dt_ref[...] + dtb_ref[...])
    da = dt * (-jnp.exp(alog_ref[...]))
    causal = _tri(q)
    a = _dot_f32(causal.astype(F32), da)
    a_t = a.T
    ea_t = jnp.exp(a_t)
    a_last = a[q - 1:q, :]
    to_end_t = (jnp.exp(a_last - a) * dt).T
    ea_last = jnp.exp(a_last)
    dt_t = dt.T
    eye = (lax.broadcasted_iota(I32, (q, q), 0) == lax.broadcasted_iota(I32, (q, q), 1))
    lane_lo = lax.broadcasted_iota(I32, (q, 2 * SSD_HEADDIM), 1) < SSD_HEADDIM

    for g in range(SSD_GROUPS):
        bg = bm[:, g * SSD_STATE:(g + 1) * SSD_STATE]
        cg = cm[:, g * SSD_STATE:(g + 1) * SSD_STATE]
        cb = _dot_nt(cg, bg)
        gw = hpg * SSD_HEADDIM
        cs = _dot(cg, st_ref[g])
        for pr in range(hpg // 2):
            j0 = g * hpg + 2 * pr
            lo, hi = j0 * SSD_HEADDIM, (j0 + 2) * SSD_HEADDIM
            xp = xs[:, lo:hi]
            cp = cs[:, 2 * pr * SSD_HEADDIM:(2 * pr + 2) * SSD_HEADDIM]
            x_sel = [jnp.where(lane_lo, xp, 0.0), jnp.where(lane_lo, 0.0, xp)]
            c_sel = [jnp.where(lane_lo, cp, 0.0), jnp.where(lane_lo, 0.0, cp)]
            lhs, rhs, te = [], [], []
            for k in range(2):
                j = j0 + k
                diff = a[:, j:j + 1] - a_t[j:j + 1, :]
                seg = jnp.exp(jnp.where(causal, diff, -jnp.inf))
                lhs += [cb * seg * dt_t[j:j + 1, :], jnp.where(eye, ea_t[j:j + 1, :], 0.0)]
                rhs += [x_sel[k], c_sel[k]]
                te.append(jnp.where(eye, to_end_t[j:j + 1, :], 0.0))
            y_ref[:, lo:hi] = _dot(jnp.concatenate(lhs, axis=1), jnp.concatenate(rhs, axis=0))
            xw_ref[:, lo:hi] = _dot(jnp.concatenate(te, axis=1), jnp.concatenate(x_sel, axis=0))
        upd = _dot_tn(bg, xw_ref[:, g * gw:(g + 1) * gw])
        for jj in range(hpg):
            j = g * hpg + jj
            sl = slice(jj * SSD_HEADDIM, (jj + 1) * SSD_HEADDIM)
            st_ref[g, :, sl] = st_ref[g, :, sl] * ea_last[:, j:j + 1] + upd[:, sl]

    y = y_ref[...] + dfull_ref[...] * xs
    o_ref[...] = _rms(y * _silu(z_ref[...]), nw_ref[...])


def ssd_mixer_core(proj, conv_w, conv_b, dt_bias, a_log, d_skip, norm_w, *, batch, seq,
                   inner, heads):
    t = proj.shape[0]
    q = SSD_CHUNK
    gn = SSD_GROUPS * SSD_STATE
    nc = seq // q
    cwx, cwb, cwc = conv_w[:, :inner], conv_w[:, inner:inner + gn], conv_w[:, inner + gn:]
    cbx = conv_b[:inner].reshape(1, inner)
    cbb = conv_b[inner:inner + gn].reshape(1, gn)
    cbc = conv_b[inner + gn:].reshape(1, gn)
    pad = LANES - heads
    dtb = jnp.pad(dt_bias, (0, pad)).reshape(1, LANES)
    alog = jnp.pad(a_log, (0, pad)).reshape(1, LANES)
    dfull = jnp.repeat(d_skip, SSD_HEADDIM).reshape(1, inner)
    nw = norm_w.reshape(1, inner)

    def row(i, c):
        return i * nc + c

    const = lambda shape: pl.BlockSpec(shape, lambda i, c: (0, 0))
    x_blk = inner // inner
    b_blk = (2 * inner) // gn
    c_blk = b_blk + 1
    dt_blk = (2 * inner + 2 * gn) // LANES
    return pl.pallas_call(
        _ssd_kernel,
        grid=(batch, nc),
        in_specs=[pl.BlockSpec((q, inner), lambda i, c: (row(i, c), 0)),
                  pl.BlockSpec((q, inner), lambda i, c: (row(i, c), x_blk)),
                  pl.BlockSpec((q, gn), lambda i, c: (row(i, c), b_blk)),
                  pl.BlockSpec((q, gn), lambda i, c: (row(i, c), c_blk)),
                  pl.BlockSpec((q, LANES), lambda i, c: (row(i, c), dt_blk)),
                  const((CONV_K, inner)), const((CONV_K, gn)), const((CONV_K, gn)),
                  const((1, inner)), const((1, gn)), const((1, gn)),
                  const((1, LANES)), const((1, LANES)), const((1, inner)), const((1, inner))],
        out_specs=pl.BlockSpec((q, inner), lambda i, c: (row(i, c), 0)),
        out_shape=jax.ShapeDtypeStruct((t, inner), F32),
        scratch_shapes=[pltpu.VMEM((q + SUBLANES, inner), F32),
                        pltpu.VMEM((q + SUBLANES, gn), F32),
                        pltpu.VMEM((q + SUBLANES, gn), F32),
                        pltpu.VMEM((q, inner), F32),
                        pltpu.VMEM((q, inner), F32),
                        pltpu.VMEM((SSD_GROUPS, SSD_STATE, inner // SSD_GROUPS), F32)],
        compiler_params=_cparams("arbitrary", "arbitrary"),
        name="ssd_scan",
    )(proj, proj, proj, proj, proj, cwx, cwb, cwc, cbx, cbb, cbc, dtb, alog, dfull, nw)


def _l2n(x):
    return x * lax.rsqrt(jnp.sum(x * x, axis=-1, keepdims=True) + EPS)


def _pair_cols(x, i0, i1, half):
    c = x.shape[0]
    lane = lax.broadcasted_iota(I32, (c, 2 * half), 1)
    return jnp.where(lane < half, jnp.broadcast_to(x[:, i0:i0 + 1], (c, 2 * half)),
                     jnp.broadcast_to(x[:, i1:i1 + 1], (c, 2 * half)))


def _gdn_kernel(q_ref, k_ref, v_ref, z_ref, ba_ref,
                cwq_ref, cwk_ref, cwv_ref, dtb_ref, alog_ref, nw_ref,
                o_ref,
                sq_ref, sk_ref, sv_ref, s_ref, *, v_heads):
    nb, c = q_ref.shape[0], q_ref.shape[1]
    qk_heads = q_ref.shape[2] // GDN_HEAD
    rep = v_heads // qk_heads
    gsz = GDN_STACK // c
    n_grp = v_heads // gsz
    first = pl.program_id(1) == 0

    @pl.when(first)
    def _():
        s_ref[...] = jnp.zeros(s_ref.shape, F32)

    t_idx = lax.broadcasted_iota(I32, (c, GDN_STACK), 0)
    s_idx = lax.broadcasted_iota(I32, (c, GDN_STACK), 1) & (c - 1)
    causal = t_idx >= s_idx
    strict = t_idx > s_idx
    eye = t_idx == s_idx
    eye_f = eye.astype(F32)
    sb_shift = GDN_SUBBLOCK.bit_length() - 1
    same_blk = (t_idx >> sb_shift) == (s_idx >> sb_shift)
    n_col = GDN_STACK // LANES
    lane_lo = lax.broadcasted_iota(I32, (c, LANES), 1) < c

    def bdiag(parts):
        zero = jnp.zeros((), BF16)
        cols = lambda p: [p[:, k * LANES:(k + 1) * LANES] for k in range(n_col)]
        return tuple([jnp.concatenate([jnp.where(lane_lo, y, zero), jnp.where(lane_lo, zero, y)],
                                      axis=0) for y in cols(p)] for p in parts)

    def mm(lhs, rhs_bd):
        n = len(lhs) * c
        parts = [_split(x) for x in lhs]
        l_hi = jnp.concatenate([p[0] for p in parts], axis=0)
        l_both = jnp.concatenate([l_hi] + [p[1] for p in parts], axis=0)
        outs = []
        for k in range(n_col):
            sl = slice(k * LANES, (k + 1) * LANES)
            r = jnp.dot(l_both[:, sl], rhs_bd[0][k], preferred_element_type=F32)
            outs.append(r[:n] + r[n:] + jnp.dot(l_hi[:, sl], rhs_bd[1][k],
                                                preferred_element_type=F32))
        out = jnp.concatenate(outs, axis=1)
        return [out[i * c:(i + 1) * c] for i in range(len(lhs))]

    def neumann(p0s, n_sq):
        ps = list(p0s)
        ds = [eye_f + p for p in ps]
        for i in range(n_sq):
            for g in range(len(ps)):
                bd = bdiag(_split(ps[g]))
                if i == 0:
                    ps[g], = mm([ps[g]], bd)
                else:
                    ps[g], x = mm([ps[g], ds[g]], bd)
                    ds[g] = ds[g] + x
        if n_sq > 0:
            for g in range(len(ps)):
                x, = mm([ds[g]], bdiag(_split(ps[g])))
                ds[g] = ds[g] + x
        return ds

    def prepare(b):
        qc = _conv_silu(q_ref.at[b], sq_ref.at[b], cwq_ref, None, first)
        kc = _conv_silu(k_ref.at[b], sk_ref.at[b], cwk_ref, None, first)
        vc = _conv_silu(v_ref.at[b], sv_ref.at[b], cwv_ref, None, first)
        ba = ba_ref[b]
        beta = jax.nn.sigmoid(ba)
        gl = -jnp.exp(alog_ref[...]) * _softplus(ba + dtb_ref[...])
        gcs = _dot_f32(_tri(c).astype(F32), gl)
        g_last = gcs[c - 1:c, :]
        st = dict(vc=vc, eg_last=jnp.exp(g_last), qns=[], kns=[], kts=[], a_kks=[], a_qks=[],
                  gcols=[], brows=[], grows=[])
        kks, qks = [], []
        for hq in range(qk_heads):
            sl = slice(hq * GDN_HEAD, (hq + 1) * GDN_HEAD)
            qn = _l2n(qc[:, sl]) * (GDN_HEAD ** -0.5)
            kn = _l2n(kc[:, sl])
            k2 = jnp.concatenate([kn] * rep, axis=0)
            st["qns"].append(qn)
            st["kns"].append(kn)
            st["kts"].append(kn.T.astype(BF16))
            kks.append(_dot_nt(kn, k2))
            qks.append(_dot_nt(qn, k2))
        for grp in range(n_grp):
            h0 = grp * gsz
            hq0 = h0 // rep
            nq = gsz // rep
            kk = jnp.concatenate(kks[hq0:hq0 + nq], axis=1)
            qk = jnp.concatenate(qks[hq0:hq0 + nq], axis=1)
            gcol = jnp.concatenate(
                [_pair_cols(gcs, v_heads + h0 + 2 * i, v_heads + h0 + 2 * i + 1, c)
                 for i in range(gsz // 2)], axis=1)
            bcol = jnp.concatenate([_pair_cols(beta, h0 + 2 * i, h0 + 2 * i + 1, c)
                                    for i in range(gsz // 2)], axis=1)
            grow = jnp.sum(jnp.where(eye, gcol, 0.0), axis=0, keepdims=True)
            decay = jnp.exp(jnp.where(causal, gcol - grow, -jnp.inf))
            st["a_kks"].append(jnp.where(strict, bcol * kk * decay, 0.0))
            st["a_qks"].append(jnp.where(causal, qk * decay, 0.0))
            st["gcols"].append(gcol)
            st["grows"].append(grow)
            st["brows"].append(jnp.sum(jnp.where(eye, bcol, 0.0), axis=0, keepdims=True))
        return st

    def pair_blocks(b0, b1):
        z = jnp.zeros((c, GDN_HEAD), BF16)
        return jnp.concatenate([jnp.concatenate([b0.astype(BF16), z], axis=1),
                                jnp.concatenate([z, b1.astype(BF16)], axis=1)], axis=0)

    def heads(b, st, grp, tinv):
        h0 = grp * gsz
        gcol, grow, brow = st["gcols"][grp], st["grows"][grp], st["brows"][grp]
        hd = lambda x, i: x[:, i * GDN_HEAD:(i + 1) * GDN_HEAD]
        col = lambda x, k: x[:, k * LANES:(k + 1) * LANES]
        tb = tinv * brow
        tbe = tinv * (brow * jnp.exp(grow))
        us, ws_ = [], []
        for k in range(n_col):
            ha = h0 + 2 * k
            us.append(_dot(col(tb, k), pair_blocks(hd(st["vc"], ha), hd(st["vc"], ha + 1))))
            ws_.append(_dot(col(tbe, k), pair_blocks(st["kns"][ha // rep],
                                                      st["kns"][(ha + 1) // rep])))
        u = jnp.concatenate(us, axis=1)
        w = jnp.concatenate(ws_, axis=1)
        yield
        v_news, q_states = [], []
        for i in range(gsz):
            hq = (h0 + i) // rep
            ws = _dot(jnp.concatenate([hd(w, i), st["qns"][hq]], axis=0), s_ref[b, h0 + i])
            v_news.append(hd(u, i) - ws[:c])
            q_states.append(ws[c:])
        yield
        eg_diag =jnp.where(eye, jnp.exp(gcol), 0.0)
        eg_end_diag = jnp.where(eye, jnp.exp(gcol[c - 1:c, :] - gcol), 0.0)
        os_, vss = [], []
        for k in range(n_col):
            pv = pair_blocks(v_news[2 * k], v_news[2 * k + 1])
            pq = pair_blocks(q_states[2 * k], q_states[2 * k + 1])
            os_.append(_dot(jnp.concatenate([col(st["a_qks"][grp], k), col(eg_diag, k)], axis=1),
                            jnp.concatenate([pv, pq], axis=0)))
            vss.append(_dot(col(eg_end_diag, k), pv))
        o = jnp.concatenate(os_, axis=1)
        v_scaled = jnp.concatenate(vss, axis=1)
        yield
        for i in range(gsz):
            h = h0 + i
            gi = v_heads + h
            hs = slice(h * GDN_HEAD, (h + 1) * GDN_HEAD)
            s_ref[b, h] = (s_ref[b, h] * st["eg_last"][:, gi:gi + 1]
                           + _dot(st["kts"][h // rep], hd(v_scaled, i)))
            o_ref[b, :, hs] = _rms(hd(o, i), nw_ref[...]) * _silu(z_ref[b, :, hs])

    sts = [prepare(b) for b in range(nb)]

    a_all = [a for st in sts for a in st["a_kks"]]
    dms = neumann([jnp.where(same_blk, -a, 0.0) for a in a_all], sb_shift - 1)
    ms = [-mm([d], bdiag(_split(jnp.where(same_blk, 0.0, a))))[0] for d, a in zip(dms, a_all)]
    ws = neumann(ms, (c // GDN_SUBBLOCK).bit_length() - 2)
    tinvs = [mm([w], bdiag(_split(d)))[0] for w, d in zip(ws, dms)]

    gens = [heads(b, sts[b], grp, tinvs[b * n_grp + grp]) for grp in range(n_grp)
            for b in range(nb)]
    for _ in range(4):
        for gen in gens:
            next(gen, None)


def gdn_mixer_core(proj, conv_w, dt_bias, a_log, norm_w, *, batch, seq, qk_dim, v_dim, v_heads):
    t, n = proj.shape
    c = GDN_CHUNK
    nb = GDN_BATCH_PER_STEP
    cwq, cwk, cwv = conv_w[:, :qk_dim], conv_w[:, qk_dim:2 * qk_dim], conv_w[:, 2 * qk_dim:]
    dtb = jnp.pad(dt_bias, (v_heads, LANES - 2 * v_heads)).reshape(1, LANES)
    alog = jnp.pad(a_log, (v_heads, LANES - 2 * v_heads)).reshape(1, LANES)
    nw = norm_w.reshape(1, GDN_HEAD)
    proj3 = proj.reshape(batch, seq, n)

    const = lambda shape: pl.BlockSpec(shape, lambda i, cc: (0, 0))
    blk = lambda width, col: pl.BlockSpec((nb, c, width), lambda i, cc: (i, cc, col))
    v_blk = (2 * qk_dim) // v_dim
    z_blk = v_blk + 1
    ba_blk = (2 * qk_dim + 2 * v_dim) // LANES
    out = pl.pallas_call(
        functools.partial(_gdn_kernel, v_heads=v_heads),
        grid=(batch // nb, seq // c),
        in_specs=[blk(qk_dim, 0), blk(qk_dim, 1), blk(v_dim, v_blk), blk(v_dim, z_blk),
                  blk(LANES, ba_blk),
                  const((CONV_K, qk_dim)), const((CONV_K, qk_dim)), const((CONV_K, v_dim)),
                  const((1, LANES)), const((1, LANES)), const((1, GDN_HEAD))],
        out_specs=blk(v_dim, 0),
        out_shape=jax.ShapeDtypeStruct((batch, seq, v_dim), F32),
        scratch_shapes=[pltpu.VMEM((nb, c + SUBLANES, qk_dim), F32),
                        pltpu.VMEM((nb, c + SUBLANES, qk_dim), F32),
                        pltpu.VMEM((nb, c + SUBLANES, v_dim), F32),
                        pltpu.VMEM((nb, v_heads, GDN_HEAD, GDN_HEAD), F32)],
        compiler_params=_cparams("arbitrary", "arbitrary"),
        name="gdn_scan",
    )(proj3, proj3, proj3, proj3, proj3, cwq, cwk, cwv, dtb, alog, nw)
    return out.reshape(t, v_dim)


def _first_index(mask, n):
    idx = lax.broadcasted_iota(I32, mask.shape, 0)
    return jnp.min(jnp.where(mask, idx, n), axis=0, keepdims=True)


def _router_kernel(h_ref, nw_ref, wr_ref, br_ref, upper_ref,
                   eid_ref, rank_ref, gate_ref, cnt_ref, base_ref):
    tq = h_ref.shape[0]
    i = pl.program_id(0)

    @pl.when(i == 0)
    def _():
        base_ref[...] = jnp.zeros(base_ref.shape, F32)

    xn = _rms(h_ref[...], nw_ref[...])
    logits = lax.dot_general(wr_ref[...], xn, (((1,), (1,)), ((), ())),
                             preferred_element_type=F32, precision=HIGHEST) + br_ref[:, 0:1]
    gl = logits[0:N_EGROUPS, :]
    gmax = jnp.max(gl, axis=0, keepdims=True)
    g_sel = _first_index(gl == gmax, N_EGROUPS)
    p_sel = 1.0 / jnp.sum(jnp.exp(gl - gmax), axis=0, keepdims=True)
    e_in = logits[N_EGROUPS:N_EGROUPS + EXPERTS_PER_GROUP, :]
    for g in range(1, N_EGROUPS):
        lo = N_EGROUPS + g * EXPERTS_PER_GROUP
        e_in = jnp.where(g_sel == g, logits[lo:lo + EXPERTS_PER_GROUP, :], e_in)
    m1 = jnp.max(e_in, axis=0, keepdims=True)
    i1 = _first_index(e_in == m1, EXPERTS_PER_GROUP)
    sub = lax.broadcasted_iota(I32, e_in.shape, 0)
    rest = jnp.where(sub == i1, -jnp.inf, e_in)
    m2 = jnp.max(rest, axis=0, keepdims=True)
    i2 = _first_index(rest == m2, EXPERTS_PER_GROUP)
    e2 = jnp.exp(m2 - m1)
    denom = 1.0 + e2
    gate_ref[0:1, :] = (1.0 / denom) * p_sel
    gate_ref[1:2, :] = (e2 / denom) * p_sel
    eid0 = g_sel * EXPERTS_PER_GROUP + i1
    eid1 = g_sel * EXPERTS_PER_GROUP + i2

    erow = lax.broadcasted_iota(I32, (N_EXPERTS, tq), 0)
    oh0 = erow == eid0
    oh1 = erow == eid1
    oh0f = oh0.astype(F32)
    oh1f = oh1.astype(F32)
    cum0 = jnp.dot(oh0f.astype(BF16), upper_ref[...], preferred_element_type=F32)
    cum1 = jnp.dot(oh1f.astype(BF16), upper_ref[...], preferred_element_type=F32)
    base = base_ref[:, 0:1]
    tot0 = jnp.sum(oh0f, axis=1, keepdims=True)
    tot1 = jnp.sum(oh1f, axis=1, keepdims=True)
    r0 = jnp.sum(jnp.where(oh0, base + cum0, 0.0), axis=0, keepdims=True)
    r1 = jnp.sum(jnp.where(oh1, base + tot0 + cum1, 0.0), axis=0, keepdims=True)
    sub = tq // LANES
    for k, (e, r) in enumerate(((eid0, r0.astype(I32)), (eid1, r1.astype(I32)))):
        for rr in range(sub):
            row = k * sub + rr
            eid_ref[0, row:row + 1, :] = e[:, rr * LANES:(rr + 1) * LANES]
            rank_ref[0, row:row + 1, :] = r[:, rr * LANES:(rr + 1) * LANES]
    new_base = base + tot0 + tot1
    base_ref[...] = jnp.broadcast_to(new_base, base_ref.shape)
    cnt_ref[...] = jnp.broadcast_to(new_base, cnt_ref.shape).astype(I32)


def moe_router(h, nw, w_group, b_group, w_expert, b_expert):
    t, d = h.shape
    tq = ROUTE_TILE
    nr = N_EGROUPS + N_EXPERTS
    wr = jnp.pad(jnp.concatenate([w_group, w_expert], axis=1).T, ((0, LANES - nr), (0, 0)))
    br = jnp.pad(jnp.concatenate([b_group, b_expert]), (0, LANES - nr))
    br = jnp.broadcast_to(br[:, None], (LANES, LANES))
    upper = (jnp.arange(tq)[:, None] < jnp.arange(tq)[None, :]).astype(BF16)
    rows = 2 * tq // LANES
    tile_spec = pl.BlockSpec((1, rows, LANES), lambda i: (i, 0, 0))
    tile_shape = jax.ShapeDtypeStruct((t // tq, rows, LANES), I32)
    eid, rank, gate, cnt = pl.pallas_call(
        _router_kernel,
        grid=(t // tq,),
        in_specs=[pl.BlockSpec((tq, d), lambda i: (i, 0)),
                  pl.BlockSpec((1, d), lambda i: (0, 0)),
                  pl.BlockSpec((LANES, d), lambda i: (0, 0)),
                  pl.BlockSpec((LANES, LANES), lambda i: (0, 0)),
                  pl.BlockSpec((tq, tq), lambda i: (0, 0))],
        out_specs=[tile_spec, tile_spec, pl.BlockSpec((2, tq), lambda i: (0, i)),
                   pl.BlockSpec((N_EXPERTS, LANES), lambda i: (0, 0))],
        out_shape=[tile_shape, tile_shape, jax.ShapeDtypeStruct((2, t), F32),
                   jax.ShapeDtypeStruct((N_EXPERTS, LANES), I32)],
        scratch_shapes=[pltpu.VMEM((N_EXPERTS, LANES), F32)],
        compiler_params=_cparams("arbitrary"),
        name="moe_router",
    )(h, nw.reshape(1, d), wr, br, upper)
    return eid, rank, gate, cnt[:, 0]


def _slot_kernel(pstart_ref, eid_ref, rank_ref, pos_ref):
    eid = eid_ref[...]

    def body(e, acc):
        return jnp.where(eid == e, pstart_ref[e], acc)

    pos_ref[...] = rank_ref[...] + lax.fori_loop(0, N_EXPERTS, body, jnp.zeros(eid.shape, I32))


def moe_slots(pstart, eid_tiles, rank_tiles):
    nt, rows, lanes = eid_tiles.shape
    tiles_per_step = math.gcd(nt, 16)
    blk = pl.BlockSpec((tiles_per_step, rows, lanes), lambda i, ps: (i, 0, 0))
    return pl.pallas_call(
        _slot_kernel,
        grid_spec=pltpu.PrefetchScalarGridSpec(num_scalar_prefetch=1, grid=(nt // tiles_per_step,),
                                               in_specs=[blk, blk], out_specs=blk),
        out_shape=jax.ShapeDtypeStruct(eid_tiles.shape, I32),
        compiler_params=_cparams("arbitrary"),
        name="moe_slots",
    )(pstart, eid_tiles, rank_tiles)


def _row_copy(src, src_row, dst, dst_row, sem):
    return pltpu.make_async_copy(src.at[pl.ds(src_row, 1)], dst.at[pl.ds(dst_row, 1)], sem)


def _dispatch_kernel(lo_ref, hi_ref, nu_ref, pos_hbm, h_ref, xb_out, pos_smem, zero_ref, sem_idx,
                     sem_rows, sem_fill):
    tq = h_ref.shape[0]
    sub = tq // LANES
    i = pl.program_id(0)
    n_blocks = xb_out.shape[0] // MOE_BLOCK

    def zero_rows(start, size, wait):
        cp = pltpu.make_async_copy(zero_ref.at[pl.ds(0, size)], xb_out.at[pl.ds(start, size)],
                                   sem_fill)
        cp.wait() if wait else cp.start()

    def fill_padding(e, wait):
        lo, hi = lo_ref[e], hi_ref[e]
        lo8 = jnp.minimum((lo + (SUBLANES - 1)) & -SUBLANES, hi)
        for u in range(SUBLANES - 1):
            @pl.when(lo + u < lo8)
            def _(u=u):
                zero_rows(lo + u, 1, wait)
        cur = lo8
        size = MOE_BLOCK // 2
        while size >= SUBLANES:
            take = ((hi - lo8) & size) != 0

            @pl.when(take)
            def _(cur=cur, size=size):
                zero_rows(pl.multiple_of(cur, SUBLANES), size, wait)
            cur = cur + jnp.where(take, size, 0)
            size //= 2

    def fill_tail(blk, wait):
        zero_rows(pl.multiple_of(blk * MOE_BLOCK, MOE_BLOCK), MOE_BLOCK, wait)

    @pl.when(i == 0)
    def _():
        zero_ref[...] = jnp.zeros(zero_ref.shape, F32)
        for wait in (False, True):
            def per_expert(e, carry, wait=wait):
                fill_padding(e, wait)
                return carry

            def per_block(blk, carry, wait=wait):
                fill_tail(blk, wait)
                return carry

            lax.fori_loop(0, N_EXPERTS, per_expert, 0)
            lax.fori_loop(nu_ref[0], n_blocks, per_block, 0)

    cp = pltpu.make_async_copy(pos_hbm.at[i], pos_smem, sem_idx)
    cp.start()
    cp.wait()
    for rr in range(sub):
        def body(cu, carry, rr=rr):
            for u in range(DMA_UNROLL):
                cc = cu * DMA_UNROLL + u
                r = rr * LANES + cc
                _row_copy(h_ref, r, xb_out, pos_smem[rr, cc], sem_rows).start()
                _row_copy(h_ref, r, xb_out, pos_smem[sub + rr, cc], sem_rows).start()
            return carry
        lax.fori_loop(0, LANES // DMA_UNROLL, body, 0)
    pltpu.make_async_copy(h_ref, xb_out.at[pl.ds(0, tq)], sem_rows).wait()
    pltpu.make_async_copy(h_ref, xb_out.at[pl.ds(0, tq)], sem_rows).wait()


def moe_dispatch(h, pos_tiles, pad_lo, pad_hi, n_used, n_pad):
    t, d = h.shape
    tq = ROUTE_TILE
    grid_spec = pltpu.PrefetchScalarGridSpec(
        num_scalar_prefetch=3,
        grid=(t // tq,),
        in_specs=[pl.BlockSpec(memory_space=pl.ANY),
                  pl.BlockSpec((tq, d), lambda i, lo, hi, nu: (i, 0))],
        out_specs=pl.BlockSpec(memory_space=pl.ANY),
        scratch_shapes=[pltpu.SMEM((2 * tq // LANES, LANES), I32),
                        pltpu.VMEM((MOE_BLOCK, d), F32),
                        pltpu.SemaphoreType.DMA, pltpu.SemaphoreType.DMA,
                        pltpu.SemaphoreType.DMA],
    )
    return pl.pallas_call(
        _dispatch_kernel,
        grid_spec=grid_spec,
        out_shape=jax.ShapeDtypeStruct((n_pad, d), F32),
        compiler_params=_cparams("arbitrary"),
        name="moe_dispatch",
    )(pad_lo, pad_hi, n_used, pos_tiles, h)


def _expert_kernel(be_ref, nu_ref, x_ref, nw_ref, w1_ref, w3_ref, w2_ref, o_ref,
                   w1b_ref, w3b_ref, w2b_ref):
    b = pl.program_id(0)
    used = b < nu_ref[0]
    new_expert = (b == 0) | (be_ref[b] != be_ref[jnp.maximum(b - 1, 0)])

    @pl.when(used & new_expert)
    def _():
        w1b_ref[...] = w1_ref[0].astype(BF16)
        w3b_ref[...] = w3_ref[0].astype(BF16)
        w2b_ref[...] = w2_ref[0].astype(BF16)

    @pl.when(used)
    def _():
        xb = _rms(x_ref[...], nw_ref[...]).astype(BF16)
        h1 = jnp.dot(xb, w1b_ref[...], preferred_element_type=F32)
        h3 = jnp.dot(xb, w3b_ref[...], preferred_element_type=F32)
        hid = (_silu(h1) * h3).astype(BF16)
        o_ref[...] = jnp.dot(hid, w2b_ref[...], preferred_element_type=F32)

    @pl.when(b >= nu_ref[0])
    def _():
        o_ref[...] = jnp.zeros(o_ref.shape, F32)


def moe_experts(xb, nw, w1, w3, w2, blk_expert, n_used):
    n_pad, d = xb.shape
    de = w1.shape[2]
    nb = n_pad // MOE_BLOCK
    grid_spec = pltpu.PrefetchScalarGridSpec(
        num_scalar_prefetch=2,
        grid=(nb,),
        in_specs=[pl.BlockSpec((MOE_BLOCK, d), lambda b, be, nu: (jnp.minimum(b, nu[0] - 1), 0)),
                  pl.BlockSpec((1, d), lambda b, be, nu: (0, 0)),
                  pl.BlockSpec((1, d, de), lambda b, be, nu: (be[b], 0, 0)),
                  pl.BlockSpec((1, d, de), lambda b, be, nu: (be[b], 0, 0)),
                  pl.BlockSpec((1, de, d), lambda b, be, nu: (be[b], 0, 0))],
        out_specs=pl.BlockSpec((MOE_BLOCK, d), lambda b, be, nu: (b, 0)),
        scratch_shapes=[pltpu.VMEM((d, de), BF16), pltpu.VMEM((d, de), BF16),
                        pltpu.VMEM((de, d), BF16)],
    )
    return pl.pallas_call(
        _expert_kernel,
        grid_spec=grid_spec,
        out_shape=jax.ShapeDtypeStruct((n_pad, d), F32),
        compiler_params=_cparams("arbitrary"),
        name="moe_experts",
    )(blk_expert, n_used, xb, nw.reshape(1, d), w1, w3, w2)


def _combine_ple_kernel(pos_hbm, yb_hbm, h_ref, gate_ref, p_ref, nw_ref, wg_ref, wp_ref, fw_ref,
                        o_ref, pos_smem, buf_ref, sem_idx, sem_rows, *, final):
    tq = h_ref.shape[0]
    sub = tq // LANES
    i = pl.program_id(0)
    slot = i % 2

    def gather(tile, slot):
        cp = pltpu.make_async_copy(pos_hbm.at[tile], pos_smem.at[slot], sem_idx)
        cp.start()
        cp.wait()
        for rr in range(sub):
            def body(cu, carry, rr=rr):
                for u in range(DMA_UNROLL):
                    cc = cu * DMA_UNROLL + u
                    r = rr * LANES + cc
                    for k in range(2):
                        _row_copy(yb_hbm, pos_smem[slot, k * sub + rr, cc], buf_ref.at[slot, k], r,
                                  sem_rows.at[slot]).start()
                return carry
            lax.fori_loop(0, LANES // DMA_UNROLL, body, 0)

    @pl.when(i == 0)
    def _():
        gather(0, 0)

    @pl.when(i + 1 < pl.num_programs(0))
    def _():
        gather(i + 1, 1 - slot)

    for k in range(2):
        pltpu.make_async_copy(yb_hbm.at[pl.ds(0, tq)], buf_ref.at[slot, k], sem_rows.at[slot]).wait()

    gate = gate_ref[...]
    h2 = h_ref[...] + gate[:, 0:1] * buf_ref[slot, 0] + gate[:, 1:2] * buf_ref[slot, 1]
    hn = _rms(h2, nw_ref[...]).astype(BF16)
    pg = jax.nn.sigmoid(jnp.dot(hn, wg_ref[...], preferred_element_type=F32))
    pp = jnp.dot(p_ref[...].astype(BF16), wp_ref[...], preferred_element_type=F32)
    out = h2 + pg * pp
    if final:
        out = _rms(out, fw_ref[...])
    o_ref[...] = out


def moe_combine_ple(h, yb, pos_tiles, gate_cols, p, nw, wg, wp, fw, *, final):
    t, d = h.shape
    tq = ROUTE_TILE
    pd = p.shape[1]
    return pl.pallas_call(
        functools.partial(_combine_ple_kernel, final=final),
        grid=(t // tq,),
        in_specs=[pl.BlockSpec(memory_space=pl.ANY),
                  pl.BlockSpec(memory_space=pl.ANY),
                  pl.BlockSpec((tq, d), lambda i: (i, 0)),
                  pl.BlockSpec((tq, 2), lambda i: (i, 0)),
                  pl.BlockSpec((tq, pd), lambda i: (i, 0)),
                  pl.BlockSpec((1, d), lambda i: (0, 0)),
                  pl.BlockSpec((d, d), lambda i: (0, 0)),
                  pl.BlockSpec((pd, d), lambda i: (0, 0)),
                  pl.BlockSpec((1, d), lambda i: (0, 0))],
        out_specs=pl.BlockSpec((tq, d), lambda i: (i, 0)),
        out_shape=jax.ShapeDtypeStruct((t, d), F32),
        scratch_shapes=[pltpu.SMEM((2, 2 * tq // LANES, LANES), I32),
                        pltpu.VMEM((2, 2, tq, d), F32),
                        pltpu.SemaphoreType.DMA, pltpu.SemaphoreType.DMA((2,))],
        compiler_params=_cparams("arbitrary"),
        name="moe_combine_ple",
    )(pos_tiles, yb, h, gate_cols, p, nw.reshape(1, d), wg, wp, fw.reshape(1, d))


def moe_ple_layer(h, p, norm_moe, w_group, b_group, w_expert, b_expert, w1, w3, w2,
                  norm_ple, wg, wp, fw, *, layer, final):
    t, d = h.shape
    eid_tiles, rank_tiles, gate, counts = moe_router(h, norm_moe, w_group, b_group, w_expert,
                                                     b_expert)
    padded = (counts + MOE_BLOCK - 1) // MOE_BLOCK * MOE_BLOCK
    pend = jnp.cumsum(padded)
    pstart = pend - padded
    na = 2 * t
    n_pad = (na + MOE_BLOCK - 1) // MOE_BLOCK * MOE_BLOCK + N_EXPERTS * MOE_BLOCK
    nb = n_pad // MOE_BLOCK
    blk_start = jnp.arange(nb, dtype=I32) * MOE_BLOCK
    blk_expert = jnp.minimum(jnp.sum(blk_start[:, None] >= pend[None, :], axis=-1),
                             N_EXPERTS - 1).astype(I32)
    n_used = (pend[-1:] // MOE_BLOCK).astype(I32)
    pos_tiles = moe_slots(pstart.astype(I32), eid_tiles, rank_tiles)
    xb = moe_dispatch(h, pos_tiles, (pstart + counts).astype(I32), pend.astype(I32), n_used, n_pad)
    yb = moe_experts(xb, norm_moe, w1, w3, w2, blk_expert + layer * N_EXPERTS, n_used)
    return moe_combine_ple(h, yb, pos_tiles, gate.T, p, norm_ple, wg.astype(BF16),
                           wp.astype(BF16), fw, final=final)


def _pad_cols(w, n):
    return jnp.pad(w, ((0, 0), (0, n - w.shape[1])))


def kernel(x, p, norm_mix, norm_moe, norm_ple, final_norm, m_in_w, m_conv_w, m_conv_b, m_dt_bias, m_A_log, m_D, m_norm_w, m_out_w, g_in_w, g_conv_w, g_dt_bias, g_A_log, g_norm_w, g_out_w, moe_w_group, moe_b_group, moe_w_expert, moe_b_expert, moe_w1, moe_w3, moe_w2, ple_w_proj, ple_w_gate):
    batch, seq, d = x.shape
    t = batch * seq
    depth = p.shape[0]
    pd = p.shape[-1]
    h = x.reshape(t, d)
    p2 = p.reshape(depth, t, pd)
    w1_all = moe_w1.reshape((-1,) + moe_w1.shape[2:])
    w3_all = moe_w3.reshape((-1,) + moe_w3.shape[2:])
    w2_all = moe_w2.reshape((-1,) + moe_w2.shape[2:])
    for i in range(depth):
        j = i // 2
        if i % 2 == 0:
            inner = m_out_w.shape[1]
            heads = m_dt_bias.shape[1]
            conv_dim = m_conv_w.shape[2]
            w_in = _pad_cols(m_in_w[j], inner + conv_dim + LANES).astype(BF16)
            proj = norm_matmul(h, norm_mix[i], w_in)
            y = ssd_mixer_core(proj, m_conv_w[j], m_conv_b[j], m_dt_bias[j], m_A_log[j], m_D[j],
                               m_norm_w[j], batch=batch, seq=seq, inner=inner, heads=heads)
            h = matmul_residual(y, m_out_w[j].astype(BF16), h)
        else:
            v_dim = g_out_w.shape[1]
            v_heads = g_dt_bias.shape[1]
            conv_dim = g_conv_w.shape[2]
            qk_dim = (conv_dim - v_dim) // 2
            w_in = _pad_cols(g_in_w[j], conv_dim + v_dim + LANES).astype(BF16)
            proj = norm_matmul(h, norm_mix[i], w_in)
            y = gdn_mixer_core(proj, g_conv_w[j], g_dt_bias[j], g_A_log[j], g_norm_w[j],
                               batch=batch, seq=seq, qk_dim=qk_dim, v_dim=v_dim, v_heads=v_heads)
            h = matmul_residual(y, g_out_w[j].astype(BF16), h)
        h = moe_ple_layer(h, p2[i], norm_moe[i], moe_w_group[i], moe_b_group[i], moe_w_expert[i],
                          moe_b_expert[i], w1_all, w3_all, w2_all, norm_ple[i],
                          ple_w_gate[i], ple_w_proj[i], final_norm, layer=i,
                          final=(i == depth - 1))
    return h.reshape(batch, seq, d)
```

```python
import functools
import math

import jax
import jax.numpy as jnp
from jax import lax
from jax.experimental import pallas as pl
from jax.experimental.pallas import tpu as pltpu

F32 = jnp.float32
BF16 = jnp.bfloat16
I32 = jnp.int32
EPS = 1e-6
HIGHEST = lax.Precision.HIGHEST

LANES = 128
SUBLANES = 8
VMEM_LIMIT = 56 * 1024 * 1024

CONV_K = 4
SSD_CHUNK = 128
SSD_HEADDIM = 64
SSD_STATE = 128
SSD_GROUPS = 4
GDN_CHUNK = 64
GDN_HEAD = 128
GDN_STACK = 256
GDN_SUBBLOCK = 16
GDN_BATCH_PER_STEP = 1
N_EGROUPS = 8
EXPERTS_PER_GROUP = 8
N_EXPERTS = N_EGROUPS * EXPERTS_PER_GROUP
MOE_BLOCK = 512
ROUTE_TILE = 512
DMA_UNROLL = 8


def _cparams(*sem):
    return pltpu.CompilerParams(dimension_semantics=sem, vmem_limit_bytes=VMEM_LIMIT)


def _silu(x):
    return x * jax.nn.sigmoid(x)


def _softplus(x):
    return jnp.maximum(x, 0.0) + jnp.log1p(jnp.exp(-jnp.abs(x)))


def _rms(x, w):
    return x * lax.rsqrt(jnp.mean(x * x, axis=-1, keepdims=True) + EPS) * w


def _dot(a, b):
    return jnp.dot(a.astype(BF16), b.astype(BF16), preferred_element_type=F32)


def _split(x):
    hi = x.astype(BF16)
    return hi, (x - hi.astype(F32)).astype(BF16)


def _dot_f32(a, b):
    return jnp.dot(a, b, preferred_element_type=F32, precision=HIGHEST)


def _dot_tn(a, b):
    return lax.dot_general(a.astype(BF16), b.astype(BF16), (((0,), (0,)), ((), ())),
                           preferred_element_type=F32)


def _dot_nt(a, b):
    return lax.dot_general(a.astype(BF16), b.astype(BF16), (((1,), (1,)), ((), ())),
                           preferred_element_type=F32)


def _tri(n, strict=False):
    r = lax.broadcasted_iota(I32, (n, n), 0)
    c = lax.broadcasted_iota(I32, (n, n), 1)
    return (r > c) if strict else (r >= c)


def _norm_matmul_kernel(x_ref, nw_ref, w_ref, o_ref, *, n_chunk):
    xb = _rms(x_ref[...], nw_ref[...]).astype(BF16)
    n = o_ref.shape[1]
    for c0 in range(0, n, n_chunk):
        c1 = min(c0 + n_chunk, n)
        o_ref[:, c0:c1] = jnp.dot(xb, w_ref[:, c0:c1], preferred_element_type=F32)


def norm_matmul(x, nw, w, *, tm=256, n_chunk=512):
    t, d = x.shape
    n = w.shape[1]
    return pl.pallas_call(
        functools.partial(_norm_matmul_kernel, n_chunk=n_chunk),
        grid=(t // tm,),
        in_specs=[pl.BlockSpec((tm, d), lambda i: (i, 0)),
                  pl.BlockSpec((1, d), lambda i: (0, 0)),
                  pl.BlockSpec((d, n), lambda i: (0, 0))],
        out_specs=pl.BlockSpec((tm, n), lambda i: (i, 0)),
        out_shape=jax.ShapeDtypeStruct((t, n), F32),
        compiler_params=_cparams("arbitrary"),
        name="norm_matmul",
    )(x, nw.reshape(1, d), w)


def _matmul_residual_kernel(y_ref, w_ref, r_ref, o_ref):
    o_ref[...] = r_ref[...] + jnp.dot(y_ref[...].astype(BF16), w_ref[...],
                                      preferred_element_type=F32)


def matmul_residual(y, w, res, *, tm=512):
    t, k = y.shape
    d = w.shape[1]
    return pl.pallas_call(
        _matmul_residual_kernel,
        grid=(t // tm,),
        in_specs=[pl.BlockSpec((tm, k), lambda i: (i, 0)),
                  pl.BlockSpec((k, d), lambda i: (0, 0)),
                  pl.BlockSpec((tm, d), lambda i: (i, 0))],
        out_specs=pl.BlockSpec((tm, d), lambda i: (i, 0)),
        out_shape=jax.ShapeDtypeStruct((t, d), F32),
        compiler_params=_cparams("arbitrary"),
        name="matmul_residual",
    )(y, w, res)


def _conv_silu(x_ref, stage_ref, w_ref, bias, first):
    q = x_ref.shape[0]

    @pl.when(first)
    def _():
        stage_ref[0:SUBLANES, :] = jnp.zeros((SUBLANES, stage_ref.shape[1]), F32)

    stage_ref[SUBLANES:SUBLANES + q, :] = x_ref[...]
    acc = stage_ref[SUBLANES:SUBLANES + q, :] * w_ref[CONV_K - 1:CONV_K, :]
    for j in range(CONV_K - 1):
        off = SUBLANES - (CONV_K - 1) + j
        acc = acc + stage_ref[off:off + q, :] * w_ref[j:j + 1, :]
    if bias is not None:
        acc = acc + bias
    stage_ref[0:SUBLANES, :] = stage_ref[q:q + SUBLANES, :]
    return _silu(acc)


def _ssd_kernel(z_ref, x_ref, b_ref, c_ref, dt_ref,
                cwx_ref, cwb_ref, cwc_ref, cbx_ref, cbb_ref, cbc_ref,
                dtb_ref, alog_ref, dfull_ref, nw_ref,
                o_ref,
                sx_ref, sb_ref, sc_ref, y_ref, xw_ref, st_ref):
    q = x_ref.shape[0]
    hpg = x_ref.shape[1] // (SSD_GROUPS * SSD_HEADDIM)
    first = pl.program_id(1) == 0

    @pl.when(first)
    def _():
        st_ref[...] = jnp.zeros(st_ref.shape, F32)

    xs = _conv_silu(x_ref, sx_ref, cwx_ref, cbx_ref[...], first)
    bm = _conv_silu(b_ref, sb_ref, cwb_ref, cbb_ref[...], first)
    cm = _conv_silu(c_ref, sc_ref, cwc_ref, cbc_ref[...], first)

    dt = _softplus(dt_ref[...] + dtb_ref[...])
    da = dt * (-jnp.exp(alog_ref[...]))
    causal = _tri(q)
    a = _dot_f32(causal.astype(F32), da)
    a_t = a.T
    ea_t = jnp.exp(a_t)
    a_last = a[q - 1:q, :]
    to_end_t = (jnp.exp(a_last - a) * dt).T
    ea_last = jnp.exp(a_last)
    dt_t = dt.T
    eye = (lax.broadcasted_iota(I32, (q, q), 0) == lax.broadcasted_iota(I32, (q, q), 1))
    lane_lo = lax.broadcasted_iota(I32, (q, 2 * SSD_HEADDIM), 1) < SSD_HEADDIM

    for g in range(SSD_GROUPS):
        bg = bm[:, g * SSD_STATE:(g + 1) * SSD_STATE]
        cg = cm[:, g * SSD_STATE:(g + 1) * SSD_STATE]
        cb = _dot_nt(cg, bg)
        gw = hpg * SSD_HEADDIM
        cs = _dot(cg, st_ref[g])
        for pr in range(hpg // 2):
            j0 = g * hpg + 2 * pr
            lo, hi = j0 * SSD_HEADDIM, (j0 + 2) * SSD_HEADDIM
            xp = xs[:, lo:hi]
            cp = cs[:, 2 * pr * SSD_HEADDIM:(2 * pr + 2) * SSD_HEADDIM]
            x_sel = [jnp.where(lane_lo, xp, 0.0), jnp.where(lane_lo, 0.0, xp)]
            c_sel = [jnp.where(lane_lo, cp, 0.0), jnp.where(lane_lo, 0.0, cp)]
            lhs, rhs, te = [], [], []
            for k in range(2):
                j = j0 + k
                diff = a[:, j:j + 1] - a_t[j:j + 1, :]
                seg = jnp.exp(jnp.where(causal, diff, -jnp.inf))
                lhs += [cb * seg * dt_t[j:j + 1, :], jnp.where(eye, ea_t[j:j + 1, :], 0.0)]
                rhs += [x_sel[k], c_sel[k]]
                te.append(jnp.where(eye, to_end_t[j:j + 1, :], 0.0))
            y_ref[:, lo:hi] = _dot(jnp.concatenate(lhs, axis=1), jnp.concatenate(rhs, axis=0))
            xw_ref[:, lo:hi] = _dot(jnp.concatenate(te, axis=1), jnp.concatenate(x_sel, axis=0))
        upd = _dot_tn(bg, xw_ref[:, g * gw:(g + 1) * gw])
        for jj in range(hpg):
            j = g * hpg + jj
            sl = slice(jj * SSD_HEADDIM, (jj + 1) * SSD_HEADDIM)
            st_ref[g, :, sl] = st_ref[g, :, sl] * ea_last[:, j:j + 1] + upd[:, sl]

    y = y_ref[...] + dfull_ref[...] * xs
    o_ref[...] = _rms(y * _silu(z_ref[...]), nw_ref[...])


def ssd_mixer_core(proj, conv_w, conv_b, dt_bias, a_log, d_skip, norm_w, *, batch, seq,
                   inner, heads):
    t = proj.shape[0]
    q = SSD_CHUNK
    gn = SSD_GROUPS * SSD_STATE
    nc = seq // q
    cwx, cwb, cwc = conv_w[:, :inner], conv_w[:, inner:inner + gn], conv_w[:, inner + gn:]
    cbx = conv_b[:inner].reshape(1, inner)
    cbb = conv_b[inner:inner + gn].reshape(1, gn)
    cbc = conv_b[inner + gn:].reshape(1, gn)
    pad = LANES - heads
    dtb = jnp.pad(dt_bias, (0, pad)).reshape(1, LANES)
    alog = jnp.pad(a_log, (0, pad)).reshape(1, LANES)
    dfull = jnp.repeat(d_skip, SSD_HEADDIM).reshape(1, inner)
    nw = norm_w.reshape(1, inner)

    def row(i, c):
        return i * nc + c

    const = lambda shape: pl.BlockSpec(shape, lambda i, c: (0, 0))
    x_blk = inner // inner
    b_blk = (2 * inner) // gn
    c_blk = b_blk + 1
    dt_blk = (2 * inner + 2 * gn) // LANES
    return pl.pallas_call(
        _ssd_kernel,
        grid=(batch, nc),
        in_specs=[pl.BlockSpec((q, inner), lambda i, c: (row(i, c), 0)),
                  pl.BlockSpec((q, inner), lambda i, c: (row(i, c), x_blk)),
                  pl.BlockSpec((q, gn), lambda i, c: (row(i, c), b_blk)),
                  pl.BlockSpec((q, gn), lambda i, c: (row(i, c), c_blk)),
                  pl.BlockSpec((q, LANES), lambda i, c: (row(i, c), dt_blk)),
                  const((CONV_K, inner)), const((CONV_K, gn)), const((CONV_K, gn)),
                  const((1, inner)), const((1, gn)), const((1, gn)),
                  const((1, LANES)), const((1, LANES)), const((1, inner)), const((1, inner))],
        out_specs=pl.BlockSpec((q, inner), lambda i, c: (row(i, c), 0)),
        out_shape=jax.ShapeDtypeStruct((t, inner), F32),
        scratch_shapes=[pltpu.VMEM((q + SUBLANES, inner), F32),
                        pltpu.VMEM((q + SUBLANES, gn), F32),
                        pltpu.VMEM((q + SUBLANES, gn), F32),
                        pltpu.VMEM((q, inner), F32),
                        pltpu.VMEM((q, inner), F32),
                        pltpu.VMEM((SSD_GROUPS, SSD_STATE, inner // SSD_GROUPS), F32)],
        compiler_params=_cparams("arbitrary", "arbitrary"),
        name="ssd_scan",
    )(proj, proj, proj, proj, proj, cwx, cwb, cwc, cbx, cbb, cbc, dtb, alog, dfull, nw)


def _l2n(x):
    return x * lax.rsqrt(jnp.sum(x * x, axis=-1, keepdims=True) + EPS)


def _pair_cols(x, i0, i1, half):
    c = x.shape[0]
    lane = lax.broadcasted_iota(I32, (c, 2 * half), 1)
    return jnp.where(lane < half, jnp.broadcast_to(x[:, i0:i0 + 1], (c, 2 * half)),
                     jnp.broadcast_to(x[:, i1:i1 + 1], (c, 2 * half)))


def _gdn_kernel(q_ref, k_ref, v_ref, z_ref, ba_ref,
                cwq_ref, cwk_ref, cwv_ref, dtb_ref, alog_ref, nw_ref,
                o_ref,
                sq_ref, sk_ref, sv_ref, s_ref, *, v_heads):
    nb, c = q_ref.shape[0], q_ref.shape[1]
    qk_heads = q_ref.shape[2] // GDN_HEAD
    rep = v_heads // qk_heads
    gsz = GDN_STACK // c
    n_grp = v_heads // gsz
    first = pl.program_id(1) == 0

    @pl.when(first)
    def _():
        s_ref[...] = jnp.zeros(s_ref.shape, F32)

    t_idx = lax.broadcasted_iota(I32, (c, GDN_STACK), 0)
    s_idx = lax.broadcasted_iota(I32, (c, GDN_STACK), 1) & (c - 1)
    causal = t_idx >= s_idx
    strict = t_idx > s_idx
    eye = t_idx == s_idx
    eye_f = eye.astype(F32)
    sb_shift = GDN_SUBBLOCK.bit_length() - 1
    same_blk = (t_idx >> sb_shift) == (s_idx >> sb_shift)
    n_col = GDN_STACK // LANES
    lane_lo = lax.broadcasted_iota(I32, (c, LANES), 1) < c

    def bdiag(parts):
        zero = jnp.zeros((), BF16)
        cols = lambda p: [p[:, k * LANES:(k + 1) * LANES] for k in range(n_col)]
        return tuple([jnp.concatenate([jnp.where(lane_lo, y, zero), jnp.where(lane_lo, zero, y)],
                                      axis=0) for y in cols(p)] for p in parts)

    def mm(lhs, rhs_bd):
        n = len(lhs) * c
        parts = [_split(x) for x in lhs]
        l_hi = jnp.concatenate([p[0] for p in parts], axis=0)
        l_both = jnp.concatenate([l_hi] + [p[1] for p in parts], axis=0)
        outs = []
        for k in range(n_col):
            sl = slice(k * LANES, (k + 1) * LANES)
            r = jnp.dot(l_both[:, sl], rhs_bd[0][k], preferred_element_type=F32)
            outs.append(r[:n] + r[n:] + jnp.dot(l_hi[:, sl], rhs_bd[1][k],
                                                preferred_element_type=F32))
        out = jnp.concatenate(outs, axis=1)
        return [out[i * c:(i + 1) * c] for i in range(len(lhs))]

    def neumann(p0s, n_sq):
        ps = list(p0s)
        ds = [eye_f + p for p in ps]
        for i in range(n_sq):
            for g in range(len(ps)):
                bd = bdiag(_split(ps[g]))
                if i == 0:
                    ps[g], = mm([ps[g]], bd)
                else:
                    ps[g], x = mm([ps[g], ds[g]], bd)
                    ds[g] = ds[g] + x
        if n_sq > 0:
            for g in range(len(ps)):
                x, = mm([ds[g]], bdiag(_split(ps[g])))
                ds[g] = ds[g] + x
        return ds

    def prepare(b):
        qc = _conv_silu(q_ref.at[b], sq_ref.at[b], cwq_ref, None, first)
        kc = _conv_silu(k_ref.at[b], sk_ref.at[b], cwk_ref, None, first)
        vc = _conv_silu(v_ref.at[b], sv_ref.at[b], cwv_ref, None, first)
        ba = ba_ref[b]
        beta = jax.nn.sigmoid(ba)
        gl = -jnp.exp(alog_ref[...]) * _softplus(ba + dtb_ref[...])
        gcs = _dot_f32(_tri(c).astype(F32), gl)
        g_last = gcs[c - 1:c, :]
        st = dict(vc=vc, eg_last=jnp.exp(g_last), qns=[], kns=[], kts=[], a_kks=[], a_qks=[],
                  gcols=[], brows=[], grows=[])
        kks, qks = [], []
        for hq in range(qk_heads):
            sl = slice(hq * GDN_HEAD, (hq + 1) * GDN_HEAD)
            qn = _l2n(qc[:, sl]) * (GDN_HEAD ** -0.5)
            kn = _l2n(kc[:, sl])
            k2 = jnp.concatenate([kn] * rep, axis=0)
            st["qns"].append(qn)
            st["kns"].append(kn)
            st["kts"].append(kn.T.astype(BF16))
            kks.append(_dot_nt(kn, k2))
            qks.append(_dot_nt(qn, k2))
        for grp in range(n_grp):
            h0 = grp * gsz
            hq0 = h0 // rep
            nq = gsz // rep
            kk = jnp.concatenate(kks[hq0:hq0 + nq], axis=1)
            qk = jnp.concatenate(qks[hq0:hq0 + nq], axis=1)
            gcol = jnp.concatenate(
                [_pair_cols(gcs, v_heads + h0 + 2 * i, v_heads + h0 + 2 * i + 1, c)
                 for i in range(gsz // 2)], axis=1)
            bcol = jnp.concatenate([_pair_cols(beta, h0 + 2 * i, h0 + 2 * i + 1, c)
                                    for i in range(gsz // 2)], axis=1)
            grow = jnp.sum(jnp.where(eye, gcol, 0.0), axis=0, keepdims=True)
            decay = jnp.exp(jnp.where(causal, gcol - grow, -jnp.inf))
            st["a_kks"].append(jnp.where(strict, bcol * kk * decay, 0.0))
            st["a_qks"].append(jnp.where(causal, qk * decay, 0.0))
            st["gcols"].append(gcol)
            st["grows"].append(grow)
            st["brows"].append(jnp.sum(jnp.where(eye, bcol, 0.0), axis=0, keepdims=True))
        return st

    def pair_blocks(b0, b1):
        z = jnp.zeros((c, GDN_HEAD), BF16)
        return jnp.concatenate([jnp.concatenate([b0.astype(BF16), z], axis=1),
                                jnp.concatenate([z, b1.astype(BF16)], axis=1)], axis=0)

    def heads(b, st, grp, tinv):
        h0 = grp * gsz
        gcol, grow, brow = st["gcols"][grp], st["grows"][grp], st["brows"][grp]
        hd = lambda x, i: x[:, i * GDN_HEAD:(i + 1) * GDN_HEAD]
        col = lambda x, k: x[:, k * LANES:(k + 1) * LANES]
        tb = tinv * brow
        tbe = tinv * (brow * jnp.exp(grow))
        us, ws_ = [], []
        for k in range(n_col):
            ha = h0 + 2 * k
            us.append(_dot(col(tb, k), pair_blocks(hd(st["vc"], ha), hd(st["vc"], ha + 1))))
            ws_.append(_dot(col(tbe, k), pair_blocks(st["kns"][ha // rep],
                                                      st["kns"][(ha + 1) // rep])))
        u = jnp.concatenate(us, axis=1)
        w = jnp.concatenate(ws_, axis=1)
        yield
        v_news, q_states = [], []
        for i in range(gsz):
            hq = (h0 + i) // rep
            ws = _dot(jnp.concatenate([hd(w, i), st["qns"][hq]], axis=0), s_ref[b, h0 + i])
            v_news.append(hd(u, i) - ws[:c])
            q_states.append(ws[c:])
        yield
        eg_diag =jnp.where(eye, jnp.exp(gcol), 0.0)
        eg_end_diag = jnp.where(eye, jnp.exp(gcol[c - 1:c, :] - gcol), 0.0)
        os_, vss = [], []
        for k in range(n_col):
            pv = pair_blocks(v_news[2 * k], v_news[2 * k + 1])
            pq = pair_blocks(q_states[2 * k], q_states[2 * k + 1])
            os_.append(_dot(jnp.concatenate([col(st["a_qks"][grp], k), col(eg_diag, k)], axis=1),
                            jnp.concatenate([pv, pq], axis=0)))
            vss.append(_dot(col(eg_end_diag, k), pv))
        o = jnp.concatenate(os_, axis=1)
        v_scaled = jnp.concatenate(vss, axis=1)
        yield
        for i in range(gsz):
            h = h0 + i
            gi = v_heads + h
            hs = slice(h * GDN_HEAD, (h + 1) * GDN_HEAD)
            s_ref[b, h] = (s_ref[b, h] * st["eg_last"][:, gi:gi + 1]
                           + _dot(st["kts"][h // rep], hd(v_scaled, i)))
            o_ref[b, :, hs] = _rms(hd(o, i), nw_ref[...]) * _silu(z_ref[b, :, hs])

    sts = [prepare(b) for b in range(nb)]

    a_all = [a for st in sts for a in st["a_kks"]]
    dms = neumann([jnp.where(same_blk, -a, 0.0) for a in a_all], sb_shift - 1)
    ms = [-mm([d], bdiag(_split(jnp.where(same_blk, 0.0, a))))[0] for d, a in zip(dms, a_all)]
    ws = neumann(ms, (c // GDN_SUBBLOCK).bit_length() - 2)
    tinvs = [mm([w], bdiag(_split(d)))[0] for w, d in zip(ws, dms)]

    gens = [heads(b, sts[b], grp, tinvs[b * n_grp + grp]) for grp in range(n_grp)
            for b in range(nb)]
    for _ in range(4):
        for gen in gens:
            next(gen, None)


def gdn_mixer_core(proj, conv_w, dt_bias, a_log, norm_w, *, batch, seq, qk_dim, v_dim, v_heads):
    t, n = proj.shape
    c = GDN_CHUNK
    nb = GDN_BATCH_PER_STEP
    cwq, cwk, cwv = conv_w[:, :qk_dim], conv_w[:, qk_dim:2 * qk_dim], conv_w[:, 2 * qk_dim:]
    dtb = jnp.pad(dt_bias, (v_heads, LANES - 2 * v_heads)).reshape(1, LANES)
    alog = jnp.pad(a_log, (v_heads, LANES - 2 * v_heads)).reshape(1, LANES)
    nw = norm_w.reshape(1, GDN_HEAD)
    proj3 = proj.reshape(batch, seq, n)

    const = lambda shape: pl.BlockSpec(shape, lambda i, cc: (0, 0))
    blk = lambda width, col: pl.BlockSpec((nb, c, width), lambda i, cc: (i, cc, col))
    v_blk = (2 * qk_dim) // v_dim
    z_blk = v_blk + 1
    ba_blk = (2 * qk_dim + 2 * v_dim) // LANES
    out = pl.pallas_call(
        functools.partial(_gdn_kernel, v_heads=v_heads),
        grid=(batch // nb, seq // c),
        in_specs=[blk(qk_dim, 0), blk(qk_dim, 1), blk(v_dim, v_blk), blk(v_dim, z_blk),
                  blk(LANES, ba_blk),
                  const((CONV_K, qk_dim)), const((CONV_K, qk_dim)), const((CONV_K, v_dim)),
                  const((1, LANES)), const((1, LANES)), const((1, GDN_HEAD))],
        out_specs=blk(v_dim, 0),
        out_shape=jax.ShapeDtypeStruct((batch, seq, v_dim), F32),
        scratch_shapes=[pltpu.VMEM((nb, c + SUBLANES, qk_dim), F32),
                        pltpu.VMEM((nb, c + SUBLANES, qk_dim), F32),
                        pltpu.VMEM((nb, c + SUBLANES, v_dim), F32),
                        pltpu.VMEM((nb, v_heads, GDN_HEAD, GDN_HEAD), F32)],
        compiler_params=_cparams("arbitrary", "arbitrary"),
        name="gdn_scan",
    )(proj3, proj3, proj3, proj3, proj3, cwq, cwk, cwv, dtb, alog, nw)
    return out.reshape(t, v_dim)


def _first_index(mask, n):
    idx = lax.broadcasted_iota(I32, mask.shape, 0)
    return jnp.min(jnp.where(mask, idx, n), axis=0, keepdims=True)


def _router_kernel(h_ref, nw_ref, wr_ref, br_ref, upper_ref,
                   eid_ref, rank_ref, gate_ref, cnt_ref, base_ref):
    tq = h_ref.shape[0]
    i = pl.program_id(0)

    @pl.when(i == 0)
    def _():
        base_ref[...] = jnp.zeros(base_ref.shape, F32)

    xn = _rms(h_ref[...], nw_ref[...])
    logits = lax.dot_general(wr_ref[...], xn, (((1,), (1,)), ((), ())),
                             preferred_element_type=F32, precision=HIGHEST) + br_ref[:, 0:1]
    gl = logits[0:N_EGROUPS, :]
    gmax = jnp.max(gl, axis=0, keepdims=True)
    g_sel = _first_index(gl == gmax, N_EGROUPS)
    p_sel = 1.0 / jnp.sum(jnp.exp(gl - gmax), axis=0, keepdims=True)
    e_in = logits[N_EGROUPS:N_EGROUPS + EXPERTS_PER_GROUP, :]
    for g in range(1, N_EGROUPS):
        lo = N_EGROUPS + g * EXPERTS_PER_GROUP
        e_in = jnp.where(g_sel == g, logits[lo:lo + EXPERTS_PER_GROUP, :], e_in)
    m1 = jnp.max(e_in, axis=0, keepdims=True)
    i1 = _first_index(e_in == m1, EXPERTS_PER_GROUP)
    sub = lax.broadcasted_iota(I32, e_in.shape, 0)
    rest = jnp.where(sub == i1, -jnp.inf, e_in)
    m2 = jnp.max(rest, axis=0, keepdims=True)
    i2 = _first_index(rest == m2, EXPERTS_PER_GROUP)
    e2 = jnp.exp(m2 - m1)
    denom = 1.0 + e2
    gate_ref[0:1, :] = (1.0 / denom) * p_sel
    gate_ref[1:2, :] = (e2 / denom) * p_sel
    eid0 = g_sel * EXPERTS_PER_GROUP + i1
    eid1 = g_sel * EXPERTS_PER_GROUP + i2

    erow = lax.broadcasted_iota(I32, (N_EXPERTS, tq), 0)
    oh0 = erow == eid0
    oh1 = erow == eid1
    oh0f = oh0.astype(F32)
    oh1f = oh1.astype(F32)
    cum0 = jnp.dot(oh0f.astype(BF16), upper_ref[...], preferred_element_type=F32)
    cum1 = jnp.dot(oh1f.astype(BF16), upper_ref[...], preferred_element_type=F32)
    base = base_ref[:, 0:1]
    tot0 = jnp.sum(oh0f, axis=1, keepdims=True)
    tot1 = jnp.sum(oh1f, axis=1, keepdims=True)
    r0 = jnp.sum(jnp.where(oh0, base + cum0, 0.0), axis=0, keepdims=True)
    r1 = jnp.sum(jnp.where(oh1, base + tot0 + cum1, 0.0), axis=0, keepdims=True)
    sub = tq // LANES
    for k, (e, r) in enumerate(((eid0, r0.astype(I32)), (eid1, r1.astype(I32)))):
        for rr in range(sub):
            row = k * sub + rr
            eid_ref[0, row:row + 1, :] = e[:, rr * LANES:(rr + 1) * LANES]
            rank_ref[0, row:row + 1, :] = r[:, rr * LANES:(rr + 1) * LANES]
    new_base = base + tot0 + tot1
    base_ref[...] = jnp.broadcast_to(new_base, base_ref.shape)
    cnt_ref[...] = jnp.broadcast_to(new_base, cnt_ref.shape).astype(I32)


def moe_router(h, nw, w_group, b_group, w_expert, b_expert):
    t, d = h.shape
    tq = ROUTE_TILE
    nr = N_EGROUPS + N_EXPERTS
    wr = jnp.pad(jnp.concatenate([w_group, w_expert], axis=1).T, ((0, LANES - nr), (0, 0)))
    br = jnp.pad(jnp.concatenate([b_group, b_expert]), (0, LANES - nr))
    br = jnp.broadcast_to(br[:, None], (LANES, LANES))
    upper = (jnp.arange(tq)[:, None] < jnp.arange(tq)[None, :]).astype(BF16)
    rows = 2 * tq // LANES
    tile_spec = pl.BlockSpec((1, rows, LANES), lambda i: (i, 0, 0))
    tile_shape = jax.ShapeDtypeStruct((t // tq, rows, LANES), I32)
    eid, rank, gate, cnt = pl.pallas_call(
        _router_kernel,
        grid=(t // tq,),
        in_specs=[pl.BlockSpec((tq, d), lambda i: (i, 0)),
                  pl.BlockSpec((1, d), lambda i: (0, 0)),
                  pl.BlockSpec((LANES, d), lambda i: (0, 0)),
                  pl.BlockSpec((LANES, LANES), lambda i: (0, 0)),
                  pl.BlockSpec((tq, tq), lambda i: (0, 0))],
        out_specs=[tile_spec, tile_spec, pl.BlockSpec((2, tq), lambda i: (0, i)),
                   pl.BlockSpec((N_EXPERTS, LANES), lambda i: (0, 0))],
        out_shape=[tile_shape, tile_shape, jax.ShapeDtypeStruct((2, t), F32),
                   jax.ShapeDtypeStruct((N_EXPERTS, LANES), I32)],
        scratch_shapes=[pltpu.VMEM((N_EXPERTS, LANES), F32)],
        compiler_params=_cparams("arbitrary"),
        name="moe_router",
    )(h, nw.reshape(1, d), wr, br, upper)
    return eid, rank, gate, cnt[:, 0]


def _slot_kernel(pstart_ref, eid_ref, rank_ref, pos_ref):
    eid = eid_ref[...]

    def body(e, acc):
        return jnp.where(eid == e, pstart_ref[e], acc)

    pos_ref[...] = rank_ref[...] + lax.fori_loop(0, N_EXPERTS, body, jnp.zeros(eid.shape, I32))


def moe_slots(pstart, eid_tiles, rank_tiles):
    nt, rows, lanes = eid_tiles.shape
    tiles_per_step = math.gcd(nt, 16)
    blk = pl.BlockSpec((tiles_per_step, rows, lanes), lambda i, ps: (i, 0, 0))
    return pl.pallas_call(
        _slot_kernel,
        grid_spec=pltpu.PrefetchScalarGridSpec(num_scalar_prefetch=1, grid=(nt // tiles_per_step,),
                                               in_specs=[blk, blk], out_specs=blk),
        out_shape=jax.ShapeDtypeStruct(eid_tiles.shape, I32),
        compiler_params=_cparams("arbitrary"),
        name="moe_slots",
    )(pstart, eid_tiles, rank_tiles)


def _row_copy(src, src_row, dst, dst_row, sem):
    return pltpu.make_async_copy(src.at[pl.ds(src_row, 1)], dst.at[pl.ds(dst_row, 1)], sem)


def _dispatch_kernel(lo_ref, hi_ref, nu_ref, pos_hbm, h_ref, xb_out, pos_smem, zero_ref, sem_idx,
                     sem_rows, sem_fill):
    tq = h_ref.shape[0]
    sub = tq // LANES
    i = pl.program_id(0)
    n_blocks = xb_out.shape[0] // MOE_BLOCK

    def zero_rows(start, size, wait):
        cp = pltpu.make_async_copy(zero_ref.at[pl.ds(0, size)], xb_out.at[pl.ds(start, size)],
                                   sem_fill)
        cp.wait() if wait else cp.start()

    def fill_padding(e, wait):
        lo, hi = lo_ref[e], hi_ref[e]
        lo8 = jnp.minimum((lo + (SUBLANES - 1)) & -SUBLANES, hi)
        for u in range(SUBLANES - 1):
            @pl.when(lo + u < lo8)
            def _(u=u):
                zero_rows(lo + u, 1, wait)
        cur = lo8
        size = MOE_BLOCK // 2
        while size >= SUBLANES:
            take = ((hi - lo8) & size) != 0

            @pl.when(take)
            def _(cur=cur, size=size):
                zero_rows(pl.multiple_of(cur, SUBLANES), size, wait)
            cur = cur + jnp.where(take, size, 0)
            size //= 2

    def fill_tail(blk, wait):
        zero_rows(pl.multiple_of(blk * MOE_BLOCK, MOE_BLOCK), MOE_BLOCK, wait)

    @pl.when(i == 0)
    def _():
        zero_ref[...] = jnp.zeros(zero_ref.shape, F32)
        for wait in (False, True):
            def per_expert(e, carry, wait=wait):
                fill_padding(e, wait)
                return carry

            def per_block(blk, carry, wait=wait):
                fill_tail(blk, wait)
                return carry

            lax.fori_loop(0, N_EXPERTS, per_expert, 0)
            lax.fori_loop(nu_ref[0], n_blocks, per_block, 0)

    cp = pltpu.make_async_copy(pos_hbm.at[i], pos_smem, sem_idx)
    cp.start()
    cp.wait()
    for rr in range(sub):
        def body(cu, carry, rr=rr):
            for u in range(DMA_UNROLL):
                cc = cu * DMA_UNROLL + u
                r = rr * LANES + cc
                _row_copy(h_ref, r, xb_out, pos_smem[rr, cc], sem_rows).start()
                _row_copy(h_ref, r, xb_out, pos_smem[sub + rr, cc], sem_rows).start()
            return carry
        lax.fori_loop(0, LANES // DMA_UNROLL, body, 0)
    pltpu.make_async_copy(h_ref, xb_out.at[pl.ds(0, tq)], sem_rows).wait()
    pltpu.make_async_copy(h_ref, xb_out.at[pl.ds(0, tq)], sem_rows).wait()


def moe_dispatch(h, pos_tiles, pad_lo, pad_hi, n_used, n_pad):
    t, d = h.shape
    tq = ROUTE_TILE
    grid_spec = pltpu.PrefetchScalarGridSpec(
        num_scalar_prefetch=3,
        grid=(t // tq,),
        in_specs=[pl.BlockSpec(memory_space=pl.ANY),
                  pl.BlockSpec((tq, d), lambda i, lo, hi, nu: (i, 0))],
        out_specs=pl.BlockSpec(memory_space=pl.ANY),
        scratch_shapes=[pltpu.SMEM((2 * tq // LANES, LANES), I32),
                        pltpu.VMEM((MOE_BLOCK, d), F32),
                        pltpu.SemaphoreType.DMA, pltpu.SemaphoreType.DMA,
                        pltpu.SemaphoreType.DMA],
    )
    return pl.pallas_call(
        _dispatch_kernel,
        grid_spec=grid_spec,
        out_shape=jax.ShapeDtypeStruct((n_pad, d), F32),
        compiler_params=_cparams("arbitrary"),
        name="moe_dispatch",
    )(pad_lo, pad_hi, n_used, pos_tiles, h)


def _expert_kernel(be_ref, nu_ref, x_ref, nw_ref, w1_ref, w3_ref, w2_ref, o_ref,
                   w1b_ref, w3b_ref, w2b_ref):
    b = pl.program_id(0)
    used = b < nu_ref[0]
    new_expert = (b == 0) | (be_ref[b] != be_ref[jnp.maximum(b - 1, 0)])

    @pl.when(used & new_expert)
    def _():
        w1b_ref[...] = w1_ref[0].astype(BF16)
        w3b_ref[...] = w3_ref[0].astype(BF16)
        w2b_ref[...] = w2_ref[0].astype(BF16)

    @pl.when(used)
    def _():
        xb = _rms(x_ref[...], nw_ref[...]).astype(BF16)
        h1 = jnp.dot(xb, w1b_ref[...], preferred_element_type=F32)
        h3 = jnp.dot(xb, w3b_ref[...], preferred_element_type=F32)
        hid = (_silu(h1) * h3).astype(BF16)
        o_ref[...] = jnp.dot(hid, w2b_ref[...], preferred_element_type=F32)

    @pl.when(b >= nu_ref[0])
    def _():
        o_ref[...] = jnp.zeros(o_ref.shape, F32)


def moe_experts(xb, nw, w1, w3, w2, blk_expert, n_used):
    n_pad, d = xb.shape
    de = w1.shape[2]
    nb = n_pad // MOE_BLOCK
    grid_spec = pltpu.PrefetchScalarGridSpec(
        num_scalar_prefetch=2,
        grid=(nb,),
        in_specs=[pl.BlockSpec((MOE_BLOCK, d), lambda b, be, nu: (jnp.minimum(b, nu[0] - 1), 0)),
                  pl.BlockSpec((1, d), lambda b, be, nu: (0, 0)),
                  pl.BlockSpec((1, d, de), lambda b, be, nu: (be[b], 0, 0)),
                  pl.BlockSpec((1, d, de), lambda b, be, nu: (be[b], 0, 0)),
                  pl.BlockSpec((1, de, d), lambda b, be, nu: (be[b], 0, 0))],
        out_specs=pl.BlockSpec((MOE_BLOCK, d), lambda b, be, nu: (b, 0)),
        scratch_shapes=[pltpu.VMEM((d, de), BF16), pltpu.VMEM((d, de), BF16),
                        pltpu.VMEM((de, d), BF16)],
    )
    return pl.pallas_call(
        _expert_kernel,
        grid_spec=grid_spec,
        out_shape=jax.ShapeDtypeStruct((n_pad, d), F32),
        compiler_params=_cparams("arbitrary"),
        name="moe_experts",
    )(blk_expert, n_used, xb, nw.reshape(1, d), w1, w3, w2)


def _combine_ple_kernel(pos_hbm, yb_hbm, h_ref, gate_ref, p_ref, nw_ref, wg_ref, wp_ref, fw_ref,
                        o_ref, pos_smem, buf_ref, sem_idx, sem_rows, *, final, n_tiles):
    tq = h_ref.shape[0]
    sub = tq // LANES
    i = pl.program_id(0)
    n = n_tiles

    def fetch_pos(tile, slot):
        cp = pltpu.make_async_copy(pos_hbm.at[tile], pos_smem.at[slot], sem_idx)
        cp.start()
        cp.wait()

    def start_row(slot, k, rr, cc):
        _row_copy(yb_hbm, pos_smem[slot, k * sub + rr, cc], buf_ref.at[slot, k], rr * LANES + cc,
                  sem_rows.at[slot]).start()

    def wait_rows(slot):
        for k in range(2):
            pltpu.make_async_copy(yb_hbm.at[pl.ds(0, tq)], buf_ref.at[slot, k],
                                  sem_rows.at[slot]).wait()

    def combine(slot):
        gate = gate_ref[...]
        h2 = h_ref[...] + gate[:, 0:1] * buf_ref[slot, 0] + gate[:, 1:2] * buf_ref[slot, 1]
        hn = _rms(h2, nw_ref[...]).astype(BF16)
        pg = jax.nn.sigmoid(jnp.dot(hn, wg_ref[...], preferred_element_type=F32))
        pp = jnp.dot(p_ref[...].astype(BF16), wp_ref[...], preferred_element_type=F32)
        out = h2 + pg * pp
        if final:
            out = _rms(out, fw_ref[...])
        o_ref[...] = out

    @pl.when(i == 0)
    def _():
        fetch_pos(0, 0)
        for rr in range(sub):
            def body(cc, carry, rr=rr):
                for k in range(2):
                    start_row(0, k, rr, cc)
                return carry
            lax.fori_loop(0, LANES, body, 0)

    nxt = jnp.minimum(i + 1, n - 1)
    for slot in range(2):
        @pl.when(i % 2 == slot)
        def _(slot=slot):
            fetch_pos(nxt, 1 - slot)
            wait_rows(slot)
            for rr in range(sub):
                for cc in range(LANES):
                    for k in range(2):
                        start_row(1 - slot, k, rr, cc)
            combine(slot)

    @pl.when(i == n - 1)
    def _():
        wait_rows(1 - (n - 1) % 2)


def moe_combine_ple(h, yb, pos_tiles, gate_cols, p, nw, wg, wp, fw, *, final):
    t, d = h.shape
    tq = ROUTE_TILE
    pd = p.shape[1]
    return pl.pallas_call(
        functools.partial(_combine_ple_kernel, final=final, n_tiles=t // tq),
        grid=(t // tq,),
        in_specs=[pl.BlockSpec(memory_space=pl.ANY),
                  pl.BlockSpec(memory_space=pl.ANY),
                  pl.BlockSpec((tq, d), lambda i: (i, 0)),
                  pl.BlockSpec((tq, 2), lambda i: (i, 0)),
                  pl.BlockSpec((tq, pd), lambda i: (i, 0)),
                  pl.BlockSpec((1, d), lambda i: (0, 0)),
                  pl.BlockSpec((d, d), lambda i: (0, 0)),
                  pl.BlockSpec((pd, d), lambda i: (0, 0)),
                  pl.BlockSpec((1, d), lambda i: (0, 0))],
        out_specs=pl.BlockSpec((tq, d), lambda i: (i, 0)),
        out_shape=jax.ShapeDtypeStruct((t, d), F32),
        scratch_shapes=[pltpu.SMEM((2, 2 * tq // LANES, LANES), I32),
                        pltpu.VMEM((2, 2, tq, d), F32),
                        pltpu.SemaphoreType.DMA, pltpu.SemaphoreType.DMA((2,))],
        compiler_params=_cparams("arbitrary"),
        name="moe_combine_ple",
    )(pos_tiles, yb, h, gate_cols, p, nw.reshape(1, d), wg, wp, fw.reshape(1, d))


def moe_ple_layer(h, p, norm_moe, w_group, b_group, w_expert, b_expert, w1, w3, w2,
                  norm_ple, wg, wp, fw, *, layer, final):
    t, d = h.shape
    eid_tiles, rank_tiles, gate, counts = moe_router(h, norm_moe, w_group, b_group, w_expert,
                                                     b_expert)
    padded = (counts + MOE_BLOCK - 1) // MOE_BLOCK * MOE_BLOCK
    pend = jnp.cumsum(padded)
    pstart = pend - padded
    na = 2 * t
    n_pad = (na + MOE_BLOCK - 1) // MOE_BLOCK * MOE_BLOCK + N_EXPERTS * MOE_BLOCK
    nb = n_pad // MOE_BLOCK
    blk_start = jnp.arange(nb, dtype=I32) * MOE_BLOCK
    blk_expert = jnp.minimum(jnp.sum(blk_start[:, None] >= pend[None, :], axis=-1),
                             N_EXPERTS - 1).astype(I32)
    n_used = (pend[-1:] // MOE_BLOCK).astype(I32)
    pos_tiles = moe_slots(pstart.astype(I32), eid_tiles, rank_tiles)
    xb = moe_dispatch(h, pos_tiles, (pstart + counts).astype(I32), pend.astype(I32), n_used, n_pad)
    yb = moe_experts(xb, norm_moe, w1, w3, w2, blk_expert + layer * N_EXPERTS, n_used)
    return moe_combine_ple(h, yb, pos_tiles, gate.T, p, norm_ple, wg.astype(BF16),
                           wp.astype(BF16), fw, final=final)


def _pad_cols(w, n):
    return jnp.pad(w, ((0, 0), (0, n - w.shape[1])))


def kernel(x, p, norm_mix, norm_moe, norm_ple, final_norm, m_in_w, m_conv_w, m_conv_b, m_dt_bias, m_A_log, m_D, m_norm_w, m_out_w, g_in_w, g_conv_w, g_dt_bias, g_A_log, g_norm_w, g_out_w, moe_w_group, moe_b_group, moe_w_expert, moe_b_expert, moe_w1, moe_w3, moe_w2, ple_w_proj, ple_w_gate):
    batch, seq, d = x.shape
    t = batch * seq
    depth = p.shape[0]
    pd = p.shape[-1]
    h = x.reshape(t, d)
    p2 = p.reshape(depth, t, pd)
    w1_all = moe_w1.reshape((-1,) + moe_w1.shape[2:])
    w3_all = moe_w3.reshape((-1,) + moe_w3.shape[2:])
    w2_all = moe_w2.reshape((-1,) + moe_w2.shape[2:])
    for i in range(depth):
        j = i // 2
        if i % 2 == 0:
            inner = m_out_w.shape[1]
            heads = m_dt_bias.shape[1]
            conv_dim = m_conv_w.shape[2]
            w_in = _pad_cols(m_in_w[j], inner + conv_dim + LANES).astype(BF16)
            proj = norm_matmul(h, norm_mix[i], w_in)
            y = ssd_mixer_core(proj, m_conv_w[j], m_conv_b[j], m_dt_bias[j], m_A_log[j], m_D[j],
                               m_norm_w[j], batch=batch, seq=seq, inner=inner, heads=heads)
            h = matmul_residual(y, m_out_w[j].astype(BF16), h)
        else:
            v_dim = g_out_w.shape[1]
            v_heads = g_dt_bias.shape[1]
            conv_dim = g_conv_w.shape[2]
            qk_dim = (conv_dim - v_dim) // 2
            w_in = _pad_cols(g_in_w[j], conv_dim + v_dim + LANES).astype(BF16)
            proj = norm_matmul(h, norm_mix[i], w_in)
            y = gdn_mixer_core(proj, g_conv_w[j], g_dt_bias[j], g_A_log[j], g_norm_w[j],
                               batch=batch, seq=seq, qk_dim=qk_dim, v_dim=v_dim, v_heads=v_heads)
            h = matmul_residual(y, g_out_w[j].astype(BF16), h)
        h = moe_ple_layer(h, p2[i], norm_moe[i], moe_w_group[i], moe_b_group[i], moe_w_expert[i],
                          moe_b_expert[i], w1_all, w3_all, w2_all, norm_ple[i],
                          ple_w_gate[i], ple_w_proj[i], final_norm, layer=i,
                          final=(i == depth - 1))
    return h.reshape(batch, seq, d)
```

```python
import functools
import math

import jax
import jax.numpy as jnp
from jax import lax
from jax.experimental import pallas as pl
from jax.experimental.pallas import tpu as pltpu

F32 = jnp.float32
BF16 = jnp.bfloat16
I32 = jnp.int32
EPS = 1e-6
HIGHEST = lax.Precision.HIGHEST

LANES = 128
SUBLANES = 8
VMEM_LIMIT = 56 * 1024 * 1024

CONV_K = 4
SSD_CHUNK = 128
SSD_HEADDIM = 64
SSD_STATE = 128
SSD_GROUPS = 4
GDN_CHUNK = 64
GDN_HEAD = 128
GDN_STACK = 256
GDN_SUBBLOCK = 16
GDN_BATCH_PER_STEP = 1
N_EGROUPS = 8
EXPERTS_PER_GROUP = 8
N_EXPERTS = N_EGROUPS * EXPERTS_PER_GROUP
MOE_BLOCK = 512
ROUTE_TILE = 512
DMA_UNROLL = 8


def _cparams(*sem):
    return pltpu.CompilerParams(dimension_semantics=sem, vmem_limit_bytes=VMEM_LIMIT)


def _silu(x):
    return x * jax.nn.sigmoid(x)


def _softplus(x):
    return jnp.maximum(x, 0.0) + jnp.log1p(jnp.exp(-jnp.abs(x)))


def _rms(x, w):
    return x * lax.rsqrt(jnp.mean(x * x, axis=-1, keepdims=True) + EPS) * w


def _dot(a, b):
    return jnp.dot(a.astype(BF16), b.astype(BF16), preferred_element_type=F32)


def _split(x):
    hi = x.astype(BF16)
    return hi, (x - hi.astype(F32)).astype(BF16)


def _dot_f32(a, b):
    return jnp.dot(a, b, preferred_element_type=F32, precision=HIGHEST)


def _dot_tn(a, b):
    return lax.dot_general(a.astype(BF16), b.astype(BF16), (((0,), (0,)), ((), ())),
                           preferred_element_type=F32)


def _dot_nt(a, b):
    return lax.dot_general(a.astype(BF16), b.astype(BF16), (((1,), (1,)), ((), ())),
                           preferred_element_type=F32)


def _tri(n, strict=False):
    r = lax.broadcasted_iota(I32, (n, n), 0)
    c = lax.broadcasted_iota(I32, (n, n), 1)
    return (r > c) if strict else (r >= c)


def _norm_matmul_kernel(x_ref, nw_ref, w_ref, o_ref, *, n_chunk):
    xb = _rms(x_ref[...], nw_ref[...]).astype(BF16)
    n = o_ref.shape[1]
    for c0 in range(0, n, n_chunk):
        c1 = min(c0 + n_chunk, n)
        o_ref[:, c0:c1] = jnp.dot(xb, w_ref[:, c0:c1], preferred_element_type=F32)


def norm_matmul(x, nw, w, *, tm=256, n_chunk=512):
    t, d = x.shape
    n = w.shape[1]
    return pl.pallas_call(
        functools.partial(_norm_matmul_kernel, n_chunk=n_chunk),
        grid=(t // tm,),
        in_specs=[pl.BlockSpec((tm, d), lambda i: (i, 0)),
                  pl.BlockSpec((1, d), lambda i: (0, 0)),
                  pl.BlockSpec((d, n), lambda i: (0, 0))],
        out_specs=pl.BlockSpec((tm, n), lambda i: (i, 0)),
        out_shape=jax.ShapeDtypeStruct((t, n), F32),
        compiler_params=_cparams("arbitrary"),
        name="norm_matmul",
    )(x, nw.reshape(1, d), w)


def _matmul_residual_kernel(y_ref, w_ref, r_ref, o_ref):
    o_ref[...] = r_ref[...] + jnp.dot(y_ref[...].astype(BF16), w_ref[...],
                                      preferred_element_type=F32)


def matmul_residual(y, w, res, *, tm=512):
    t, k = y.shape
    d = w.shape[1]
    return pl.pallas_call(
        _matmul_residual_kernel,
        grid=(t // tm,),
        in_specs=[pl.BlockSpec((tm, k), lambda i: (i, 0)),
                  pl.BlockSpec((k, d), lambda i: (0, 0)),
                  pl.BlockSpec((tm, d), lambda i: (i, 0))],
        out_specs=pl.BlockSpec((tm, d), lambda i: (i, 0)),
        out_shape=jax.ShapeDtypeStruct((t, d), F32),
        compiler_params=_cparams("arbitrary"),
        name="matmul_residual",
    )(y, w, res)


def _conv_silu(x_ref, stage_ref, w_ref, bias, first):
    q = x_ref.shape[0]

    @pl.when(first)
    def _():
        stage_ref[0:SUBLANES, :] = jnp.zeros((SUBLANES, stage_ref.shape[1]), F32)

    stage_ref[SUBLANES:SUBLANES + q, :] = x_ref[...]
    acc = stage_ref[SUBLANES:SUBLANES + q, :] * w_ref[CONV_K - 1:CONV_K, :]
    for j in range(CONV_K - 1):
        off = SUBLANES - (CONV_K - 1) + j
        acc = acc + stage_ref[off:off + q, :] * w_ref[j:j + 1, :]
    if bias is not None:
        acc = acc + bias
    stage_ref[0:SUBLANES, :] = stage_ref[q:q + SUBLANES, :]
    return _silu(acc)


def _ssd_kernel(z_ref, x_ref, b_ref, c_ref, dt_ref,
                cwx_ref, cwb_ref, cwc_ref, cbx_ref, cbb_ref, cbc_ref,
                dtb_ref, alog_ref, dfull_ref, nw_ref,
                o_ref,
                sx_ref, sb_ref, sc_ref, y_ref, xw_ref, st_ref):
    q = x_ref.shape[0]
    hpg = x_ref.shape[1] // (SSD_GROUPS * SSD_HEADDIM)
    first = pl.program_id(1) == 0

    @pl.when(first)
    def _():
        st_ref[...] = jnp.zeros(st_ref.shape, F32)

    xs = _conv_silu(x_ref, sx_ref, cwx_ref, cbx_ref[...], first)
    bm = _conv_silu(b_ref, sb_ref, cwb_ref, cbb_ref[...], first)
    cm = _conv_silu(c_ref, sc_ref, cwc_ref, cbc_ref[...], first)

    dt = _softplus(dt_ref[...] + dtb_ref[...])
    da = dt * (-jnp.exp(alog_ref[...]))
    causal = _tri(q)
    a = _dot_f32(causal.astype(F32), da)
    a_t = a.T
    ea_t = jnp.exp(a_t)
    a_last = a[q - 1:q, :]
    to_end_t = (jnp.exp(a_last - a) * dt).T
    ea_last = jnp.exp(a_last)
    dt_t = dt.T
    eye = (lax.broadcasted_iota(I32, (q, q), 0) == lax.broadcasted_iota(I32, (q, q), 1))
    lane_lo = lax.broadcasted_iota(I32, (q, 2 * SSD_HEADDIM), 1) < SSD_HEADDIM

    for g in range(SSD_GROUPS):
        bg = bm[:, g * SSD_STATE:(g + 1) * SSD_STATE]
        cg = cm[:, g * SSD_STATE:(g + 1) * SSD_STATE]
        cb = _dot_nt(cg, bg)
        gw = hpg * SSD_HEADDIM
        cs = _dot(cg, st_ref[g])
        for pr in range(hpg // 2):
            j0 = g * hpg + 2 * pr
            lo, hi = j0 * SSD_HEADDIM, (j0 + 2) * SSD_HEADDIM
            xp = xs[:, lo:hi]
            cp = cs[:, 2 * pr * SSD_HEADDIM:(2 * pr + 2) * SSD_HEADDIM]
            x_sel = [jnp.where(lane_lo, xp, 0.0), jnp.where(lane_lo, 0.0, xp)]
            c_sel = [jnp.where(lane_lo, cp, 0.0), jnp.where(lane_lo, 0.0, cp)]
            lhs, rhs, te = [], [], []
            for k in range(2):
                j = j0 + k
                diff = a[:, j:j + 1] - a_t[j:j + 1, :]
                seg = jnp.exp(jnp.where(causal, diff, -jnp.inf))
                lhs += [cb * seg * dt_t[j:j + 1, :], jnp.where(eye, ea_t[j:j + 1, :], 0.0)]
                rhs += [x_sel[k], c_sel[k]]
                te.append(jnp.where(eye, to_end_t[j:j + 1, :], 0.0))
            y_ref[:, lo:hi] = _dot(jnp.concatenate(lhs, axis=1), jnp.concatenate(rhs, axis=0))
            xw_ref[:, lo:hi] = _dot(jnp.concatenate(te, axis=1), jnp.concatenate(x_sel, axis=0))
        upd = _dot_tn(bg, xw_ref[:, g * gw:(g + 1) * gw])
        for jj in range(hpg):
            j = g * hpg + jj
            sl = slice(jj * SSD_HEADDIM, (jj + 1) * SSD_HEADDIM)
            st_ref[g, :, sl] = st_ref[g, :, sl] * ea_last[:, j:j + 1] + upd[:, sl]

    y = y_ref[...] + dfull_ref[...] * xs
    o_ref[...] = _rms(y * _silu(z_ref[...]), nw_ref[...]).astype(o_ref.dtype)


def ssd_mixer_core(proj, conv_w, conv_b, dt_bias, a_log, d_skip, norm_w, *, batch, seq,
                   inner, heads):
    t = proj.shape[0]
    q = SSD_CHUNK
    gn = SSD_GROUPS * SSD_STATE
    nc = seq // q
    cwx, cwb, cwc = conv_w[:, :inner], conv_w[:, inner:inner + gn], conv_w[:, inner + gn:]
    cbx = conv_b[:inner].reshape(1, inner)
    cbb = conv_b[inner:inner + gn].reshape(1, gn)
    cbc = conv_b[inner + gn:].reshape(1, gn)
    pad = LANES - heads
    dtb = jnp.pad(dt_bias, (0, pad)).reshape(1, LANES)
    alog = jnp.pad(a_log, (0, pad)).reshape(1, LANES)
    dfull = jnp.repeat(d_skip, SSD_HEADDIM).reshape(1, inner)
    nw = norm_w.reshape(1, inner)

    def row(i, c):
        return i * nc + c

    const = lambda shape: pl.BlockSpec(shape, lambda i, c: (0, 0))
    x_blk = inner // inner
    b_blk = (2 * inner) // gn
    c_blk = b_blk + 1
    dt_blk = (2 * inner + 2 * gn) // LANES
    return pl.pallas_call(
        _ssd_kernel,
        grid=(batch, nc),
        in_specs=[pl.BlockSpec((q, inner), lambda i, c: (row(i, c), 0)),
                  pl.BlockSpec((q, inner), lambda i, c: (row(i, c), x_blk)),
                  pl.BlockSpec((q, gn), lambda i, c: (row(i, c), b_blk)),
                  pl.BlockSpec((q, gn), lambda i, c: (row(i, c), c_blk)),
                  pl.BlockSpec((q, LANES), lambda i, c: (row(i, c), dt_blk)),
                  const((CONV_K, inner)), const((CONV_K, gn)), const((CONV_K, gn)),
                  const((1, inner)), const((1, gn)), const((1, gn)),
                  const((1, LANES)), const((1, LANES)), const((1, inner)), const((1, inner))],
        out_specs=pl.BlockSpec((q, inner), lambda i, c: (row(i, c), 0)),
        out_shape=jax.ShapeDtypeStruct((t, inner), BF16),
        scratch_shapes=[pltpu.VMEM((q + SUBLANES, inner), F32),
                        pltpu.VMEM((q + SUBLANES, gn), F32),
                        pltpu.VMEM((q + SUBLANES, gn), F32),
                        pltpu.VMEM((q, inner), F32),
                        pltpu.VMEM((q, inner), F32),
                        pltpu.VMEM((SSD_GROUPS, SSD_STATE, inner // SSD_GROUPS), F32)],
        compiler_params=_cparams("arbitrary", "arbitrary"),
        name="ssd_scan",
    )(proj, proj, proj, proj, proj, cwx, cwb, cwc, cbx, cbb, cbc, dtb, alog, dfull, nw)


def _l2n(x):
    return x * lax.rsqrt(jnp.sum(x * x, axis=-1, keepdims=True) + EPS)


def _pair_cols(x, i0, i1, half):
    c = x.shape[0]
    lane = lax.broadcasted_iota(I32, (c, 2 * half), 1)
    return jnp.where(lane < half, jnp.broadcast_to(x[:, i0:i0 + 1], (c, 2 * half)),
                     jnp.broadcast_to(x[:, i1:i1 + 1], (c, 2 * half)))


def _gdn_kernel(q_ref, k_ref, v_ref, z_ref, ba_ref,
                cwq_ref, cwk_ref, cwv_ref, dtb_ref, alog_ref, nw_ref,
                o_ref,
                sq_ref, sk_ref, sv_ref, s_ref, *, v_heads):
    nb, c = q_ref.shape[0], q_ref.shape[1]
    qk_heads = q_ref.shape[2] // GDN_HEAD
    rep = v_heads // qk_heads
    gsz = GDN_STACK // c
    n_grp = v_heads // gsz
    first = pl.program_id(1) == 0

    @pl.when(first)
    def _():
        s_ref[...] = jnp.zeros(s_ref.shape, F32)

    t_idx = lax.broadcasted_iota(I32, (c, GDN_STACK), 0)
    s_idx = lax.broadcasted_iota(I32, (c, GDN_STACK), 1) & (c - 1)
    causal = t_idx >= s_idx
    strict = t_idx > s_idx
    eye = t_idx == s_idx
    eye_f = eye.astype(F32)
    sb_shift = GDN_SUBBLOCK.bit_length() - 1
    same_blk = (t_idx >> sb_shift) == (s_idx >> sb_shift)
    n_col = GDN_STACK // LANES
    lane_lo = lax.broadcasted_iota(I32, (c, LANES), 1) < c

    def bdiag(parts):
        zero = jnp.zeros((), BF16)
        cols = lambda p: [p[:, k * LANES:(k + 1) * LANES] for k in range(n_col)]
        return tuple([jnp.concatenate([jnp.where(lane_lo, y, zero), jnp.where(lane_lo, zero, y)],
                                      axis=0) for y in cols(p)] for p in parts)

    def mm(lhs, rhs_bd):
        n = len(lhs) * c
        parts = [_split(x) for x in lhs]
        l_hi = jnp.concatenate([p[0] for p in parts], axis=0)
        l_both = jnp.concatenate([l_hi] + [p[1] for p in parts], axis=0)
        outs = []
        for k in range(n_col):
            sl = slice(k * LANES, (k + 1) * LANES)
            r = jnp.dot(l_both[:, sl], rhs_bd[0][k], preferred_element_type=F32)
            outs.append(r[:n] + r[n:] + jnp.dot(l_hi[:, sl], rhs_bd[1][k],
                                                preferred_element_type=F32))
        out = jnp.concatenate(outs, axis=1)
        return [out[i * c:(i + 1) * c] for i in range(len(lhs))]

    def neumann(p0s, n_sq):
        ps = list(p0s)
        ds = [eye_f + p for p in ps]
        for i in range(n_sq):
            for g in range(len(ps)):
                bd = bdiag(_split(ps[g]))
                if i == 0:
                    ps[g], = mm([ps[g]], bd)
                else:
                    ps[g], x = mm([ps[g], ds[g]], bd)
                    ds[g] = ds[g] + x
        if n_sq > 0:
            for g in range(len(ps)):
                x, = mm([ds[g]], bdiag(_split(ps[g])))
                ds[g] = ds[g] + x
        return ds

    def prepare(b):
        qc = _conv_silu(q_ref.at[b], sq_ref.at[b], cwq_ref, None, first)
        kc = _conv_silu(k_ref.at[b], sk_ref.at[b], cwk_ref, None, first)
        vc = _conv_silu(v_ref.at[b], sv_ref.at[b], cwv_ref, None, first)
        ba = ba_ref[b]
        beta = jax.nn.sigmoid(ba)
        gl = -jnp.exp(alog_ref[...]) * _softplus(ba + dtb_ref[...])
        gcs = _dot_f32(_tri(c).astype(F32), gl)
        g_last = gcs[c - 1:c, :]
        st = dict(vc=vc, eg_last=jnp.exp(g_last), qns=[], kns=[], kts=[], a_kks=[], a_qks=[],
                  gcols=[], brows=[], grows=[])
        kks, qks = [], []
        for hq in range(qk_heads):
            sl = slice(hq * GDN_HEAD, (hq + 1) * GDN_HEAD)
            qn = _l2n(qc[:, sl]) * (GDN_HEAD ** -0.5)
            kn = _l2n(kc[:, sl])
            k2 = jnp.concatenate([kn] * rep, axis=0)
            st["qns"].append(qn)
            st["kns"].append(kn)
            st["kts"].append(kn.T.astype(BF16))
            kks.append(_dot_nt(kn, k2))
            qks.append(_dot_nt(qn, k2))
        for grp in range(n_grp):
            h0 = grp * gsz
            hq0 = h0 // rep
            nq = gsz // rep
            kk = jnp.concatenate(kks[hq0:hq0 + nq], axis=1)
            qk = jnp.concatenate(qks[hq0:hq0 + nq], axis=1)
            gcol = jnp.concatenate(
                [_pair_cols(gcs, v_heads + h0 + 2 * i, v_heads + h0 + 2 * i + 1, c)
                 for i in range(gsz // 2)], axis=1)
            bcol = jnp.concatenate([_pair_cols(beta, h0 + 2 * i, h0 + 2 * i + 1, c)
                                    for i in range(gsz // 2)], axis=1)
            grow = jnp.sum(jnp.where(eye, gcol, 0.0), axis=0, keepdims=True)
            decay = jnp.exp(jnp.where(causal, gcol - grow, -jnp.inf))
            st["a_kks"].append(jnp.where(strict, bcol * kk * decay, 0.0))
            st["a_qks"].append(jnp.where(causal, qk * decay, 0.0))
            st["gcols"].append(gcol)
            st["grows"].append(grow)
            st["brows"].append(jnp.sum(jnp.where(eye, bcol, 0.0), axis=0, keepdims=True))
        return st

    def pair_blocks(b0, b1):
        z = jnp.zeros((c, GDN_HEAD), BF16)
        return jnp.concatenate([jnp.concatenate([b0.astype(BF16), z], axis=1),
                                jnp.concatenate([z, b1.astype(BF16)], axis=1)], axis=0)

    def heads(b, st, grp, tinv):
        h0 = grp * gsz
        gcol, grow, brow = st["gcols"][grp], st["grows"][grp], st["brows"][grp]
        hd = lambda x, i: x[:, i * GDN_HEAD:(i + 1) * GDN_HEAD]
        col = lambda x, k: x[:, k * LANES:(k + 1) * LANES]
        tb = tinv * brow
        tbe = tinv * (brow * jnp.exp(grow))
        us, ws_ = [], []
        for k in range(n_col):
            ha = h0 + 2 * k
            us.append(_dot(col(tb, k), pair_blocks(hd(st["vc"], ha), hd(st["vc"], ha + 1))))
            ws_.append(_dot(col(tbe, k), pair_blocks(st["kns"][ha // rep],
                                                      st["kns"][(ha + 1) // rep])))
        u = jnp.concatenate(us, axis=1)
        w = jnp.concatenate(ws_, axis=1)
        yield
        v_news, q_states = [], []
        for i in range(gsz):
            hq = (h0 + i) // rep
            ws = _dot(jnp.concatenate([hd(w, i), st["qns"][hq]], axis=0), s_ref[b, h0 + i])
            v_news.append(hd(u, i) - ws[:c])
            q_states.append(ws[c:])
        yield
        eg_diag =jnp.where(eye, jnp.exp(gcol), 0.0)
        eg_end_diag = jnp.where(eye, jnp.exp(gcol[c - 1:c, :] - gcol), 0.0)
        os_, vss = [], []
        for k in range(n_col):
            pv = pair_blocks(v_news[2 * k], v_news[2 * k + 1])
            pq = pair_blocks(q_states[2 * k], q_states[2 * k + 1])
            os_.append(_dot(jnp.concatenate([col(st["a_qks"][grp], k), col(eg_diag, k)], axis=1),
                            jnp.concatenate([pv, pq], axis=0)))
            vss.append(_dot(col(eg_end_diag, k), pv))
        o = jnp.concatenate(os_, axis=1)
        v_scaled = jnp.concatenate(vss, axis=1)
        yield
        for i in range(gsz):
            h = h0 + i
            gi = v_heads + h
            hs = slice(h * GDN_HEAD, (h + 1) * GDN_HEAD)
            s_ref[b, h] = (s_ref[b, h] * st["eg_last"][:, gi:gi + 1]
                           + _dot(st["kts"][h // rep], hd(v_scaled, i)))
            o_ref[b, :, hs] = (_rms(hd(o, i), nw_ref[...])
                               * _silu(z_ref[b, :, hs])).astype(o_ref.dtype)

    sts = [prepare(b) for b in range(nb)]

    a_all = [a for st in sts for a in st["a_kks"]]
    dms = neumann([jnp.where(same_blk, -a, 0.0) for a in a_all], sb_shift - 1)
    ms = [-mm([d], bdiag(_split(jnp.where(same_blk, 0.0, a))))[0] for d, a in zip(dms, a_all)]
    ws = neumann(ms, (c // GDN_SUBBLOCK).bit_length() - 2)
    tinvs = [mm([w], bdiag(_split(d)))[0] for w, d in zip(ws, dms)]

    gens = [heads(b, sts[b], grp, tinvs[b * n_grp + grp]) for grp in range(n_grp)
            for b in range(nb)]
    for _ in range(4):
        for gen in gens:
            next(gen, None)


def gdn_mixer_core(proj, conv_w, dt_bias, a_log, norm_w, *, batch, seq, qk_dim, v_dim, v_heads):
    t, n = proj.shape
    c = GDN_CHUNK
    nb = GDN_BATCH_PER_STEP
    cwq, cwk, cwv = conv_w[:, :qk_dim], conv_w[:, qk_dim:2 * qk_dim], conv_w[:, 2 * qk_dim:]
    dtb = jnp.pad(dt_bias, (v_heads, LANES - 2 * v_heads)).reshape(1, LANES)
    alog = jnp.pad(a_log, (v_heads, LANES - 2 * v_heads)).reshape(1, LANES)
    nw = norm_w.reshape(1, GDN_HEAD)
    proj3 = proj.reshape(batch, seq, n)

    const = lambda shape: pl.BlockSpec(shape, lambda i, cc: (0, 0))
    blk = lambda width, col: pl.BlockSpec((nb, c, width), lambda i, cc: (i, cc, col))
    v_blk = (2 * qk_dim) // v_dim
    z_blk = v_blk + 1
    ba_blk = (2 * qk_dim + 2 * v_dim) // LANES
    out = pl.pallas_call(
        functools.partial(_gdn_kernel, v_heads=v_heads),
        grid=(batch // nb, seq // c),
        in_specs=[blk(qk_dim, 0), blk(qk_dim, 1), blk(v_dim, v_blk), blk(v_dim, z_blk),
                  blk(LANES, ba_blk),
                  const((CONV_K, qk_dim)), const((CONV_K, qk_dim)), const((CONV_K, v_dim)),
                  const((1, LANES)), const((1, LANES)), const((1, GDN_HEAD))],
        out_specs=blk(v_dim, 0),
        out_shape=jax.ShapeDtypeStruct((batch, seq, v_dim), BF16),
        scratch_shapes=[pltpu.VMEM((nb, c + SUBLANES, qk_dim), F32),
                        pltpu.VMEM((nb, c + SUBLANES, qk_dim), F32),
                        pltpu.VMEM((nb, c + SUBLANES, v_dim), F32),
                        pltpu.VMEM((nb, v_heads, GDN_HEAD, GDN_HEAD), F32)],
        compiler_params=_cparams("arbitrary", "arbitrary"),
        name="gdn_scan",
    )(proj3, proj3, proj3, proj3, proj3, cwq, cwk, cwv, dtb, alog, nw)
    return out.reshape(t, v_dim)


def _first_index(mask, n):
    idx = lax.broadcasted_iota(I32, mask.shape, 0)
    return jnp.min(jnp.where(mask, idx, n), axis=0, keepdims=True)


def _router_kernel(h_ref, nw_ref, wr_ref, br_ref, upper_ref,
                   eid_ref, rank_ref, gate_ref, cnt_ref, base_ref):
    tq = h_ref.shape[0]
    i = pl.program_id(0)

    @pl.when(i == 0)
    def _():
        base_ref[...] = jnp.zeros(base_ref.shape, F32)

    xn = _rms(h_ref[...], nw_ref[...])
    x_hi, x_lo = _split(xn)
    w_hi, w_lo = wr_ref[0], wr_ref[1]
    logits = (_dot_nt(jnp.concatenate([w_hi, w_lo], axis=0), x_hi)
              + jnp.concatenate([_dot_nt(w_hi, x_lo), jnp.zeros((LANES, tq), F32)], axis=0))
    logits = logits[:LANES] + logits[LANES:] + br_ref[:, 0:1]
    gl = logits[0:N_EGROUPS, :]
    gmax = jnp.max(gl, axis=0, keepdims=True)
    g_sel = _first_index(gl == gmax, N_EGROUPS)
    p_sel = 1.0 / jnp.sum(jnp.exp(gl - gmax), axis=0, keepdims=True)
    e_in = logits[N_EGROUPS:N_EGROUPS + EXPERTS_PER_GROUP, :]
    for g in range(1, N_EGROUPS):
        lo = N_EGROUPS + g * EXPERTS_PER_GROUP
        e_in = jnp.where(g_sel == g, logits[lo:lo + EXPERTS_PER_GROUP, :], e_in)
    m1 = jnp.max(e_in, axis=0, keepdims=True)
    i1 = _first_index(e_in == m1, EXPERTS_PER_GROUP)
    sub = lax.broadcasted_iota(I32, e_in.shape, 0)
    rest = jnp.where(sub == i1, -jnp.inf, e_in)
    m2 = jnp.max(rest, axis=0, keepdims=True)
    i2 = _first_index(rest == m2, EXPERTS_PER_GROUP)
    e2 = jnp.exp(m2 - m1)
    denom = 1.0 + e2
    gate_ref[0:1, :] = (1.0 / denom) * p_sel
    gate_ref[1:2, :] = (e2 / denom) * p_sel
    eid0 = g_sel * EXPERTS_PER_GROUP + i1
    eid1 = g_sel * EXPERTS_PER_GROUP + i2

    erow = lax.broadcasted_iota(I32, (N_EXPERTS, tq), 0)
    oh0 = erow == eid0
    oh1 = erow == eid1
    oh0f = oh0.astype(F32)
    oh1f = oh1.astype(F32)
    cum0 = jnp.dot(oh0f.astype(BF16), upper_ref[...], preferred_element_type=F32)
    cum1 = jnp.dot(oh1f.astype(BF16), upper_ref[...], preferred_element_type=F32)
    base = base_ref[:, 0:1]
    tot0 = jnp.sum(oh0f, axis=1, keepdims=True)
    tot1 = jnp.sum(oh1f, axis=1, keepdims=True)
    r0 = jnp.sum(jnp.where(oh0, base + cum0, 0.0), axis=0, keepdims=True)
    r1 = jnp.sum(jnp.where(oh1, base + tot0 + cum1, 0.0), axis=0, keepdims=True)
    sub = tq // LANES
    for k, (e, r) in enumerate(((eid0, r0.astype(I32)), (eid1, r1.astype(I32)))):
        for rr in range(sub):
            row = k * sub + rr
            eid_ref[0, row:row + 1, :] = e[:, rr * LANES:(rr + 1) * LANES]
            rank_ref[0, row:row + 1, :] = r[:, rr * LANES:(rr + 1) * LANES]
    new_base = base + tot0 + tot1
    base_ref[...] = jnp.broadcast_to(new_base, base_ref.shape)
    cnt_ref[...] = jnp.broadcast_to(new_base, cnt_ref.shape).astype(I32)


def moe_router(h, nw, w_group, b_group, w_expert, b_expert):
    t, d = h.shape
    tq = ROUTE_TILE
    nr = N_EGROUPS + N_EXPERTS
    wr = jnp.pad(jnp.concatenate([w_group, w_expert], axis=1).T, ((0, LANES - nr), (0, 0)))
    br = jnp.pad(jnp.concatenate([b_group, b_expert]), (0, LANES - nr))
    br = jnp.broadcast_to(br[:, None], (LANES, LANES))
    upper = (jnp.arange(tq)[:, None] < jnp.arange(tq)[None, :]).astype(BF16)
    rows = 2 * tq // LANES
    tile_spec = pl.BlockSpec((1, rows, LANES), lambda i: (i, 0, 0))
    tile_shape = jax.ShapeDtypeStruct((t // tq, rows, LANES), I32)
    eid, rank, gate, cnt = pl.pallas_call(
        _router_kernel,
        grid=(t // tq,),
        in_specs=[pl.BlockSpec((tq, d), lambda i: (i, 0)),
                  pl.BlockSpec((1, d), lambda i: (0, 0)),
                  pl.BlockSpec((2, LANES, d), lambda i: (0, 0, 0)),
                  pl.BlockSpec((LANES, LANES), lambda i: (0, 0)),
                  pl.BlockSpec((tq, tq), lambda i: (0, 0))],
        out_specs=[tile_spec, tile_spec, pl.BlockSpec((2, tq), lambda i: (0, i)),
                   pl.BlockSpec((N_EXPERTS, LANES), lambda i: (0, 0))],
        out_shape=[tile_shape, tile_shape, jax.ShapeDtypeStruct((2, t), F32),
                   jax.ShapeDtypeStruct((N_EXPERTS, LANES), I32)],
        scratch_shapes=[pltpu.VMEM((N_EXPERTS, LANES), F32)],
        compiler_params=_cparams("arbitrary"),
        name="moe_router",
    )(h, nw.reshape(1, d), jnp.stack(_split(wr)), br, upper)
    return eid, rank, gate, cnt[:, 0]


def _slot_kernel(pstart_ref, eid_ref, rank_ref, pos_ref):
    eid = eid_ref[...]

    def body(e, acc):
        return jnp.where(eid == e, pstart_ref[e], acc)

    pos_ref[...] = rank_ref[...] + lax.fori_loop(0, N_EXPERTS, body, jnp.zeros(eid.shape, I32))


def moe_slots(pstart, eid_tiles, rank_tiles):
    nt, rows, lanes = eid_tiles.shape
    tiles_per_step = math.gcd(nt, 16)
    blk = pl.BlockSpec((tiles_per_step, rows, lanes), lambda i, ps: (i, 0, 0))
    return pl.pallas_call(
        _slot_kernel,
        grid_spec=pltpu.PrefetchScalarGridSpec(num_scalar_prefetch=1, grid=(nt // tiles_per_step,),
                                               in_specs=[blk, blk], out_specs=blk),
        out_shape=jax.ShapeDtypeStruct(eid_tiles.shape, I32),
        compiler_params=_cparams("arbitrary"),
        name="moe_slots",
    )(pstart, eid_tiles, rank_tiles)


def _row_copy(src, src_row, dst, dst_row, sem):
    return pltpu.make_async_copy(src.at[pl.ds(src_row, 1)], dst.at[pl.ds(dst_row, 1)], sem)


def _dispatch_kernel(lo_ref, hi_ref, nu_ref, pos_hbm, h_ref, xb_out, pos_smem, zero_ref, sem_idx,
                     sem_rows, sem_fill):
    tq = h_ref.shape[0]
    sub = tq // LANES
    i = pl.program_id(0)
    n_blocks = xb_out.shape[0] // MOE_BLOCK

    def zero_rows(start, size, wait):
        cp = pltpu.make_async_copy(zero_ref.at[pl.ds(0, size)], xb_out.at[pl.ds(start, size)],
                                   sem_fill)
        cp.wait() if wait else cp.start()

    def fill_padding(e, wait):
        lo, hi = lo_ref[e], hi_ref[e]
        lo8 = jnp.minimum((lo + (SUBLANES - 1)) & -SUBLANES, hi)
        for u in range(SUBLANES - 1):
            @pl.when(lo + u < lo8)
            def _(u=u):
                zero_rows(lo + u, 1, wait)
        cur = lo8
        size = MOE_BLOCK // 2
        while size >= SUBLANES:
            take = ((hi - lo8) & size) != 0

            @pl.when(take)
            def _(cur=cur, size=size):
                zero_rows(pl.multiple_of(cur, SUBLANES), size, wait)
            cur = cur + jnp.where(take, size, 0)
            size //= 2

    def fill_tail(blk, wait):
        zero_rows(pl.multiple_of(blk * MOE_BLOCK, MOE_BLOCK), MOE_BLOCK, wait)

    @pl.when(i == 0)
    def _():
        zero_ref[...] = jnp.zeros(zero_ref.shape, F32)
        for wait in (False, True):
            def per_expert(e, carry, wait=wait):
                fill_padding(e, wait)
                return carry

            def per_block(blk, carry, wait=wait):
                fill_tail(blk, wait)
                return carry

            lax.fori_loop(0, N_EXPERTS, per_expert, 0)
            lax.fori_loop(nu_ref[0], n_blocks, per_block, 0)

    cp = pltpu.make_async_copy(pos_hbm.at[i], pos_smem, sem_idx)
    cp.start()
    cp.wait()
    for rr in range(sub):
        def body(cu, carry, rr=rr):
            for u in range(DMA_UNROLL):
                cc = cu * DMA_UNROLL + u
                r = rr * LANES + cc
                _row_copy(h_ref, r, xb_out, pos_smem[rr, cc], sem_rows).start()
                _row_copy(h_ref, r, xb_out, pos_smem[sub + rr, cc], sem_rows).start()
            return carry
        lax.fori_loop(0, LANES // DMA_UNROLL, body, 0)
    pltpu.make_async_copy(h_ref, xb_out.at[pl.ds(0, tq)], sem_rows).wait()
    pltpu.make_async_copy(h_ref, xb_out.at[pl.ds(0, tq)], sem_rows).wait()


def moe_dispatch(h, pos_tiles, pad_lo, pad_hi, n_used, n_pad):
    t, d = h.shape
    tq = ROUTE_TILE
    grid_spec = pltpu.PrefetchScalarGridSpec(
        num_scalar_prefetch=3,
        grid=(t // tq,),
        in_specs=[pl.BlockSpec(memory_space=pl.ANY),
                  pl.BlockSpec((tq, d), lambda i, lo, hi, nu: (i, 0))],
        out_specs=pl.BlockSpec(memory_space=pl.ANY),
        scratch_shapes=[pltpu.SMEM((2 * tq // LANES, LANES), I32),
                        pltpu.VMEM((MOE_BLOCK, d), F32),
                        pltpu.SemaphoreType.DMA, pltpu.SemaphoreType.DMA,
                        pltpu.SemaphoreType.DMA],
    )
    return pl.pallas_call(
        _dispatch_kernel,
        grid_spec=grid_spec,
        out_shape=jax.ShapeDtypeStruct((n_pad, d), F32),
        compiler_params=_cparams("arbitrary"),
        name="moe_dispatch",
    )(pad_lo, pad_hi, n_used, pos_tiles, h)


def _expert_kernel(be_ref, nu_ref, x_ref, nw_ref, w1_ref, w3_ref, w2_ref, o_ref,
                   w1b_ref, w3b_ref, w2b_ref):
    b = pl.program_id(0)
    used = b < nu_ref[0]
    new_expert = (b == 0) | (be_ref[b] != be_ref[jnp.maximum(b - 1, 0)])

    @pl.when(used & new_expert)
    def _():
        w1b_ref[...] = w1_ref[0].astype(BF16)
        w3b_ref[...] = w3_ref[0].astype(BF16)
        w2b_ref[...] = w2_ref[0].astype(BF16)

    @pl.when(used)
    def _():
        xb = _rms(x_ref[...], nw_ref[...]).astype(BF16)
        h1 = jnp.dot(xb, w1b_ref[...], preferred_element_type=F32)
        h3 = jnp.dot(xb, w3b_ref[...], preferred_element_type=F32)
        hid = (_silu(h1) * h3).astype(BF16)
        o_ref[...] = jnp.dot(hid, w2b_ref[...], preferred_element_type=F32)

    @pl.when(b >= nu_ref[0])
    def _():
        o_ref[...] = jnp.zeros(o_ref.shape, F32)


def moe_experts(xb, nw, w1, w3, w2, blk_expert, n_used):
    n_pad, d = xb.shape
    de = w1.shape[2]
    nb = n_pad // MOE_BLOCK
    grid_spec = pltpu.PrefetchScalarGridSpec(
        num_scalar_prefetch=2,
        grid=(nb,),
        in_specs=[pl.BlockSpec((MOE_BLOCK, d), lambda b, be, nu: (jnp.minimum(b, nu[0] - 1), 0)),
                  pl.BlockSpec((1, d), lambda b, be, nu: (0, 0)),
                  pl.BlockSpec((1, d, de), lambda b, be, nu: (be[b], 0, 0)),
                  pl.BlockSpec((1, d, de), lambda b, be, nu: (be[b], 0, 0)),
                  pl.BlockSpec((1, de, d), lambda b, be, nu: (be[b], 0, 0))],
        out_specs=pl.BlockSpec((MOE_BLOCK, d), lambda b, be, nu: (b, 0)),
        scratch_shapes=[pltpu.VMEM((d, de), BF16), pltpu.VMEM((d, de), BF16),
                        pltpu.VMEM((de, d), BF16)],
    )
    return pl.pallas_call(
        _expert_kernel,
        grid_spec=grid_spec,
        out_shape=jax.ShapeDtypeStruct((n_pad, d), F32),
        compiler_params=_cparams("arbitrary"),
        name="moe_experts",
    )(blk_expert, n_used, xb, nw.reshape(1, d), w1, w3, w2)


def _combine_ple_kernel(pos_hbm, yb_hbm, h_ref, gate_ref, p_ref, nw_ref, wg_ref, wp_ref, fw_ref,
                        o_ref, pos_smem, buf_ref, sem_idx, sem_rows, *, final, n_tiles):
    tq = h_ref.shape[0]
    sub = tq // LANES
    i = pl.program_id(0)
    n = n_tiles

    def fetch_pos(tile, slot):
        cp = pltpu.make_async_copy(pos_hbm.at[tile], pos_smem.at[slot], sem_idx)
        cp.start()
        cp.wait()

    def start_row(slot, k, rr, cc):
        _row_copy(yb_hbm, pos_smem[slot, k * sub + rr, cc], buf_ref.at[slot, k], rr * LANES + cc,
                  sem_rows.at[slot]).start()

    def wait_rows(slot):
        for k in range(2):
            pltpu.make_async_copy(yb_hbm.at[pl.ds(0, tq)], buf_ref.at[slot, k],
                                  sem_rows.at[slot]).wait()

    def combine(slot):
        gate = gate_ref[...]
        h2 = h_ref[...] + gate[:, 0:1] * buf_ref[slot, 0] + gate[:, 1:2] * buf_ref[slot, 1]
        hn = _rms(h2, nw_ref[...]).astype(BF16)
        pg = jax.nn.sigmoid(jnp.dot(hn, wg_ref[...], preferred_element_type=F32))
        pp = jnp.dot(p_ref[...].astype(BF16), wp_ref[...], preferred_element_type=F32)
        out = h2 + pg * pp
        if final:
            out = _rms(out, fw_ref[...])
        o_ref[...] = out

    @pl.when(i == 0)
    def _():
        fetch_pos(0, 0)
        for rr in range(sub):
            def body(cc, carry, rr=rr):
                for k in range(2):
                    start_row(0, k, rr, cc)
                return carry
            lax.fori_loop(0, LANES, body, 0)

    nxt = jnp.minimum(i + 1, n - 1)
    for slot in range(2):
        @pl.when(i % 2 == slot)
        def _(slot=slot):
            fetch_pos(nxt, 1 - slot)
            wait_rows(slot)
            for rr in range(sub):
                for cc in range(LANES):
                    for k in range(2):
                        start_row(1 - slot, k, rr, cc)
            combine(slot)

    @pl.when(i == n - 1)
    def _():
        wait_rows(1 - (n - 1) % 2)


def moe_combine_ple(h, yb, pos_tiles, gate_cols, p, nw, wg, wp, fw, *, final):
    t, d = h.shape
    tq = ROUTE_TILE
    pd = p.shape[1]
    return pl.pallas_call(
        functools.partial(_combine_ple_kernel, final=final, n_tiles=t // tq),
        grid=(t // tq,),
        in_specs=[pl.BlockSpec(memory_space=pl.ANY),
                  pl.BlockSpec(memory_space=pl.ANY),
                  pl.BlockSpec((tq, d), lambda i: (i, 0)),
                  pl.BlockSpec((tq, 2), lambda i: (i, 0)),
                  pl.BlockSpec((tq, pd), lambda i: (i, 0)),
                  pl.BlockSpec((1, d), lambda i: (0, 0)),
                  pl.BlockSpec((d, d), lambda i: (0, 0)),
                  pl.BlockSpec((pd, d), lambda i: (0, 0)),
                  pl.BlockSpec((1, d), lambda i: (0, 0))],
        out_specs=pl.BlockSpec((tq, d), lambda i: (i, 0)),
        out_shape=jax.ShapeDtypeStruct((t, d), F32),
        scratch_shapes=[pltpu.SMEM((2, 2 * tq // LANES, LANES), I32),
                        pltpu.VMEM((2, 2, tq, d), F32),
                        pltpu.SemaphoreType.DMA, pltpu.SemaphoreType.DMA((2,))],
        compiler_params=_cparams("arbitrary"),
        name="moe_combine_ple",
    )(pos_tiles, yb, h, gate_cols, p, nw.reshape(1, d), wg, wp, fw.reshape(1, d))


def moe_ple_layer(h, p, norm_moe, w_group, b_group, w_expert, b_expert, w1, w3, w2,
                  norm_ple, wg, wp, fw, *, layer, final):
    t, d = h.shape
    eid_tiles, rank_tiles, gate, counts = moe_router(h, norm_moe, w_group, b_group, w_expert,
                                                     b_expert)
    padded = (counts + MOE_BLOCK - 1) // MOE_BLOCK * MOE_BLOCK
    pend = jnp.cumsum(padded)
    pstart = pend - padded
    na = 2 * t
    n_pad = (na + MOE_BLOCK - 1) // MOE_BLOCK * MOE_BLOCK + N_EXPERTS * MOE_BLOCK
    nb = n_pad // MOE_BLOCK
    blk_start = jnp.arange(nb, dtype=I32) * MOE_BLOCK
    blk_expert = jnp.minimum(jnp.sum(blk_start[:, None] >= pend[None, :], axis=-1),
                             N_EXPERTS - 1).astype(I32)
    n_used = (pend[-1:] // MOE_BLOCK).astype(I32)
    pos_tiles = moe_slots(pstart.astype(I32), eid_tiles, rank_tiles)
    xb = moe_dispatch(h, pos_tiles, (pstart + counts).astype(I32), pend.astype(I32), n_used, n_pad)
    yb = moe_experts(xb, norm_moe, w1, w3, w2, blk_expert + layer * N_EXPERTS, n_used)
    return moe_combine_ple(h, yb, pos_tiles, gate.T, p, norm_ple, wg.astype(BF16),
                           wp.astype(BF16), fw, final=final)


def _pad_cols(w, n):
    return jnp.pad(w, ((0, 0), (0, n - w.shape[1])))


def kernel(x, p, norm_mix, norm_moe, norm_ple, final_norm, m_in_w, m_conv_w, m_conv_b, m_dt_bias, m_A_log, m_D, m_norm_w, m_out_w, g_in_w, g_conv_w, g_dt_bias, g_A_log, g_norm_w, g_out_w, moe_w_group, moe_b_group, moe_w_expert, moe_b_expert, moe_w1, moe_w3, moe_w2, ple_w_proj, ple_w_gate):
    batch, seq, d = x.shape
    t = batch * seq
    depth = p.shape[0]
    pd = p.shape[-1]
    h = x.reshape(t, d)
    p2 = p.reshape(depth, t, pd)
    w1_all = moe_w1.reshape((-1,) + moe_w1.shape[2:])
    w3_all = moe_w3.reshape((-1,) + moe_w3.shape[2:])
    w2_all = moe_w2.reshape((-1,) + moe_w2.shape[2:])
    for i in range(depth):
        j = i // 2
        if i % 2 == 0:
            inner = m_out_w.shape[1]
            heads = m_dt_bias.shape[1]
            conv_dim = m_conv_w.shape[2]
            w_in = _pad_cols(m_in_w[j], inner + conv_dim + LANES).astype(BF16)
            proj = norm_matmul(h, norm_mix[i], w_in)
            y = ssd_mixer_core(proj, m_conv_w[j], m_conv_b[j], m_dt_bias[j], m_A_log[j], m_D[j],
                               m_norm_w[j], batch=batch, seq=seq, inner=inner, heads=heads)
            h = matmul_residual(y, m_out_w[j].astype(BF16), h)
        else:
            v_dim = g_out_w.shape[1]
            v_heads = g_dt_bias.shape[1]
            conv_dim = g_conv_w.shape[2]
            qk_dim = (conv_dim - v_dim) // 2
            w_in = _pad_cols(g_in_w[j], conv_dim + v_dim + LANES).astype(BF16)
            proj = norm_matmul(h, norm_mix[i], w_in)
            y = gdn_mixer_core(proj, g_conv_w[j], g_dt_bias[j], g_A_log[j], g_norm_w[j],
                               batch=batch, seq=seq, qk_dim=qk_dim, v_dim=v_dim, v_heads=v_heads)
            h = matmul_residual(y, g_out_w[j].astype(BF16), h)
        h = moe_ple_layer(h, p2[i], norm_moe[i], moe_w_group[i], moe_b_group[i], moe_w_expert[i],
                          moe_b_expert[i], w1_all, w3_all, w2_all, norm_ple[i],
                          ple_w_gate[i], ple_w_proj[i], final_norm, layer=i,
                          final=(i == depth - 1))
    return h.reshape(batch, seq, d)
```

```python
import functools
import math

import jax
import jax.numpy as jnp
from jax import lax
from jax.experimental import pallas as pl
from jax.experimental.pallas import tpu as pltpu

F32 = jnp.float32
BF16 = jnp.bfloat16
I32 = jnp.int32
EPS = 1e-6
HIGHEST = lax.Precision.HIGHEST

LANES = 128
SUBLANES = 8
VMEM_LIMIT = 56 * 1024 * 1024

CONV_K = 4
SSD_CHUNK = 128
SSD_HEADDIM = 64
SSD_STATE = 128
SSD_GROUPS = 4
GDN_CHUNK = 64
GDN_HEAD = 128
GDN_STACK = 256
GDN_SUBBLOCK = 16
GDN_BATCH_PER_STEP = 1
N_EGROUPS = 8
EXPERTS_PER_GROUP = 8
N_EXPERTS = N_EGROUPS * EXPERTS_PER_GROUP
MOE_BLOCK = 512
ROUTE_TILE = 512
DMA_UNROLL = 8


def _cparams(*sem):
    return pltpu.CompilerParams(dimension_semantics=sem, vmem_limit_bytes=VMEM_LIMIT)


def _silu(x):
    return x * jax.nn.sigmoid(x)


def _softplus(x):
    return jnp.maximum(x, 0.0) + jnp.log1p(jnp.exp(-jnp.abs(x)))


def _rms(x, w):
    return x * lax.rsqrt(jnp.mean(x * x, axis=-1, keepdims=True) + EPS) * w


def _dot(a, b):
    return jnp.dot(a.astype(BF16), b.astype(BF16), preferred_element_type=F32)


def _split(x):
    hi = x.astype(BF16)
    return hi, (x - hi.astype(F32)).astype(BF16)


def _dot_f32(a, b):
    return jnp.dot(a, b, preferred_element_type=F32, precision=HIGHEST)


def _dot_tn(a, b):
    return lax.dot_general(a.astype(BF16), b.astype(BF16), (((0,), (0,)), ((), ())),
                           preferred_element_type=F32)


def _dot_nt(a, b):
    return lax.dot_general(a.astype(BF16), b.astype(BF16), (((1,), (1,)), ((), ())),
                           preferred_element_type=F32)


def _tri(n, strict=False):
    r = lax.broadcasted_iota(I32, (n, n), 0)
    c = lax.broadcasted_iota(I32, (n, n), 1)
    return (r > c) if strict else (r >= c)


def _norm_matmul_kernel(x_ref, nw_ref, w_ref, o_ref, *, n_chunk):
    xb = _rms(x_ref[...], nw_ref[...]).astype(BF16)
    n = o_ref.shape[1]
    for c0 in range(0, n, n_chunk):
        c1 = min(c0 + n_chunk, n)
        o_ref[:, c0:c1] = jnp.dot(xb, w_ref[:, c0:c1], preferred_element_type=F32)


def norm_matmul(x, nw, w, *, tm=256, n_chunk=512):
    t, d = x.shape
    n = w.shape[1]
    return pl.pallas_call(
        functools.partial(_norm_matmul_kernel, n_chunk=n_chunk),
        grid=(t // tm,),
        in_specs=[pl.BlockSpec((tm, d), lambda i: (i, 0)),
                  pl.BlockSpec((1, d), lambda i: (0, 0)),
                  pl.BlockSpec((d, n), lambda i: (0, 0))],
        out_specs=pl.BlockSpec((tm, n), lambda i: (i, 0)),
        out_shape=jax.ShapeDtypeStruct((t, n), F32),
        compiler_params=_cparams("arbitrary"),
        name="norm_matmul",
    )(x, nw.reshape(1, d), w)


def _matmul_residual_kernel(y_ref, w_ref, r_ref, o_ref):
    o_ref[...] = r_ref[...] + jnp.dot(y_ref[...].astype(BF16), w_ref[...],
                                      preferred_element_type=F32)


def matmul_residual(y, w, res, *, tm=512):
    t, k = y.shape
    d = w.shape[1]
    return pl.pallas_call(
        _matmul_residual_kernel,
        grid=(t // tm,),
        in_specs=[pl.BlockSpec((tm, k), lambda i: (i, 0)),
                  pl.BlockSpec((k, d), lambda i: (0, 0)),
                  pl.BlockSpec((tm, d), lambda i: (i, 0))],
        out_specs=pl.BlockSpec((tm, d), lambda i: (i, 0)),
        out_shape=jax.ShapeDtypeStruct((t, d), F32),
        compiler_params=_cparams("arbitrary"),
        name="matmul_residual",
    )(y, w, res)


def _conv_silu(x_ref, stage_ref, w_ref, bias, first):
    q = x_ref.shape[0]

    @pl.when(first)
    def _():
        stage_ref[0:SUBLANES, :] = jnp.zeros((SUBLANES, stage_ref.shape[1]), F32)

    stage_ref[SUBLANES:SUBLANES + q, :] = x_ref[...]
    acc = stage_ref[SUBLANES:SUBLANES + q, :] * w_ref[CONV_K - 1:CONV_K, :]
    for j in range(CONV_K - 1):
        off = SUBLANES - (CONV_K - 1) + j
        acc = acc + stage_ref[off:off + q, :] * w_ref[j:j + 1, :]
    if bias is not None:
        acc = acc + bias
    stage_ref[0:SUBLANES, :] = stage_ref[q:q + SUBLANES, :]
    return _silu(acc)


def _ssd_kernel(z_ref, x_ref, b_ref, c_ref, dt_ref,
                cwx_ref, cwb_ref, cwc_ref, cbx_ref, cbb_ref, cbc_ref,
                dtb_ref, alog_ref, dfull_ref, nw_ref,
                o_ref,
                sx_ref, sb_ref, sc_ref, y_ref, xw_ref, st_ref):
    q = x_ref.shape[0]
    hpg = x_ref.shape[1] // (SSD_GROUPS * SSD_HEADDIM)
    first = pl.program_id(1) == 0

    @pl.when(first)
    def _():
        st_ref[...] = jnp.zeros(st_ref.shape, F32)

    xs = _conv_silu(x_ref, sx_ref, cwx_ref, cbx_ref[...], first)
    bm = _conv_silu(b_ref, sb_ref, cwb_ref, cbb_ref[...], first)
    cm = _conv_silu(c_ref, sc_ref, cwc_ref, cbc_ref[...], first)

    dt = _softplus(dt_ref[...] + dtb_ref[...])
    da = dt * (-jnp.exp(alog_ref[...]))
    causal = _tri(q)
    a = _dot_f32(causal.astype(F32), da)
    a_t = a.T
    ea_t = jnp.exp(a_t)
    a_last = a[q - 1:q, :]
    to_end_t = (jnp.exp(a_last - a) * dt).T
    ea_last = jnp.exp(a_last)
    dt_t = dt.T
    eye = (lax.broadcasted_iota(I32, (q, q), 0) == lax.broadcasted_iota(I32, (q, q), 1))
    lane_lo = lax.broadcasted_iota(I32, (q, 2 * SSD_HEADDIM), 1) < SSD_HEADDIM

    for g in range(SSD_GROUPS):
        bg = bm[:, g * SSD_STATE:(g + 1) * SSD_STATE]
        cg = cm[:, g * SSD_STATE:(g + 1) * SSD_STATE]
        cb = _dot_nt(cg, bg)
        gw = hpg * SSD_HEADDIM
        cs = _dot(cg, st_ref[g])
        for pr in range(hpg // 2):
            j0 = g * hpg + 2 * pr
            lo, hi = j0 * SSD_HEADDIM, (j0 + 2) * SSD_HEADDIM
            xp = xs[:, lo:hi]
            cp = cs[:, 2 * pr * SSD_HEADDIM:(2 * pr + 2) * SSD_HEADDIM]
            x_sel = [jnp.where(lane_lo, xp, 0.0), jnp.where(lane_lo, 0.0, xp)]
            c_sel = [jnp.where(lane_lo, cp, 0.0), jnp.where(lane_lo, 0.0, cp)]
            lhs, rhs, te = [], [], []
            for k in range(2):
                j = j0 + k
                diff = a[:, j:j + 1] - a_t[j:j + 1, :]
                seg = jnp.exp(jnp.where(causal, diff, -jnp.inf))
                lhs += [cb * seg * dt_t[j:j + 1, :], jnp.where(eye, ea_t[j:j + 1, :], 0.0)]
                rhs += [x_sel[k], c_sel[k]]
                te.append(jnp.where(eye, to_end_t[j:j + 1, :], 0.0))
            y_ref[:, lo:hi] = _dot(jnp.concatenate(lhs, axis=1), jnp.concatenate(rhs, axis=0))
            xw_ref[:, lo:hi] = _dot(jnp.concatenate(te, axis=1), jnp.concatenate(x_sel, axis=0))
        upd = _dot_tn(bg, xw_ref[:, g * gw:(g + 1) * gw])
        for jj in range(hpg):
            j = g * hpg + jj
            sl = slice(jj * SSD_HEADDIM, (jj + 1) * SSD_HEADDIM)
            st_ref[g, :, sl] = st_ref[g, :, sl] * ea_last[:, j:j + 1] + upd[:, sl]

    y = y_ref[...] + dfull_ref[...] * xs
    o_ref[...] = _rms(y * _silu(z_ref[...]), nw_ref[...]).astype(o_ref.dtype)


def ssd_mixer_core(proj, conv_w, conv_b, dt_bias, a_log, d_skip, norm_w, *, batch, seq,
                   inner, heads):
    t = proj.shape[0]
    q = SSD_CHUNK
    gn = SSD_GROUPS * SSD_STATE
    nc = seq // q
    cwx, cwb, cwc = conv_w[:, :inner], conv_w[:, inner:inner + gn], conv_w[:, inner + gn:]
    cbx = conv_b[:inner].reshape(1, inner)
    cbb = conv_b[inner:inner + gn].reshape(1, gn)
    cbc = conv_b[inner + gn:].reshape(1, gn)
    pad = LANES - heads
    dtb = jnp.pad(dt_bias, (0, pad)).reshape(1, LANES)
    alog = jnp.pad(a_log, (0, pad)).reshape(1, LANES)
    dfull = jnp.repeat(d_skip, SSD_HEADDIM).reshape(1, inner)
    nw = norm_w.reshape(1, inner)

    def row(i, c):
        return i * nc + c

    const = lambda shape: pl.BlockSpec(shape, lambda i, c: (0, 0))
    x_blk = inner // inner
    b_blk = (2 * inner) // gn
    c_blk = b_blk + 1
    dt_blk = (2 * inner + 2 * gn) // LANES
    return pl.pallas_call(
        _ssd_kernel,
        grid=(batch, nc),
        in_specs=[pl.BlockSpec((q, inner), lambda i, c: (row(i, c), 0)),
                  pl.BlockSpec((q, inner), lambda i, c: (row(i, c), x_blk)),
                  pl.BlockSpec((q, gn), lambda i, c: (row(i, c), b_blk)),
                  pl.BlockSpec((q, gn), lambda i, c: (row(i, c), c_blk)),
                  pl.BlockSpec((q, LANES), lambda i, c: (row(i, c), dt_blk)),
                  const((CONV_K, inner)), const((CONV_K, gn)), const((CONV_K, gn)),
                  const((1, inner)), const((1, gn)), const((1, gn)),
                  const((1, LANES)), const((1, LANES)), const((1, inner)), const((1, inner))],
        out_specs=pl.BlockSpec((q, inner), lambda i, c: (row(i, c), 0)),
        out_shape=jax.ShapeDtypeStruct((t, inner), BF16),
        scratch_shapes=[pltpu.VMEM((q + SUBLANES, inner), F32),
                        pltpu.VMEM((q + SUBLANES, gn), F32),
                        pltpu.VMEM((q + SUBLANES, gn), F32),
                        pltpu.VMEM((q, inner), F32),
                        pltpu.VMEM((q, inner), F32),
                        pltpu.VMEM((SSD_GROUPS, SSD_STATE, inner // SSD_GROUPS), F32)],
        compiler_params=_cparams("arbitrary", "arbitrary"),
        name="ssd_scan",
    )(proj, proj, proj, proj, proj, cwx, cwb, cwc, cbx, cbb, cbc, dtb, alog, dfull, nw)


def _l2n(x):
    return x * lax.rsqrt(jnp.sum(x * x, axis=-1, keepdims=True) + EPS)


def _pair_cols(x, i0, i1, half):
    c = x.shape[0]
    lane = lax.broadcasted_iota(I32, (c, 2 * half), 1)
    return jnp.where(lane < half, jnp.broadcast_to(x[:, i0:i0 + 1], (c, 2 * half)),
                     jnp.broadcast_to(x[:, i1:i1 + 1], (c, 2 * half)))


def _gdn_kernel(q_ref, k_ref, v_ref, z_ref, ba_ref,
                cwq_ref, cwk_ref, cwv_ref, dtb_ref, alog_ref, nw_ref,
                o_ref,
                sq_ref, sk_ref, sv_ref, s_ref, *, v_heads):
    nb, c = q_ref.shape[0], q_ref.shape[1]
    qk_heads = q_ref.shape[2] // GDN_HEAD
    rep = v_heads // qk_heads
    gsz = GDN_STACK // c
    n_grp = v_heads // gsz
    first = pl.program_id(1) == 0

    @pl.when(first)
    def _():
        s_ref[...] = jnp.zeros(s_ref.shape, F32)

    t_idx = lax.broadcasted_iota(I32, (c, GDN_STACK), 0)
    s_idx = lax.broadcasted_iota(I32, (c, GDN_STACK), 1) & (c - 1)
    causal = t_idx >= s_idx
    strict = t_idx > s_idx
    eye = t_idx == s_idx
    eye_f = eye.astype(F32)
    sb_shift = GDN_SUBBLOCK.bit_length() - 1
    same_blk = (t_idx >> sb_shift) == (s_idx >> sb_shift)
    n_col = GDN_STACK // LANES
    lane_lo = lax.broadcasted_iota(I32, (c, LANES), 1) < c

    def bdiag(parts):
        zero = jnp.zeros((), BF16)
        cols = lambda p: [p[:, k * LANES:(k + 1) * LANES] for k in range(n_col)]
        return tuple([jnp.concatenate([jnp.where(lane_lo, y, zero), jnp.where(lane_lo, zero, y)],
                                      axis=0) for y in cols(p)] for p in parts)

    def mm(lhs, rhs_bd):
        n = len(lhs) * c
        parts = [_split(x) for x in lhs]
        l_hi = jnp.concatenate([p[0] for p in parts], axis=0)
        l_both = jnp.concatenate([l_hi] + [p[1] for p in parts], axis=0)
        outs = []
        for k in range(n_col):
            sl = slice(k * LANES, (k + 1) * LANES)
            r = jnp.dot(l_both[:, sl], rhs_bd[0][k], preferred_element_type=F32)
            outs.append(r[:n] + r[n:] + jnp.dot(l_hi[:, sl], rhs_bd[1][k],
                                                preferred_element_type=F32))
        out = jnp.concatenate(outs, axis=1)
        return [out[i * c:(i + 1) * c] for i in range(len(lhs))]

    def neumann(p0s, n_sq):
        ps = list(p0s)
        ds = [eye_f + p for p in ps]
        for i in range(n_sq):
            for g in range(len(ps)):
                bd = bdiag(_split(ps[g]))
                if i == 0:
                    ps[g], = mm([ps[g]], bd)
                else:
                    ps[g], x = mm([ps[g], ds[g]], bd)
                    ds[g] = ds[g] + x
        if n_sq > 0:
            for g in range(len(ps)):
                x, = mm([ds[g]], bdiag(_split(ps[g])))
                ds[g] = ds[g] + x
        return ds

    def prepare(b):
        qc = _conv_silu(q_ref.at[b], sq_ref.at[b], cwq_ref, None, first)
        kc = _conv_silu(k_ref.at[b], sk_ref.at[b], cwk_ref, None, first)
        vc = _conv_silu(v_ref.at[b], sv_ref.at[b], cwv_ref, None, first)
        ba = ba_ref[b]
        beta = jax.nn.sigmoid(ba)
        gl = -jnp.exp(alog_ref[...]) * _softplus(ba + dtb_ref[...])
        gcs = _dot_f32(_tri(c).astype(F32), gl)
        g_last = gcs[c - 1:c, :]
        st = dict(vc=vc, eg_last=jnp.exp(g_last), qns=[], kns=[], kts=[], a_kks=[], a_qks=[],
                  gcols=[], brows=[], grows=[])
        kks, qks = [], []
        for hq in range(qk_heads):
            sl = slice(hq * GDN_HEAD, (hq + 1) * GDN_HEAD)
            qn = _l2n(qc[:, sl]) * (GDN_HEAD ** -0.5)
            kn = _l2n(kc[:, sl])
            k2 = jnp.concatenate([kn] * rep, axis=0)
            st["qns"].append(qn)
            st["kns"].append(kn)
            st["kts"].append(kn.T.astype(BF16))
            kks.append(_dot_nt(kn, k2))
            qks.append(_dot_nt(qn, k2))
        for grp in range(n_grp):
            h0 = grp * gsz
            hq0 = h0 // rep
            nq = gsz // rep
            kk = jnp.concatenate(kks[hq0:hq0 + nq], axis=1)
            qk = jnp.concatenate(qks[hq0:hq0 + nq], axis=1)
            gcol = jnp.concatenate(
                [_pair_cols(gcs, v_heads + h0 + 2 * i, v_heads + h0 + 2 * i + 1, c)
                 for i in range(gsz // 2)], axis=1)
            bcol = jnp.concatenate([_pair_cols(beta, h0 + 2 * i, h0 + 2 * i + 1, c)
                                    for i in range(gsz // 2)], axis=1)
            grow = jnp.sum(jnp.where(eye, gcol, 0.0), axis=0, keepdims=True)
            decay = jnp.exp(jnp.where(causal, gcol - grow, -jnp.inf))
            st["a_kks"].append(jnp.where(strict, bcol * kk * decay, 0.0))
            st["a_qks"].append(jnp.where(causal, qk * decay, 0.0))
            st["gcols"].append(gcol)
            st["grows"].append(grow)
            st["brows"].append(jnp.sum(jnp.where(eye, bcol, 0.0), axis=0, keepdims=True))
        return st

    def pair_blocks(b0, b1):
        z = jnp.zeros((c, GDN_HEAD), BF16)
        return jnp.concatenate([jnp.concatenate([b0.astype(BF16), z], axis=1),
                                jnp.concatenate([z, b1.astype(BF16)], axis=1)], axis=0)

    def heads(b, st, grp, tinv):
        h0 = grp * gsz
        gcol, grow, brow = st["gcols"][grp], st["grows"][grp], st["brows"][grp]
        hd = lambda x, i: x[:, i * GDN_HEAD:(i + 1) * GDN_HEAD]
        col = lambda x, k: x[:, k * LANES:(k + 1) * LANES]
        tb = tinv * brow
        tbe = tinv * (brow * jnp.exp(grow))
        us, ws_ = [], []
        for k in range(n_col):
            ha = h0 + 2 * k
            us.append(_dot(col(tb, k), pair_blocks(hd(st["vc"], ha), hd(st["vc"], ha + 1))))
            ws_.append(_dot(col(tbe, k), pair_blocks(st["kns"][ha // rep],
                                                      st["kns"][(ha + 1) // rep])))
        u = jnp.concatenate(us, axis=1)
        w = jnp.concatenate(ws_, axis=1)
        yield
        v_news, q_states = [], []
        for i in range(gsz):
            hq = (h0 + i) // rep
            ws = _dot(jnp.concatenate([hd(w, i), st["qns"][hq]], axis=0), s_ref[b, h0 + i])
            v_news.append(hd(u, i) - ws[:c])
            q_states.append(ws[c:])
        yield
        eg_diag =jnp.where(eye, jnp.exp(gcol), 0.0)
        eg_end_diag = jnp.where(eye, jnp.exp(gcol[c - 1:c, :] - gcol), 0.0)
        os_, vss = [], []
        for k in range(n_col):
            pv = pair_blocks(v_news[2 * k], v_news[2 * k + 1])
            pq = pair_blocks(q_states[2 * k], q_states[2 * k + 1])
            os_.append(_dot(jnp.concatenate([col(st["a_qks"][grp], k), col(eg_diag, k)], axis=1),
                            jnp.concatenate([pv, pq], axis=0)))
            vss.append(_dot(col(eg_end_diag, k), pv))
        o = jnp.concatenate(os_, axis=1)
        v_scaled = jnp.concatenate(vss, axis=1)
        yield
        for i in range(gsz):
            h = h0 + i
            gi = v_heads + h
            hs = slice(h * GDN_HEAD, (h + 1) * GDN_HEAD)
            s_ref[b, h] = (s_ref[b, h] * st["eg_last"][:, gi:gi + 1]
                           + _dot(st["kts"][h // rep], hd(v_scaled, i)))
            o_ref[b, :, hs] = (_rms(hd(o, i), nw_ref[...])
                               * _silu(z_ref[b, :, hs])).astype(o_ref.dtype)

    sts = [prepare(b) for b in range(nb)]

    a_all = [a for st in sts for a in st["a_kks"]]
    dms = neumann([jnp.where(same_blk, -a, 0.0) for a in a_all], sb_shift - 1)
    ms = [-mm([d], bdiag(_split(jnp.where(same_blk, 0.0, a))))[0] for d, a in zip(dms, a_all)]
    ws = neumann(ms, (c // GDN_SUBBLOCK).bit_length() - 2)
    tinvs = [mm([w], bdiag(_split(d)))[0] for w, d in zip(ws, dms)]

    gens = [heads(b, sts[b], grp, tinvs[b * n_grp + grp]) for grp in range(n_grp)
            for b in range(nb)]
    for _ in range(4):
        for gen in gens:
            next(gen, None)


def gdn_mixer_core(proj, conv_w, dt_bias, a_log, norm_w, *, batch, seq, qk_dim, v_dim, v_heads):
    t, n = proj.shape
    c = GDN_CHUNK
    nb = GDN_BATCH_PER_STEP
    cwq, cwk, cwv = conv_w[:, :qk_dim], conv_w[:, qk_dim:2 * qk_dim], conv_w[:, 2 * qk_dim:]
    dtb = jnp.pad(dt_bias, (v_heads, LANES - 2 * v_heads)).reshape(1, LANES)
    alog = jnp.pad(a_log, (v_heads, LANES - 2 * v_heads)).reshape(1, LANES)
    nw = norm_w.reshape(1, GDN_HEAD)
    proj3 = proj.reshape(batch, seq, n)

    const = lambda shape: pl.BlockSpec(shape, lambda i, cc: (0, 0))
    blk = lambda width, col: pl.BlockSpec((nb, c, width), lambda i, cc: (i, cc, col))
    v_blk = (2 * qk_dim) // v_dim
    z_blk = v_blk + 1
    ba_blk = (2 * qk_dim + 2 * v_dim) // LANES
    out = pl.pallas_call(
        functools.partial(_gdn_kernel, v_heads=v_heads),
        grid=(batch // nb, seq // c),
        in_specs=[blk(qk_dim, 0), blk(qk_dim, 1), blk(v_dim, v_blk), blk(v_dim, z_blk),
                  blk(LANES, ba_blk),
                  const((CONV_K, qk_dim)), const((CONV_K, qk_dim)), const((CONV_K, v_dim)),
                  const((1, LANES)), const((1, LANES)), const((1, GDN_HEAD))],
        out_specs=blk(v_dim, 0),
        out_shape=jax.ShapeDtypeStruct((batch, seq, v_dim), BF16),
        scratch_shapes=[pltpu.VMEM((nb, c + SUBLANES, qk_dim), F32),
                        pltpu.VMEM((nb, c + SUBLANES, qk_dim), F32),
                        pltpu.VMEM((nb, c + SUBLANES, v_dim), F32),
                        pltpu.VMEM((nb, v_heads, GDN_HEAD, GDN_HEAD), F32)],
        compiler_params=_cparams("arbitrary", "arbitrary"),
        name="gdn_scan",
    )(proj3, proj3, proj3, proj3, proj3, cwq, cwk, cwv, dtb, alog, nw)
    return out.reshape(t, v_dim)


def _first_index(mask, n):
    idx = lax.broadcasted_iota(I32, mask.shape, 0)
    return jnp.min(jnp.where(mask, idx, n), axis=0, keepdims=True)


def _router_kernel(h_ref, nw_ref, wr_ref, br_ref, upper_ref,
                   eid_ref, rank_ref, gate_ref, cnt_ref, base_ref):
    tq = h_ref.shape[0]
    i = pl.program_id(0)

    @pl.when(i == 0)
    def _():
        base_ref[...] = jnp.zeros(base_ref.shape, F32)

    xn = _rms(h_ref[...], nw_ref[...])
    x_hi, x_lo = _split(xn)
    w_hi, w_lo = wr_ref[0], wr_ref[1]
    logits = (_dot_nt(jnp.concatenate([w_hi, w_lo], axis=0), x_hi)
              + jnp.concatenate([_dot_nt(w_hi, x_lo), jnp.zeros((LANES, tq), F32)], axis=0))
    logits = logits[:LANES] + logits[LANES:] + br_ref[:, 0:1]
    gl = logits[0:N_EGROUPS, :]
    gmax = jnp.max(gl, axis=0, keepdims=True)
    g_sel = _first_index(gl == gmax, N_EGROUPS)
    p_sel = 1.0 / jnp.sum(jnp.exp(gl - gmax), axis=0, keepdims=True)
    e_in = logits[N_EGROUPS:N_EGROUPS + EXPERTS_PER_GROUP, :]
    for g in range(1, N_EGROUPS):
        lo = N_EGROUPS + g * EXPERTS_PER_GROUP
        e_in = jnp.where(g_sel == g, logits[lo:lo + EXPERTS_PER_GROUP, :], e_in)
    m1 = jnp.max(e_in, axis=0, keepdims=True)
    i1 = _first_index(e_in == m1, EXPERTS_PER_GROUP)
    sub = lax.broadcasted_iota(I32, e_in.shape, 0)
    rest = jnp.where(sub == i1, -jnp.inf, e_in)
    m2 = jnp.max(rest, axis=0, keepdims=True)
    i2 = _first_index(rest == m2, EXPERTS_PER_GROUP)
    e2 = jnp.exp(m2 - m1)
    denom = 1.0 + e2
    gate_ref[0:1, :] = (1.0 / denom) * p_sel
    gate_ref[1:2, :] = (e2 / denom) * p_sel
    eid0 = g_sel * EXPERTS_PER_GROUP + i1
    eid1 = g_sel * EXPERTS_PER_GROUP + i2

    erow = lax.broadcasted_iota(I32, (N_EXPERTS, tq), 0)
    oh0 = erow == eid0
    oh1 = erow == eid1
    oh0f = oh0.astype(F32)
    oh1f = oh1.astype(F32)
    cum0 = jnp.dot(oh0f.astype(BF16), upper_ref[...], preferred_element_type=F32)
    cum1 = jnp.dot(oh1f.astype(BF16), upper_ref[...], preferred_element_type=F32)
    base = base_ref[:, 0:1]
    tot0 = jnp.sum(oh0f, axis=1, keepdims=True)
    tot1 = jnp.sum(oh1f, axis=1, keepdims=True)
    r0 = jnp.sum(jnp.where(oh0, base + cum0, 0.0), axis=0, keepdims=True)
    r1 = jnp.sum(jnp.where(oh1, base + tot0 + cum1, 0.0), axis=0, keepdims=True)
    sub = tq // LANES
    for k, (e, r) in enumerate(((eid0, r0.astype(I32)), (eid1, r1.astype(I32)))):
        for rr in range(sub):
            row = k * sub + rr
            eid_ref[0, row:row + 1, :] = e[:, rr * LANES:(rr + 1) * LANES]
            rank_ref[0, row:row + 1, :] = r[:, rr * LANES:(rr + 1) * LANES]
    new_base = base + tot0 + tot1
    base_ref[...] = jnp.broadcast_to(new_base, base_ref.shape)
    cnt_ref[...] = jnp.broadcast_to(new_base, cnt_ref.shape).astype(I32)


def moe_router(h, nw, w_group, b_group, w_expert, b_expert):
    t, d = h.shape
    tq = ROUTE_TILE
    nr = N_EGROUPS + N_EXPERTS
    wr = jnp.pad(jnp.concatenate([w_group, w_expert], axis=1).T, ((0, LANES - nr), (0, 0)))
    br = jnp.pad(jnp.concatenate([b_group, b_expert]), (0, LANES - nr))
    br = jnp.broadcast_to(br[:, None], (LANES, LANES))
    upper = (jnp.arange(tq)[:, None] < jnp.arange(tq)[None, :]).astype(BF16)
    rows = 2 * tq // LANES
    tile_spec = pl.BlockSpec((1, rows, LANES), lambda i: (i, 0, 0))
    tile_shape = jax.ShapeDtypeStruct((t // tq, rows, LANES), I32)
    eid, rank, gate, cnt = pl.pallas_call(
        _router_kernel,
        grid=(t // tq,),
        in_specs=[pl.BlockSpec((tq, d), lambda i: (i, 0)),
                  pl.BlockSpec((1, d), lambda i: (0, 0)),
                  pl.BlockSpec((2, LANES, d), lambda i: (0, 0, 0)),
                  pl.BlockSpec((LANES, LANES), lambda i: (0, 0)),
                  pl.BlockSpec((tq, tq), lambda i: (0, 0))],
        out_specs=[tile_spec, tile_spec, pl.BlockSpec((2, tq), lambda i: (0, i)),
                   pl.BlockSpec((N_EXPERTS, LANES), lambda i: (0, 0))],
        out_shape=[tile_shape, tile_shape, jax.ShapeDtypeStruct((2, t), F32),
                   jax.ShapeDtypeStruct((N_EXPERTS, LANES), I32)],
        scratch_shapes=[pltpu.VMEM((N_EXPERTS, LANES), F32)],
        compiler_params=_cparams("arbitrary"),
        name="moe_router",
    )(h, nw.reshape(1, d), jnp.stack(_split(wr)), br, upper)
    return eid, rank, gate, cnt[:, 0]


def _slot_kernel(pstart_ref, eid_ref, rank_ref, pos_ref):
    eid = eid_ref[...]

    def body(e, acc):
        return jnp.where(eid == e, pstart_ref[e], acc)

    pos_ref[...] = rank_ref[...] + lax.fori_loop(0, N_EXPERTS, body, jnp.zeros(eid.shape, I32))


def moe_slots(pstart, eid_tiles, rank_tiles):
    nt, rows, lanes = eid_tiles.shape
    tiles_per_step = math.gcd(nt, 16)
    blk = pl.BlockSpec((tiles_per_step, rows, lanes), lambda i, ps: (i, 0, 0))
    return pl.pallas_call(
        _slot_kernel,
        grid_spec=pltpu.PrefetchScalarGridSpec(num_scalar_prefetch=1, grid=(nt // tiles_per_step,),
                                               in_specs=[blk, blk], out_specs=blk),
        out_shape=jax.ShapeDtypeStruct(eid_tiles.shape, I32),
        compiler_params=_cparams("arbitrary"),
        name="moe_slots",
    )(pstart, eid_tiles, rank_tiles)


def _row_copy(src, src_row, dst, dst_row, sem):
    return pltpu.make_async_copy(src.at[pl.ds(src_row, 1)], dst.at[pl.ds(dst_row, 1)], sem)


def _dispatch_kernel(lo_ref, hi_ref, nu_ref, pos_hbm, h_ref, xb_out, pos_smem, zero_ref, sem_idx,
                     sem_rows, sem_fill):
    tq = h_ref.shape[0]
    sub = tq // LANES
    i = pl.program_id(0)
    n_blocks = xb_out.shape[0] // MOE_BLOCK

    def zero_rows(start, size, wait):
        cp = pltpu.make_async_copy(zero_ref.at[pl.ds(0, size)], xb_out.at[pl.ds(start, size)],
                                   sem_fill)
        cp.wait() if wait else cp.start()

    def fill_padding(e, wait):
        lo, hi = lo_ref[e], hi_ref[e]
        lo8 = jnp.minimum((lo + (SUBLANES - 1)) & -SUBLANES, hi)
        for u in range(SUBLANES - 1):
            @pl.when(lo + u < lo8)
            def _(u=u):
                zero_rows(lo + u, 1, wait)
        cur = lo8
        size = MOE_BLOCK // 2
        while size >= SUBLANES:
            take = ((hi - lo8) & size) != 0

            @pl.when(take)
            def _(cur=cur, size=size):
                zero_rows(pl.multiple_of(cur, SUBLANES), size, wait)
            cur = cur + jnp.where(take, size, 0)
            size //= 2

    def fill_tail(blk, wait):
        zero_rows(pl.multiple_of(blk * MOE_BLOCK, MOE_BLOCK), MOE_BLOCK, wait)

    @pl.when(i == 0)
    def _():
        zero_ref[...] = jnp.zeros(zero_ref.shape, F32)
        for wait in (False, True):
            def per_expert(e, carry, wait=wait):
                fill_padding(e, wait)
                return carry

            def per_block(blk, carry, wait=wait):
                fill_tail(blk, wait)
                return carry

            lax.fori_loop(0, N_EXPERTS, per_expert, 0)
            lax.fori_loop(nu_ref[0], n_blocks, per_block, 0)

    n = pl.num_programs(0)
    slot = i % 2

    def pos_copy(tile, slot):
        return pltpu.make_async_copy(pos_hbm.at[tile], pos_smem.at[slot], sem_idx)

    @pl.when(i == 0)
    def _():
        pos_copy(0, 0).start()

    pos_copy(i, slot).wait()
    pos_copy(jnp.minimum(i + 1, n - 1), 1 - slot).start()
    for rr in range(sub):
        def body(cu, carry, rr=rr):
            for u in range(DMA_UNROLL):
                cc = cu * DMA_UNROLL + u
                r = rr * LANES + cc
                _row_copy(h_ref, r, xb_out, pos_smem[slot, rr, cc], sem_rows).start()
                _row_copy(h_ref, r, xb_out, pos_smem[slot, sub + rr, cc], sem_rows).start()
            return carry
        lax.fori_loop(0, LANES // DMA_UNROLL, body, 0)
    pltpu.make_async_copy(h_ref, xb_out.at[pl.ds(0, tq)], sem_rows).wait()
    pltpu.make_async_copy(h_ref, xb_out.at[pl.ds(0, tq)], sem_rows).wait()

    @pl.when(i == n - 1)
    def _():
        pos_copy(i, 1 - slot).wait()


def moe_dispatch(h, pos_tiles, pad_lo, pad_hi, n_used, n_pad):
    t, d = h.shape
    tq = ROUTE_TILE
    grid_spec = pltpu.PrefetchScalarGridSpec(
        num_scalar_prefetch=3,
        grid=(t // tq,),
        in_specs=[pl.BlockSpec(memory_space=pl.ANY),
                  pl.BlockSpec((tq, d), lambda i, lo, hi, nu: (i, 0))],
        out_specs=pl.BlockSpec(memory_space=pl.ANY),
        scratch_shapes=[pltpu.SMEM((2, 2 * tq // LANES, LANES), I32),
                        pltpu.VMEM((MOE_BLOCK, d), F32),
                        pltpu.SemaphoreType.DMA, pltpu.SemaphoreType.DMA,
                        pltpu.SemaphoreType.DMA],
    )
    return pl.pallas_call(
        _dispatch_kernel,
        grid_spec=grid_spec,
        out_shape=jax.ShapeDtypeStruct((n_pad, d), F32),
        compiler_params=_cparams("arbitrary"),
        name="moe_dispatch",
    )(pad_lo, pad_hi, n_used, pos_tiles, h)


def _expert_kernel(be_ref, nu_ref, x_ref, nw_ref, w1_ref, w3_ref, w2_ref, o_ref,
                   w1b_ref, w3b_ref, w2b_ref):
    b = pl.program_id(0)
    used = b < nu_ref[0]
    new_expert = (b == 0) | (be_ref[b] != be_ref[jnp.maximum(b - 1, 0)])

    @pl.when(used & new_expert)
    def _():
        w1b_ref[...] = w1_ref[0].astype(BF16)
        w3b_ref[...] = w3_ref[0].astype(BF16)
        w2b_ref[...] = w2_ref[0].astype(BF16)

    @pl.when(used)
    def _():
        xb = _rms(x_ref[...], nw_ref[...]).astype(BF16)
        h1 = jnp.dot(xb, w1b_ref[...], preferred_element_type=F32)
        h3 = jnp.dot(xb, w3b_ref[...], preferred_element_type=F32)
        hid = (_silu(h1) * h3).astype(BF16)
        o_ref[...] = jnp.dot(hid, w2b_ref[...], preferred_element_type=F32)

    @pl.when(b >= nu_ref[0])
    def _():
        o_ref[...] = jnp.zeros(o_ref.shape, F32)


def moe_experts(xb, nw, w1, w3, w2, blk_expert, n_used):
    n_pad, d = xb.shape
    de = w1.shape[2]
    nb = n_pad // MOE_BLOCK
    grid_spec = pltpu.PrefetchScalarGridSpec(
        num_scalar_prefetch=2,
        grid=(nb,),
        in_specs=[pl.BlockSpec((MOE_BLOCK, d), lambda b, be, nu: (jnp.minimum(b, nu[0] - 1), 0)),
                  pl.BlockSpec((1, d), lambda b, be, nu: (0, 0)),
                  pl.BlockSpec((1, d, de), lambda b, be, nu: (be[b], 0, 0)),
                  pl.BlockSpec((1, d, de), lambda b, be, nu: (be[b], 0, 0)),
                  pl.BlockSpec((1, de, d), lambda b, be, nu: (be[b], 0, 0))],
        out_specs=pl.BlockSpec((MOE_BLOCK, d), lambda b, be, nu: (b, 0)),
        scratch_shapes=[pltpu.VMEM((d, de), BF16), pltpu.VMEM((d, de), BF16),
                        pltpu.VMEM((de, d), BF16)],
    )
    return pl.pallas_call(
        _expert_kernel,
        grid_spec=grid_spec,
        out_shape=jax.ShapeDtypeStruct((n_pad, d), F32),
        compiler_params=_cparams("arbitrary"),
        name="moe_experts",
    )(blk_expert, n_used, xb, nw.reshape(1, d), w1, w3, w2)


def _combine_ple_kernel(pos_hbm, yb_hbm, h_ref, gate_ref, p_ref, nw_ref, wg_ref, wp_ref, fw_ref,
                        o_ref, pos_smem, buf_ref, sem_idx, sem_rows, *, final, n_tiles):
    tq = h_ref.shape[0]
    sub = tq // LANES
    i = pl.program_id(0)
    n = n_tiles

    def pos_copy(tile, slot):
        return pltpu.make_async_copy(pos_hbm.at[tile], pos_smem.at[slot], sem_idx)

    def start_row(slot, k, rr, cc):
        _row_copy(yb_hbm, pos_smem[slot, k * sub + rr, cc], buf_ref.at[slot, k], rr * LANES + cc,
                  sem_rows.at[slot]).start()

    def wait_rows(slot):
        for k in range(2):
            pltpu.make_async_copy(yb_hbm.at[pl.ds(0, tq)], buf_ref.at[slot, k],
                                  sem_rows.at[slot]).wait()

    def combine(slot):
        gate = gate_ref[...]
        h2 = h_ref[...] + gate[:, 0:1] * buf_ref[slot, 0] + gate[:, 1:2] * buf_ref[slot, 1]
        hn = _rms(h2, nw_ref[...]).astype(BF16)
        pg = jax.nn.sigmoid(jnp.dot(hn, wg_ref[...], preferred_element_type=F32))
        pp = jnp.dot(p_ref[...].astype(BF16), wp_ref[...], preferred_element_type=F32)
        out = h2 + pg * pp
        if final:
            out = _rms(out, fw_ref[...])
        o_ref[...] = out

    @pl.when(i == 0)
    def _():
        first = pos_copy(0, 0)
        first.start()
        first.wait()
        for rr in range(sub):
            def body(cc, carry, rr=rr):
                for k in range(2):
                    start_row(0, k, rr, cc)
                return carry
            lax.fori_loop(0, LANES, body, 0)
        pos_copy(min(1, n - 1), 1).start()

    nxt = jnp.minimum(i + 1, n - 1)
    for slot in range(2):
        @pl.when(i % 2 == slot)
        def _(slot=slot):
            pos_copy(nxt, 1 - slot).wait()
            wait_rows(slot)
            pos_copy(jnp.minimum(i + 2, n - 1), slot).start()
            for rr in range(sub):
                for cc in range(LANES):
                    for k in range(2):
                        start_row(1 - slot, k, rr, cc)
            combine(slot)

    @pl.when(i == n - 1)
    def _():
        wait_rows(1 - (n - 1) % 2)
        pos_copy(i, (n - 1) % 2).wait()


def moe_combine_ple(h, yb, pos_tiles, gate_cols, p, nw, wg, wp, fw, *, final):
    t, d = h.shape
    tq = ROUTE_TILE
    pd = p.shape[1]
    return pl.pallas_call(
        functools.partial(_combine_ple_kernel, final=final, n_tiles=t // tq),
        grid=(t // tq,),
        in_specs=[pl.BlockSpec(memory_space=pl.ANY),
                  pl.BlockSpec(memory_space=pl.ANY),
                  pl.BlockSpec((tq, d), lambda i: (i, 0)),
                  pl.BlockSpec((tq, 2), lambda i: (i, 0)),
                  pl.BlockSpec((tq, pd), lambda i: (i, 0)),
                  pl.BlockSpec((1, d), lambda i: (0, 0)),
                  pl.BlockSpec((d, d), lambda i: (0, 0)),
                  pl.BlockSpec((pd, d), lambda i: (0, 0)),
                  pl.BlockSpec((1, d), lambda i: (0, 0))],
        out_specs=pl.BlockSpec((tq, d), lambda i: (i, 0)),
        out_shape=jax.ShapeDtypeStruct((t, d), F32),
        scratch_shapes=[pltpu.SMEM((2, 2 * tq // LANES, LANES), I32),
                        pltpu.VMEM((2, 2, tq, d), F32),
                        pltpu.SemaphoreType.DMA, pltpu.SemaphoreType.DMA((2,))],
        compiler_params=_cparams("arbitrary"),
        name="moe_combine_ple",
    )(pos_tiles, yb, h, gate_cols, p, nw.reshape(1, d), wg, wp, fw.reshape(1, d))


def moe_ple_layer(h, p, norm_moe, w_group, b_group, w_expert, b_expert, w1, w3, w2,
                  norm_ple, wg, wp, fw, *, layer, final):
    t, d = h.shape
    eid_tiles, rank_tiles, gate, counts = moe_router(h, norm_moe, w_group, b_group, w_expert,
                                                     b_expert)
    padded = (counts + MOE_BLOCK - 1) // MOE_BLOCK * MOE_BLOCK
    pend = jnp.cumsum(padded)
    pstart = pend - padded
    na = 2 * t
    n_pad = (na + MOE_BLOCK - 1) // MOE_BLOCK * MOE_BLOCK + N_EXPERTS * MOE_BLOCK
    nb = n_pad // MOE_BLOCK
    blk_start = jnp.arange(nb, dtype=I32) * MOE_BLOCK
    blk_expert = jnp.minimum(jnp.sum(blk_start[:, None] >= pend[None, :], axis=-1),
                             N_EXPERTS - 1).astype(I32)
    n_used = (pend[-1:] // MOE_BLOCK).astype(I32)
    pos_tiles = moe_slots(pstart.astype(I32), eid_tiles, rank_tiles)
    xb = moe_dispatch(h, pos_tiles, (pstart + counts).astype(I32), pend.astype(I32), n_used, n_pad)
    yb = moe_experts(xb, norm_moe, w1, w3, w2, blk_expert + layer * N_EXPERTS, n_used)
    return moe_combine_ple(h, yb, pos_tiles, gate.T, p, norm_ple, wg.astype(BF16),
                           wp.astype(BF16), fw, final=final)


def _pad_cols(w, n):
    return jnp.pad(w, ((0, 0), (0, n - w.shape[1])))


def kernel(x, p, norm_mix, norm_moe, norm_ple, final_norm, m_in_w, m_conv_w, m_conv_b, m_dt_bias, m_A_log, m_D, m_norm_w, m_out_w, g_in_w, g_conv_w, g_dt_bias, g_A_log, g_norm_w, g_out_w, moe_w_group, moe_b_group, moe_w_expert, moe_b_expert, moe_w1, moe_w3, moe_w2, ple_w_proj, ple_w_gate):
    batch, seq, d = x.shape
    t = batch * seq
    depth = p.shape[0]
    pd = p.shape[-1]
    h = x.reshape(t, d)
    p2 = p.reshape(depth, t, pd)
    w1_all = moe_w1.reshape((-1,) + moe_w1.shape[2:])
    w3_all = moe_w3.reshape((-1,) + moe_w3.shape[2:])
    w2_all = moe_w2.reshape((-1,) + moe_w2.shape[2:])
    for i in range(depth):
        j = i // 2
        if i % 2 == 0:
            inner = m_out_w.shape[1]
            heads = m_dt_bias.shape[1]
            conv_dim = m_conv_w.shape[2]
            w_in = _pad_cols(m_in_w[j], inner + conv_dim + LANES).astype(BF16)
            proj = norm_matmul(h, norm_mix[i], w_in)
            y = ssd_mixer_core(proj, m_conv_w[j], m_conv_b[j], m_dt_bias[j], m_A_log[j], m_D[j],
                               m_norm_w[j], batch=batch, seq=seq, inner=inner, heads=heads)
            h = matmul_residual(y, m_out_w[j].astype(BF16), h)
        else:
            v_dim = g_out_w.shape[1]
            v_heads = g_dt_bias.shape[1]
            conv_dim = g_conv_w.shape[2]
            qk_dim = (conv_dim - v_dim) // 2
            w_in = _pad_cols(g_in_w[j], conv_dim + v_dim + LANES).astype(BF16)
            proj = norm_matmul(h, norm_mix[i], w_in)
            y = gdn_mixer_core(proj, g_conv_w[j], g_dt_bias[j], g_A_log[j], g_norm_w[j],
                               batch=batch, seq=seq, qk_dim=qk_dim, v_dim=v_dim, v_heads=v_heads)
            h = matmul_residual(y, g_out_w[j].astype(BF16), h)
        h = moe_ple_layer(h, p2[i], norm_moe[i], moe_w_group[i], moe_b_group[i], moe_w_expert[i],
                          moe_b_expert[i], w1_all, w3_all, w2_all, norm_ple[i],
                          ple_w_gate[i], ple_w_proj[i], final_norm, layer=i,
                          final=(i == depth - 1))
    return h.reshape(batch, seq, d)
```

```python
import functools
import math

import jax
import jax.numpy as jnp
from jax import lax
from jax.experimental import pallas as pl
from jax.experimental.pallas import tpu as pltpu

F32 = jnp.float32
BF16 = jnp.bfloat16
I32 = jnp.int32
EPS = 1e-6
HIGHEST = lax.Precision.HIGHEST

LANES = 128
SUBLANES = 8
VMEM_LIMIT = 56 * 1024 * 1024

CONV_K = 4
SSD_CHUNK = 128
SSD_HEADDIM = 64
SSD_STATE = 128
SSD_GROUPS = 4
GDN_CHUNK = 64
GDN_HEAD = 128
GDN_STACK = 256
GDN_SUBBLOCK = 16
GDN_BATCH_PER_STEP = 1
N_EGROUPS = 8
EXPERTS_PER_GROUP = 8
N_EXPERTS = N_EGROUPS * EXPERTS_PER_GROUP
MOE_BLOCK = 512
ROUTE_TILE = 512


def _cparams(*sem):
    return pltpu.CompilerParams(dimension_semantics=sem, vmem_limit_bytes=VMEM_LIMIT)


def _silu(x):
    return x * jax.nn.sigmoid(x)


def _softplus(x):
    return jnp.maximum(x, 0.0) + jnp.log1p(jnp.exp(-jnp.abs(x)))


def _rms(x, w):
    return x * lax.rsqrt(jnp.mean(x * x, axis=-1, keepdims=True) + EPS) * w


def _dot(a, b):
    return jnp.dot(a.astype(BF16), b.astype(BF16), preferred_element_type=F32)


def _split(x):
    hi = x.astype(BF16)
    return hi, (x - hi.astype(F32)).astype(BF16)


def _dot_f32(a, b):
    return jnp.dot(a, b, preferred_element_type=F32, precision=HIGHEST)


def _dot_tn(a, b):
    return lax.dot_general(a.astype(BF16), b.astype(BF16), (((0,), (0,)), ((), ())),
                           preferred_element_type=F32)


def _dot_nt(a, b):
    return lax.dot_general(a.astype(BF16), b.astype(BF16), (((1,), (1,)), ((), ())),
                           preferred_element_type=F32)


def _tri(n, strict=False):
    r = lax.broadcasted_iota(I32, (n, n), 0)
    c = lax.broadcasted_iota(I32, (n, n), 1)
    return (r > c) if strict else (r >= c)


def _norm_matmul_kernel(x_ref, nw_ref, w_ref, o_ref, *, n_chunk):
    xb = _rms(x_ref[...], nw_ref[...]).astype(BF16)
    n = o_ref.shape[1]
    for c0 in range(0, n, n_chunk):
        c1 = min(c0 + n_chunk, n)
        o_ref[:, c0:c1] = jnp.dot(xb, w_ref[:, c0:c1], preferred_element_type=F32)


def norm_matmul(x, nw, w, *, tm=256, n_chunk=512):
    t, d = x.shape
    n = w.shape[1]
    return pl.pallas_call(
        functools.partial(_norm_matmul_kernel, n_chunk=n_chunk),
        grid=(t // tm,),
        in_specs=[pl.BlockSpec((tm, d), lambda i: (i, 0)),
                  pl.BlockSpec((1, d), lambda i: (0, 0)),
                  pl.BlockSpec((d, n), lambda i: (0, 0))],
        out_specs=pl.BlockSpec((tm, n), lambda i: (i, 0)),
        out_shape=jax.ShapeDtypeStruct((t, n), F32),
        compiler_params=_cparams("arbitrary"),
        name="norm_matmul",
    )(x, nw.reshape(1, d), w)


def _matmul_residual_kernel(y_ref, w_ref, r_ref, o_ref):
    o_ref[...] = r_ref[...] + jnp.dot(y_ref[...].astype(BF16), w_ref[...],
                                      preferred_element_type=F32)


def matmul_residual(y, w, res, *, tm=512):
    t, k = y.shape
    d = w.shape[1]
    return pl.pallas_call(
        _matmul_residual_kernel,
        grid=(t // tm,),
        in_specs=[pl.BlockSpec((tm, k), lambda i: (i, 0)),
                  pl.BlockSpec((k, d), lambda i: (0, 0)),
                  pl.BlockSpec((tm, d), lambda i: (i, 0))],
        out_specs=pl.BlockSpec((tm, d), lambda i: (i, 0)),
        out_shape=jax.ShapeDtypeStruct((t, d), F32),
        compiler_params=_cparams("arbitrary"),
        name="matmul_residual",
    )(y, w, res)


def _conv_silu(x_ref, stage_ref, w_ref, bias, first):
    q = x_ref.shape[0]

    @pl.when(first)
    def _():
        stage_ref[0:SUBLANES, :] = jnp.zeros((SUBLANES, stage_ref.shape[1]), F32)

    stage_ref[SUBLANES:SUBLANES + q, :] = x_ref[...]
    acc = stage_ref[SUBLANES:SUBLANES + q, :] * w_ref[CONV_K - 1:CONV_K, :]
    for j in range(CONV_K - 1):
        off = SUBLANES - (CONV_K - 1) + j
        acc = acc + stage_ref[off:off + q, :] * w_ref[j:j + 1, :]
    if bias is not None:
        acc = acc + bias
    stage_ref[0:SUBLANES, :] = stage_ref[q:q + SUBLANES, :]
    return _silu(acc)


def _ssd_kernel(z_ref, x_ref, b_ref, c_ref, dt_ref,
                cwx_ref, cwb_ref, cwc_ref, cbx_ref, cbb_ref, cbc_ref,
                dtb_ref, alog_ref, dfull_ref, nw_ref,
                o_ref,
                sx_ref, sb_ref, sc_ref, y_ref, xw_ref, st_ref):
    q = x_ref.shape[0]
    hpg = x_ref.shape[1] // (SSD_GROUPS * SSD_HEADDIM)
    first = pl.program_id(1) == 0

    @pl.when(first)
    def _():
        st_ref[...] = jnp.zeros(st_ref.shape, F32)

    xs = _conv_silu(x_ref, sx_ref, cwx_ref, cbx_ref[...], first)
    bm = _conv_silu(b_ref, sb_ref, cwb_ref, cbb_ref[...], first)
    cm = _conv_silu(c_ref, sc_ref, cwc_ref, cbc_ref[...], first)

    dt = _softplus(dt_ref[...] + dtb_ref[...])
    da = dt * (-jnp.exp(alog_ref[...]))
    causal = _tri(q)
    a = _dot_f32(causal.astype(F32), da)
    a_t = a.T
    ea_t = jnp.exp(a_t)
    a_last = a[q - 1:q, :]
    to_end_t = (jnp.exp(a_last - a) * dt).T
    ea_last = jnp.exp(a_last)
    dt_t = dt.T
    eye = (lax.broadcasted_iota(I32, (q, q), 0) == lax.broadcasted_iota(I32, (q, q), 1))
    lane_lo = lax.broadcasted_iota(I32, (q, 2 * SSD_HEADDIM), 1) < SSD_HEADDIM

    for g in range(SSD_GROUPS):
        bg = bm[:, g * SSD_STATE:(g + 1) * SSD_STATE]
        cg = cm[:, g * SSD_STATE:(g + 1) * SSD_STATE]
        cb = _dot_nt(cg, bg)
        gw = hpg * SSD_HEADDIM
        cs = _dot(cg, st_ref[g])
        for pr in range(hpg // 2):
            j0 = g * hpg + 2 * pr
            lo, hi = j0 * SSD_HEADDIM, (j0 + 2) * SSD_HEADDIM
            xp = xs[:, lo:hi]
            cp = cs[:, 2 * pr * SSD_HEADDIM:(2 * pr + 2) * SSD_HEADDIM]
            x_sel = [jnp.where(lane_lo, xp, 0.0), jnp.where(lane_lo, 0.0, xp)]
            c_sel = [jnp.where(lane_lo, cp, 0.0), jnp.where(lane_lo, 0.0, cp)]
            lhs, rhs, te = [], [], []
            for k in range(2):
                j = j0 + k
                diff = a[:, j:j + 1] - a_t[j:j + 1, :]
                seg = jnp.exp(jnp.where(causal, diff, -jnp.inf))
                lhs += [cb * seg * dt_t[j:j + 1, :], jnp.where(eye, ea_t[j:j + 1, :], 0.0)]
                rhs += [x_sel[k], c_sel[k]]
                te.append(jnp.where(eye, to_end_t[j:j + 1, :], 0.0))
            y_ref[:, lo:hi] = _dot(jnp.concatenate(lhs, axis=1), jnp.concatenate(rhs, axis=0))
            xw_ref[:, lo:hi] = _dot(jnp.concatenate(te, axis=1), jnp.concatenate(x_sel, axis=0))
        upd = _dot_tn(bg, xw_ref[:, g * gw:(g + 1) * gw])
        for jj in range(hpg):
            j = g * hpg + jj
            sl = slice(jj * SSD_HEADDIM, (jj + 1) * SSD_HEADDIM)
            st_ref[g, :, sl] = st_ref[g, :, sl] * ea_last[:, j:j + 1] + upd[:, sl]

    y = y_ref[...] + dfull_ref[...] * xs
    o_ref[...] = _rms(y * _silu(z_ref[...]), nw_ref[...]).astype(o_ref.dtype)


def ssd_mixer_core(proj, conv_w, conv_b, dt_bias, a_log, d_skip, norm_w, *, batch, seq,
                   inner, heads):
    t = proj.shape[0]
    q = SSD_CHUNK
    gn = SSD_GROUPS * SSD_STATE
    nc = seq // q
    cwx, cwb, cwc = conv_w[:, :inner], conv_w[:, inner:inner + gn], conv_w[:, inner + gn:]
    cbx = conv_b[:inner].reshape(1, inner)
    cbb = conv_b[inner:inner + gn].reshape(1, gn)
    cbc = conv_b[inner + gn:].reshape(1, gn)
    pad = LANES - heads
    dtb = jnp.pad(dt_bias, (0, pad)).reshape(1, LANES)
    alog = jnp.pad(a_log, (0, pad)).reshape(1, LANES)
    dfull = jnp.repeat(d_skip, SSD_HEADDIM).reshape(1, inner)
    nw = norm_w.reshape(1, inner)

    def row(i, c):
        return i * nc + c

    const = lambda shape: pl.BlockSpec(shape, lambda i, c: (0, 0))
    x_blk = inner // inner
    b_blk = (2 * inner) // gn
    c_blk = b_blk + 1
    dt_blk = (2 * inner + 2 * gn) // LANES
    return pl.pallas_call(
        _ssd_kernel,
        grid=(batch, nc),
        in_specs=[pl.BlockSpec((q, inner), lambda i, c: (row(i, c), 0)),
                  pl.BlockSpec((q, inner), lambda i, c: (row(i, c), x_blk)),
                  pl.BlockSpec((q, gn), lambda i, c: (row(i, c), b_blk)),
                  pl.BlockSpec((q, gn), lambda i, c: (row(i, c), c_blk)),
                  pl.BlockSpec((q, LANES), lambda i, c: (row(i, c), dt_blk)),
                  const((CONV_K, inner)), const((CONV_K, gn)), const((CONV_K, gn)),
                  const((1, inner)), const((1, gn)), const((1, gn)),
                  const((1, LANES)), const((1, LANES)), const((1, inner)), const((1, inner))],
        out_specs=pl.BlockSpec((q, inner), lambda i, c: (row(i, c), 0)),
        out_shape=jax.ShapeDtypeStruct((t, inner), BF16),
        scratch_shapes=[pltpu.VMEM((q + SUBLANES, inner), F32),
                        pltpu.VMEM((q + SUBLANES, gn), F32),
                        pltpu.VMEM((q + SUBLANES, gn), F32),
                        pltpu.VMEM((q, inner), F32),
                        pltpu.VMEM((q, inner), F32),
                        pltpu.VMEM((SSD_GROUPS, SSD_STATE, inner // SSD_GROUPS), F32)],
        compiler_params=_cparams("arbitrary", "arbitrary"),
        name="ssd_scan",
    )(proj, proj, proj, proj, proj, cwx, cwb, cwc, cbx, cbb, cbc, dtb, alog, dfull, nw)


def _l2n(x):
    return x * lax.rsqrt(jnp.sum(x * x, axis=-1, keepdims=True) + EPS)


def _pair_cols(x, i0, i1, half):
    c = x.shape[0]
    lane = lax.broadcasted_iota(I32, (c, 2 * half), 1)
    return jnp.where(lane < half, jnp.broadcast_to(x[:, i0:i0 + 1], (c, 2 * half)),
                     jnp.broadcast_to(x[:, i1:i1 + 1], (c, 2 * half)))


def _gdn_kernel(q_ref, k_ref, v_ref, z_ref, ba_ref,
                cwq_ref, cwk_ref, cwv_ref, dtb_ref, alog_ref, nw_ref,
                o_ref,
                sq_ref, sk_ref, sv_ref, s_ref, *, v_heads):
    nb, c = q_ref.shape[0], q_ref.shape[1]
    qk_heads = q_ref.shape[2] // GDN_HEAD
    rep = v_heads // qk_heads
    gsz = GDN_STACK // c
    n_grp = v_heads // gsz
    first = pl.program_id(1) == 0

    @pl.when(first)
    def _():
        s_ref[...] = jnp.zeros(s_ref.shape, F32)

    t_idx = lax.broadcasted_iota(I32, (c, GDN_STACK), 0)
    s_idx = lax.broadcasted_iota(I32, (c, GDN_STACK), 1) & (c - 1)
    causal = t_idx >= s_idx
    strict = t_idx > s_idx
    eye = t_idx == s_idx
    eye_f = eye.astype(F32)
    sb_shift = GDN_SUBBLOCK.bit_length() - 1
    same_blk = (t_idx >> sb_shift) == (s_idx >> sb_shift)
    n_col = GDN_STACK // LANES
    lane_lo = lax.broadcasted_iota(I32, (c, LANES), 1) < c

    def bdiag(parts):
        zero = jnp.zeros((), BF16)
        cols = lambda p: [p[:, k * LANES:(k + 1) * LANES] for k in range(n_col)]
        return tuple([jnp.concatenate([jnp.where(lane_lo, y, zero), jnp.where(lane_lo, zero, y)],
                                      axis=0) for y in cols(p)] for p in parts)

    def mm(lhs, rhs_bd):
        n = len(lhs) * c
        parts = [_split(x) for x in lhs]
        l_hi = jnp.concatenate([p[0] for p in parts], axis=0)
        l_both = jnp.concatenate([l_hi] + [p[1] for p in parts], axis=0)
        outs = []
        for k in range(n_col):
            sl = slice(k * LANES, (k + 1) * LANES)
            r = jnp.dot(l_both[:, sl], rhs_bd[0][k], preferred_element_type=F32)
            outs.append(r[:n] + r[n:] + jnp.dot(l_hi[:, sl], rhs_bd[1][k],
                                                preferred_element_type=F32))
        out = jnp.concatenate(outs, axis=1)
        return [out[i * c:(i + 1) * c] for i in range(len(lhs))]

    def neumann(p0s, n_sq):
        ps = list(p0s)
        ds = [eye_f + p for p in ps]
        for i in range(n_sq):
            for g in range(len(ps)):
                bd = bdiag(_split(ps[g]))
                if i == 0:
                    ps[g], = mm([ps[g]], bd)
                else:
                    ps[g], x = mm([ps[g], ds[g]], bd)
                    ds[g] = ds[g] + x
        if n_sq > 0:
            for g in range(len(ps)):
                x, = mm([ds[g]], bdiag(_split(ps[g])))
                ds[g] = ds[g] + x
        return ds

    def prepare(b):
        qc = _conv_silu(q_ref.at[b], sq_ref.at[b], cwq_ref, None, first)
        kc = _conv_silu(k_ref.at[b], sk_ref.at[b], cwk_ref, None, first)
        vc = _conv_silu(v_ref.at[b], sv_ref.at[b], cwv_ref, None, first)
        ba = ba_ref[b]
        beta = jax.nn.sigmoid(ba)
        gl = -jnp.exp(alog_ref[...]) * _softplus(ba + dtb_ref[...])
        gcs = _dot_f32(_tri(c).astype(F32), gl)
        g_last = gcs[c - 1:c, :]
        st = dict(vc=vc, eg_last=jnp.exp(g_last), qns=[], kns=[], kts=[], a_kks=[], a_qks=[],
                  gcols=[], brows=[], grows=[])
        kks, qks = [], []
        for hq in range(qk_heads):
            sl = slice(hq * GDN_HEAD, (hq + 1) * GDN_HEAD)
            qn = _l2n(qc[:, sl]) * (GDN_HEAD ** -0.5)
            kn = _l2n(kc[:, sl])
            k2 = jnp.concatenate([kn] * rep, axis=0)
            st["qns"].append(qn)
            st["kns"].append(kn)
            st["kts"].append(kn.T.astype(BF16))
            kks.append(_dot_nt(kn, k2))
            qks.append(_dot_nt(qn, k2))
        for grp in range(n_grp):
            h0 = grp * gsz
            hq0 = h0 // rep
            nq = gsz // rep
            kk = jnp.concatenate(kks[hq0:hq0 + nq], axis=1)
            qk = jnp.concatenate(qks[hq0:hq0 + nq], axis=1)
            gcol = jnp.concatenate(
                [_pair_cols(gcs, v_heads + h0 + 2 * i, v_heads + h0 + 2 * i + 1, c)
                 for i in range(gsz // 2)], axis=1)
            bcol = jnp.concatenate([_pair_cols(beta, h0 + 2 * i, h0 + 2 * i + 1, c)
                                    for i in range(gsz // 2)], axis=1)
            grow = jnp.sum(jnp.where(eye, gcol, 0.0), axis=0, keepdims=True)
            decay = jnp.exp(jnp.where(causal, gcol - grow, -jnp.inf))
            st["a_kks"].append(jnp.where(strict, bcol * kk * decay, 0.0))
            st["a_qks"].append(jnp.where(causal, qk * decay, 0.0))
            st["gcols"].append(gcol)
            st["grows"].append(grow)
            st["brows"].append(jnp.sum(jnp.where(eye, bcol, 0.0), axis=0, keepdims=True))
        return st

    def pair_blocks(b0, b1):
        z = jnp.zeros((c, GDN_HEAD), BF16)
        return jnp.concatenate([jnp.concatenate([b0.astype(BF16), z], axis=1),
                                jnp.concatenate([z, b1.astype(BF16)], axis=1)], axis=0)

    def heads(b, st, grp, tinv):
        h0 = grp * gsz
        gcol, grow, brow = st["gcols"][grp], st["grows"][grp], st["brows"][grp]
        hd = lambda x, i: x[:, i * GDN_HEAD:(i + 1) * GDN_HEAD]
        col = lambda x, k: x[:, k * LANES:(k + 1) * LANES]
        tb = tinv * brow
        tbe = tinv * (brow * jnp.exp(grow))
        us, ws_ = [], []
        for k in range(n_col):
            ha = h0 + 2 * k
            us.append(_dot(col(tb, k), pair_blocks(hd(st["vc"], ha), hd(st["vc"], ha + 1))))
            ws_.append(_dot(col(tbe, k), pair_blocks(st["kns"][ha // rep],
                                                      st["kns"][(ha + 1) // rep])))
        u = jnp.concatenate(us, axis=1)
        w = jnp.concatenate(ws_, axis=1)
        yield
        v_news, q_states = [], []
        for i in range(gsz):
            hq = (h0 + i) // rep
            ws = _dot(jnp.concatenate([hd(w, i), st["qns"][hq]], axis=0), s_ref[b, h0 + i])
            v_news.append(hd(u, i) - ws[:c])
            q_states.append(ws[c:])
        yield
        eg_diag =jnp.where(eye, jnp.exp(gcol), 0.0)
        eg_end_diag = jnp.where(eye, jnp.exp(gcol[c - 1:c, :] - gcol), 0.0)
        os_, vss = [], []
        for k in range(n_col):
            pv = pair_blocks(v_news[2 * k], v_news[2 * k + 1])
            pq = pair_blocks(q_states[2 * k], q_states[2 * k + 1])
            os_.append(_dot(jnp.concatenate([col(st["a_qks"][grp], k), col(eg_diag, k)], axis=1),
                            jnp.concatenate([pv, pq], axis=0)))
            vss.append(_dot(col(eg_end_diag, k), pv))
        o = jnp.concatenate(os_, axis=1)
        v_scaled = jnp.concatenate(vss, axis=1)
        yield
        for i in range(gsz):
            h = h0 + i
            gi = v_heads + h
            hs = slice(h * GDN_HEAD, (h + 1) * GDN_HEAD)
            s_ref[b, h] = (s_ref[b, h] * st["eg_last"][:, gi:gi + 1]
                           + _dot(st["kts"][h // rep], hd(v_scaled, i)))
            o_ref[b, :, hs] = (_rms(hd(o, i), nw_ref[...])
                               * _silu(z_ref[b, :, hs])).astype(o_ref.dtype)

    sts = [prepare(b) for b in range(nb)]

    a_all = [a for st in sts for a in st["a_kks"]]
    dms = neumann([jnp.where(same_blk, -a, 0.0) for a in a_all], sb_shift - 1)
    ms = [-mm([d], bdiag(_split(jnp.where(same_blk, 0.0, a))))[0] for d, a in zip(dms, a_all)]
    ws = neumann(ms, (c // GDN_SUBBLOCK).bit_length() - 2)
    tinvs = [mm([w], bdiag(_split(d)))[0] for w, d in zip(ws, dms)]

    gens = [heads(b, sts[b], grp, tinvs[b * n_grp + grp]) for grp in range(n_grp)
            for b in range(nb)]
    for _ in range(4):
        for gen in gens:
            next(gen, None)


def gdn_mixer_core(proj, conv_w, dt_bias, a_log, norm_w, *, batch, seq, qk_dim, v_dim, v_heads):
    t, n = proj.shape
    c = GDN_CHUNK
    nb = GDN_BATCH_PER_STEP
    cwq, cwk, cwv = conv_w[:, :qk_dim], conv_w[:, qk_dim:2 * qk_dim], conv_w[:, 2 * qk_dim:]
    dtb = jnp.pad(dt_bias, (v_heads, LANES - 2 * v_heads)).reshape(1, LANES)
    alog = jnp.pad(a_log, (v_heads, LANES - 2 * v_heads)).reshape(1, LANES)
    nw = norm_w.reshape(1, GDN_HEAD)
    proj3 = proj.reshape(batch, seq, n)

    const = lambda shape: pl.BlockSpec(shape, lambda i, cc: (0, 0))
    blk = lambda width, col: pl.BlockSpec((nb, c, width), lambda i, cc: (i, cc, col))
    v_blk = (2 * qk_dim) // v_dim
    z_blk = v_blk + 1
    ba_blk = (2 * qk_dim + 2 * v_dim) // LANES
    out = pl.pallas_call(
        functools.partial(_gdn_kernel, v_heads=v_heads),
        grid=(batch // nb, seq // c),
        in_specs=[blk(qk_dim, 0), blk(qk_dim, 1), blk(v_dim, v_blk), blk(v_dim, z_blk),
                  blk(LANES, ba_blk),
                  const((CONV_K, qk_dim)), const((CONV_K, qk_dim)), const((CONV_K, v_dim)),
                  const((1, LANES)), const((1, LANES)), const((1, GDN_HEAD))],
        out_specs=blk(v_dim, 0),
        out_shape=jax.ShapeDtypeStruct((batch, seq, v_dim), BF16),
        scratch_shapes=[pltpu.VMEM((nb, c + SUBLANES, qk_dim), F32),
                        pltpu.VMEM((nb, c + SUBLANES, qk_dim), F32),
                        pltpu.VMEM((nb, c + SUBLANES, v_dim), F32),
                        pltpu.VMEM((nb, v_heads, GDN_HEAD, GDN_HEAD), F32)],
        compiler_params=_cparams("arbitrary", "arbitrary"),
        name="gdn_scan",
    )(proj3, proj3, proj3, proj3, proj3, cwq, cwk, cwv, dtb, alog, nw)
    return out.reshape(t, v_dim)


def _first_index(mask, n):
    idx = lax.broadcasted_iota(I32, mask.shape, 0)
    return jnp.min(jnp.where(mask, idx, n), axis=0, keepdims=True)


def _router_kernel(h_ref, nw_ref, wr_ref, br_ref, upper_ref,
                   eid_ref, rank_ref, gate_ref, cnt_ref, base_ref):
    tq = h_ref.shape[0]
    i = pl.program_id(0)

    @pl.when(i == 0)
    def _():
        base_ref[...] = jnp.zeros(base_ref.shape, F32)

    xn = _rms(h_ref[...], nw_ref[...])
    x_hi, x_lo = _split(xn)
    w_hi, w_lo = wr_ref[0], wr_ref[1]
    logits = (_dot_nt(jnp.concatenate([w_hi, w_lo], axis=0), x_hi)
              + jnp.concatenate([_dot_nt(w_hi, x_lo), jnp.zeros((LANES, tq), F32)], axis=0))
    logits = logits[:LANES] + logits[LANES:] + br_ref[:, 0:1]
    gl = logits[0:N_EGROUPS, :]
    gmax = jnp.max(gl, axis=0, keepdims=True)
    g_sel = _first_index(gl == gmax, N_EGROUPS)
    p_sel = 1.0 / jnp.sum(jnp.exp(gl - gmax), axis=0, keepdims=True)
    e_in = logits[N_EGROUPS:N_EGROUPS + EXPERTS_PER_GROUP, :]
    for g in range(1, N_EGROUPS):
        lo = N_EGROUPS + g * EXPERTS_PER_GROUP
        e_in = jnp.where(g_sel == g, logits[lo:lo + EXPERTS_PER_GROUP, :], e_in)
    m1 = jnp.max(e_in, axis=0, keepdims=True)
    i1 = _first_index(e_in == m1, EXPERTS_PER_GROUP)
    sub = lax.broadcasted_iota(I32, e_in.shape, 0)
    rest = jnp.where(sub == i1, -jnp.inf, e_in)
    m2 = jnp.max(rest, axis=0, keepdims=True)
    i2 = _first_index(rest == m2, EXPERTS_PER_GROUP)
    e2 = jnp.exp(m2 - m1)
    denom = 1.0 + e2
    gate_ref[0:1, :] = (1.0 / denom) * p_sel
    gate_ref[1:2, :] = (e2 / denom) * p_sel
    eid0 = g_sel * EXPERTS_PER_GROUP + i1
    eid1 = g_sel * EXPERTS_PER_GROUP + i2

    erow = lax.broadcasted_iota(I32, (N_EXPERTS, tq), 0)
    oh0 = erow == eid0
    oh1 = erow == eid1
    oh0f = oh0.astype(F32)
    oh1f = oh1.astype(F32)
    cum0 = jnp.dot(oh0f.astype(BF16), upper_ref[...], preferred_element_type=F32)
    cum1 = jnp.dot(oh1f.astype(BF16), upper_ref[...], preferred_element_type=F32)
    base = base_ref[:, 0:1]
    tot0 = jnp.sum(oh0f, axis=1, keepdims=True)
    tot1 = jnp.sum(oh1f, axis=1, keepdims=True)
    r0 = jnp.sum(jnp.where(oh0, base + cum0, 0.0), axis=0, keepdims=True)
    r1 = jnp.sum(jnp.where(oh1, base + tot0 + cum1, 0.0), axis=0, keepdims=True)
    sub = tq // LANES
    for k, (e, r) in enumerate(((eid0, r0.astype(I32)), (eid1, r1.astype(I32)))):
        for rr in range(sub):
            row = k * sub + rr
            eid_ref[0, row:row + 1, :] = e[:, rr * LANES:(rr + 1) * LANES]
            rank_ref[0, row:row + 1, :] = r[:, rr * LANES:(rr + 1) * LANES]
    new_base = base + tot0 + tot1
    base_ref[...] = jnp.broadcast_to(new_base, base_ref.shape)
    cnt_ref[...] = jnp.broadcast_to(new_base, cnt_ref.shape).astype(I32)


def moe_router(h, nw, w_group, b_group, w_expert, b_expert):
    t, d = h.shape
    tq = ROUTE_TILE
    nr = N_EGROUPS + N_EXPERTS
    wr = jnp.pad(jnp.concatenate([w_group, w_expert], axis=1).T, ((0, LANES - nr), (0, 0)))
    br = jnp.pad(jnp.concatenate([b_group, b_expert]), (0, LANES - nr))
    br = jnp.broadcast_to(br[:, None], (LANES, LANES))
    upper = (jnp.arange(tq)[:, None] < jnp.arange(tq)[None, :]).astype(BF16)
    rows = 2 * tq // LANES
    tile_spec = pl.BlockSpec((1, rows, LANES), lambda i: (i, 0, 0))
    tile_shape = jax.ShapeDtypeStruct((t // tq, rows, LANES), I32)
    eid, rank, gate, cnt = pl.pallas_call(
        _router_kernel,
        grid=(t // tq,),
        in_specs=[pl.BlockSpec((tq, d), lambda i: (i, 0)),
                  pl.BlockSpec((1, d), lambda i: (0, 0)),
                  pl.BlockSpec((2, LANES, d), lambda i: (0, 0, 0)),
                  pl.BlockSpec((LANES, LANES), lambda i: (0, 0)),
                  pl.BlockSpec((tq, tq), lambda i: (0, 0))],
        out_specs=[tile_spec, tile_spec, pl.BlockSpec((2, tq), lambda i: (0, i)),
                   pl.BlockSpec((N_EXPERTS, LANES), lambda i: (0, 0))],
        out_shape=[tile_shape, tile_shape, jax.ShapeDtypeStruct((2, t), F32),
                   jax.ShapeDtypeStruct((N_EXPERTS, LANES), I32)],
        scratch_shapes=[pltpu.VMEM((N_EXPERTS, LANES), F32)],
        compiler_params=_cparams("arbitrary"),
        name="moe_router",
    )(h, nw.reshape(1, d), jnp.stack(_split(wr)), br, upper)
    return eid, rank, gate, cnt[:, 0]


def _slot_kernel(pstart_ref, eid_ref, rank_ref, pos_ref):
    eid = eid_ref[...]

    def body(e, acc):
        return jnp.where(eid == e, pstart_ref[e], acc)

    pos_ref[...] = rank_ref[...] + lax.fori_loop(0, N_EXPERTS, body, jnp.zeros(eid.shape, I32))


def moe_slots(pstart, eid_tiles, rank_tiles):
    nt, rows, lanes = eid_tiles.shape
    tiles_per_step = math.gcd(nt, 16)
    blk = pl.BlockSpec((tiles_per_step, rows, lanes), lambda i, ps: (i, 0, 0))
    return pl.pallas_call(
        _slot_kernel,
        grid_spec=pltpu.PrefetchScalarGridSpec(num_scalar_prefetch=1, grid=(nt // tiles_per_step,),
                                               in_specs=[blk, blk], out_specs=blk),
        out_shape=jax.ShapeDtypeStruct(eid_tiles.shape, I32),
        compiler_params=_cparams("arbitrary"),
        name="moe_slots",
    )(pstart, eid_tiles, rank_tiles)


def _row_copy(src, src_row, dst, dst_row, sem):
    return pltpu.make_async_copy(src.at[pl.ds(src_row, 1)], dst.at[pl.ds(dst_row, 1)], sem)


def _dispatch_kernel(lo_ref, hi_ref, nu_ref, pos_hbm, h_ref, nw_ref, xb_out, pos_smem, zero_ref,
                     xn_ref, sem_idx, sem_rows, sem_fill, *, n_tiles):
    tq = h_ref.shape[0]
    sub = tq // LANES
    i = pl.program_id(0)
    n_blocks = xb_out.shape[0] // MOE_BLOCK

    def zero_rows(start, size, wait):
        cp = pltpu.make_async_copy(zero_ref.at[pl.ds(0, size)], xb_out.at[pl.ds(start, size)],
                                   sem_fill)
        cp.wait() if wait else cp.start()

    def fill_padding(e, wait):
        lo, hi = lo_ref[e], hi_ref[e]
        lo8 = jnp.minimum((lo + (SUBLANES - 1)) & -SUBLANES, hi)
        for u in range(SUBLANES - 1):
            @pl.when(lo + u < lo8)
            def _(u=u):
                zero_rows(lo + u, 1, wait)
        cur = lo8
        size = MOE_BLOCK // 2
        while size >= SUBLANES:
            take = ((hi - lo8) & size) != 0

            @pl.when(take)
            def _(cur=cur, size=size):
                zero_rows(pl.multiple_of(cur, SUBLANES), size, wait)
            cur = cur + jnp.where(take, size, 0)
            size //= 2

    def fill_tail(blk, wait):
        zero_rows(pl.multiple_of(blk * MOE_BLOCK, MOE_BLOCK), MOE_BLOCK, wait)

    @pl.when(i == 0)
    def _():
        zero_ref[...] = jnp.zeros(zero_ref.shape, F32)
        for wait in (False, True):
            def per_expert(e, carry, wait=wait):
                fill_padding(e, wait)
                return carry

            def per_block(blk, carry, wait=wait):
                fill_tail(blk, wait)
                return carry

            lax.fori_loop(0, N_EXPERTS, per_expert, 0)
            lax.fori_loop(nu_ref[0], n_blocks, per_block, 0)

    n = n_tiles

    def pos_copy(tile, slot):
        return pltpu.make_async_copy(pos_hbm.at[tile], pos_smem.at[slot], sem_idx)

    def wait_rows(slot):
        for _ in range(2):
            pltpu.make_async_copy(xn_ref.at[slot], xb_out.at[pl.ds(0, tq)], sem_rows.at[slot]).wait()

    @pl.when(i == 0)
    def _():
        pos_copy(0, 0).start()

    for slot in range(2):
        @pl.when(i % 2 == slot)
        def _(slot=slot):
            pos_copy(i, slot).wait()
            pos_copy(jnp.minimum(i + 1, n - 1), 1 - slot).start()
            xn_ref[slot] = _rms(h_ref[...], nw_ref[...])
            for rr in range(sub):
                for cc in range(LANES):
                    for k in range(2):
                        _row_copy(xn_ref.at[slot], rr * LANES + cc, xb_out,
                                  pos_smem[slot, k * sub + rr, cc], sem_rows.at[slot]).start()

            @pl.when(i > 0)
            def _():
                wait_rows(1 - slot)

    @pl.when(i == n - 1)
    def _():
        wait_rows((n - 1) % 2)
        pos_copy(i, 1 - (n - 1) % 2).wait()


def moe_dispatch(h, nw, pos_tiles, pad_lo, pad_hi, n_used, n_pad):
    t, d = h.shape
    tq = ROUTE_TILE
    grid_spec = pltpu.PrefetchScalarGridSpec(
        num_scalar_prefetch=3,
        grid=(t // tq,),
        in_specs=[pl.BlockSpec(memory_space=pl.ANY),
                  pl.BlockSpec((tq, d), lambda i, lo, hi, nu: (i, 0)),
                  pl.BlockSpec((1, d), lambda i, lo, hi, nu: (0, 0))],
        out_specs=pl.BlockSpec(memory_space=pl.ANY),
        scratch_shapes=[pltpu.SMEM((2, 2 * tq // LANES, LANES), I32),
                        pltpu.VMEM((MOE_BLOCK, d), F32),
                        pltpu.VMEM((2, tq, d), F32),
                        pltpu.SemaphoreType.DMA, pltpu.SemaphoreType.DMA((2,)),
                        pltpu.SemaphoreType.DMA],
    )
    return pl.pallas_call(
        functools.partial(_dispatch_kernel, n_tiles=t // tq),
        grid_spec=grid_spec,
        out_shape=jax.ShapeDtypeStruct((n_pad, d), F32),
        compiler_params=_cparams("arbitrary"),
        name="moe_dispatch",
    )(pad_lo, pad_hi, n_used, pos_tiles, h, nw.reshape(1, d))


def _expert_kernel(be_ref, nu_ref, x_ref, w1_ref, w3_ref, w2_ref, o_ref,
                   w1b_ref, w3b_ref, w2b_ref):
    b = pl.program_id(0)
    used = b < nu_ref[0]
    new_expert = (b == 0) | (be_ref[b] != be_ref[jnp.maximum(b - 1, 0)])

    @pl.when(used & new_expert)
    def _():
        w1b_ref[...] = w1_ref[0].astype(BF16)
        w3b_ref[...] = w3_ref[0].astype(BF16)
        w2b_ref[...] = w2_ref[0].astype(BF16)

    @pl.when(used)
    def _():
        xb = x_ref[...].astype(BF16)
        h1 = jnp.dot(xb, w1b_ref[...], preferred_element_type=F32)
        h3 = jnp.dot(xb, w3b_ref[...], preferred_element_type=F32)
        hid = (_silu(h1) * h3).astype(BF16)
        o_ref[...] = jnp.dot(hid, w2b_ref[...], preferred_element_type=F32)

    @pl.when(b >= nu_ref[0])
    def _():
        o_ref[...] = jnp.zeros(o_ref.shape, F32)


def moe_experts(xb, w1, w3, w2, blk_expert, n_used):
    n_pad, d = xb.shape
    de = w1.shape[2]
    nb = n_pad // MOE_BLOCK
    grid_spec = pltpu.PrefetchScalarGridSpec(
        num_scalar_prefetch=2,
        grid=(nb,),
        in_specs=[pl.BlockSpec((MOE_BLOCK, d), lambda b, be, nu: (jnp.minimum(b, nu[0] - 1), 0)),
                  pl.BlockSpec((1, d, de), lambda b, be, nu: (be[b], 0, 0)),
                  pl.BlockSpec((1, d, de), lambda b, be, nu: (be[b], 0, 0)),
                  pl.BlockSpec((1, de, d), lambda b, be, nu: (be[b], 0, 0))],
        out_specs=pl.BlockSpec((MOE_BLOCK, d), lambda b, be, nu: (b, 0)),
        scratch_shapes=[pltpu.VMEM((d, de), BF16), pltpu.VMEM((d, de), BF16),
                        pltpu.VMEM((de, d), BF16)],
    )
    return pl.pallas_call(
        _expert_kernel,
        grid_spec=grid_spec,
        out_shape=jax.ShapeDtypeStruct((n_pad, d), F32),
        compiler_params=_cparams("arbitrary"),
        name="moe_experts",
    )(blk_expert, n_used, xb, w1, w3, w2)


def _combine_ple_kernel(pos_hbm, yb_hbm, h_ref, gate_ref, p_ref, nw_ref, wg_ref, wp_ref, fw_ref,
                        o_ref, pos_smem, buf_ref, sem_idx, sem_rows, *, final, n_tiles):
    tq = h_ref.shape[0]
    sub = tq // LANES
    i = pl.program_id(0)
    n = n_tiles

    def pos_copy(tile, slot):
        return pltpu.make_async_copy(pos_hbm.at[tile], pos_smem.at[slot], sem_idx)

    def start_row(slot, k, rr, cc):
        _row_copy(yb_hbm, pos_smem[slot, k * sub + rr, cc], buf_ref.at[slot, k], rr * LANES + cc,
                  sem_rows.at[slot]).start()

    def wait_rows(slot):
        for k in range(2):
            pltpu.make_async_copy(yb_hbm.at[pl.ds(0, tq)], buf_ref.at[slot, k],
                                  sem_rows.at[slot]).wait()

    def combine(slot):
        gate = gate_ref[...]
        h2 = h_ref[...] + gate[:, 0:1] * buf_ref[slot, 0] + gate[:, 1:2] * buf_ref[slot, 1]
        hn = _rms(h2, nw_ref[...]).astype(BF16)
        pg = jax.nn.sigmoid(jnp.dot(hn, wg_ref[...], preferred_element_type=F32))
        pp = jnp.dot(p_ref[...].astype(BF16), wp_ref[...], preferred_element_type=F32)
        out = h2 + pg * pp
        if final:
            out = _rms(out, fw_ref[...])
        o_ref[...] = out

    @pl.when(i == 0)
    def _():
        first = pos_copy(0, 0)
        first.start()
        first.wait()
        for rr in range(sub):
            def body(cc, carry, rr=rr):
                for k in range(2):
                    start_row(0, k, rr, cc)
                return carry
            lax.fori_loop(0, LANES, body, 0)
        pos_copy(min(1, n - 1), 1).start()

    nxt = jnp.minimum(i + 1, n - 1)
    for slot in range(2):
        @pl.when(i % 2 == slot)
        def _(slot=slot):
            pos_copy(nxt, 1 - slot).wait()
            wait_rows(slot)
            pos_copy(jnp.minimum(i + 2, n - 1), slot).start()
            for rr in range(sub):
                for cc in range(LANES):
                    for k in range(2):
                        start_row(1 - slot, k, rr, cc)
            combine(slot)

    @pl.when(i == n - 1)
    def _():
        wait_rows(1 - (n - 1) % 2)
        pos_copy(i, (n - 1) % 2).wait()


def moe_combine_ple(h, yb, pos_tiles, gate_cols, p, nw, wg, wp, fw, *, final):
    t, d = h.shape
    tq = ROUTE_TILE
    pd = p.shape[1]
    return pl.pallas_call(
        functools.partial(_combine_ple_kernel, final=final, n_tiles=t // tq),
        grid=(t // tq,),
        in_specs=[pl.BlockSpec(memory_space=pl.ANY),
                  pl.BlockSpec(memory_space=pl.ANY),
                  pl.BlockSpec((tq, d), lambda i: (i, 0)),
                  pl.BlockSpec((tq, 2), lambda i: (i, 0)),
                  pl.BlockSpec((tq, pd), lambda i: (i, 0)),
                  pl.BlockSpec((1, d), lambda i: (0, 0)),
                  pl.BlockSpec((d, d), lambda i: (0, 0)),
                  pl.BlockSpec((pd, d), lambda i: (0, 0)),
                  pl.BlockSpec((1, d), lambda i: (0, 0))],
        out_specs=pl.BlockSpec((tq, d), lambda i: (i, 0)),
        out_shape=jax.ShapeDtypeStruct((t, d), F32),
        scratch_shapes=[pltpu.SMEM((2, 2 * tq // LANES, LANES), I32),
                        pltpu.VMEM((2, 2, tq, d), F32),
                        pltpu.SemaphoreType.DMA, pltpu.SemaphoreType.DMA((2,))],
        compiler_params=_cparams("arbitrary"),
        name="moe_combine_ple",
    )(pos_tiles, yb, h, gate_cols, p, nw.reshape(1, d), wg, wp, fw.reshape(1, d))


def moe_ple_layer(h, p, norm_moe, w_group, b_group, w_expert, b_expert, w1, w3, w2,
                  norm_ple, wg, wp, fw, *, layer, final):
    t, d = h.shape
    eid_tiles, rank_tiles, gate, counts = moe_router(h, norm_moe, w_group, b_group, w_expert,
                                                     b_expert)
    padded = (counts + MOE_BLOCK - 1) // MOE_BLOCK * MOE_BLOCK
    pend = jnp.cumsum(padded)
    pstart = pend - padded
    na = 2 * t
    n_pad = (na + MOE_BLOCK - 1) // MOE_BLOCK * MOE_BLOCK + N_EXPERTS * MOE_BLOCK
    nb = n_pad // MOE_BLOCK
    blk_start = jnp.arange(nb, dtype=I32) * MOE_BLOCK
    blk_expert = jnp.minimum(jnp.sum(blk_start[:, None] >= pend[None, :], axis=-1),
                             N_EXPERTS - 1).astype(I32)
    n_used = (pend[-1:] // MOE_BLOCK).astype(I32)
    pos_tiles = moe_slots(pstart.astype(I32), eid_tiles, rank_tiles)
    xb = moe_dispatch(h, norm_moe, pos_tiles, (pstart + counts).astype(I32), pend.astype(I32),
                      n_used, n_pad)
    yb = moe_experts(xb, w1, w3, w2, blk_expert + layer * N_EXPERTS, n_used)
    return moe_combine_ple(h, yb, pos_tiles, gate.T, p, norm_ple, wg.astype(BF16),
                           wp.astype(BF16), fw, final=final)


def _pad_cols(w, n):
    return jnp.pad(w, ((0, 0), (0, n - w.shape[1])))


def kernel(x, p, norm_mix, norm_moe, norm_ple, final_norm, m_in_w, m_conv_w, m_conv_b, m_dt_bias, m_A_log, m_D, m_norm_w, m_out_w, g_in_w, g_conv_w, g_dt_bias, g_A_log, g_norm_w, g_out_w, moe_w_group, moe_b_group, moe_w_expert, moe_b_expert, moe_w1, moe_w3, moe_w2, ple_w_proj, ple_w_gate):
    batch, seq, d = x.shape
    t = batch * seq
    depth = p.shape[0]
    pd = p.shape[-1]
    h = x.reshape(t, d)
    p2 = p.reshape(depth, t, pd)
    w1_all = moe_w1.reshape((-1,) + moe_w1.shape[2:])
    w3_all = moe_w3.reshape((-1,) + moe_w3.shape[2:])
    w2_all = moe_w2.reshape((-1,) + moe_w2.shape[2:])
    for i in range(depth):
        j = i // 2
        if i % 2 == 0:
            inner = m_out_w.shape[1]
            heads = m_dt_bias.shape[1]
            conv_dim = m_conv_w.shape[2]
            w_in = _pad_cols(m_in_w[j], inner + conv_dim + LANES).astype(BF16)
            proj = norm_matmul(h, norm_mix[i], w_in)
            y = ssd_mixer_core(proj, m_conv_w[j], m_conv_b[j], m_dt_bias[j], m_A_log[j], m_D[j],
                               m_norm_w[j], batch=batch, seq=seq, inner=inner, heads=heads)
            h = matmul_residual(y, m_out_w[j].astype(BF16), h)
        else:
            v_dim = g_out_w.shape[1]
            v_heads = g_dt_bias.shape[1]
            conv_dim = g_conv_w.shape[2]
            qk_dim = (conv_dim - v_dim) // 2
            w_in = _pad_cols(g_in_w[j], conv_dim + v_dim + LANES).astype(BF16)
            proj = norm_matmul(h, norm_mix[i], w_in)
            y = gdn_mixer_core(proj, g_conv_w[j], g_dt_bias[j], g_A_log[j], g_norm_w[j],
                               batch=batch, seq=seq, qk_dim=qk_dim, v_dim=v_dim, v_heads=v_heads)
            h = matmul_residual(y, g_out_w[j].astype(BF16), h)
        h = moe_ple_layer(h, p2[i], norm_moe[i], moe_w_group[i], moe_b_group[i], moe_w_expert[i],
                          moe_b_expert[i], w1_all, w3_all, w2_all, norm_ple[i],
                          ple_w_gate[i], ple_w_proj[i], final_norm, layer=i,
                          final=(i == depth - 1))
    return h.reshape(batch, seq, d)
```

```python
import functools
import math

import jax
import jax.numpy as jnp
from jax import lax
from jax.experimental import pallas as pl
from jax.experimental.pallas import tpu as pltpu

F32 = jnp.float32
BF16 = jnp.bfloat16
I32 = jnp.int32
EPS = 1e-6
HIGHEST = lax.Precision.HIGHEST

LANES = 128
SUBLANES = 8
VMEM_LIMIT = 56 * 1024 * 1024

CONV_K = 4
SSD_CHUNK = 128
SSD_HEADDIM = 64
SSD_STATE = 128
SSD_GROUPS = 4
GDN_CHUNK = 64
GDN_HEAD = 128
GDN_STACK = 256
GDN_SUBBLOCK = 16
GDN_BATCH_PER_STEP = 1
N_EGROUPS = 8
EXPERTS_PER_GROUP = 8
N_EXPERTS = N_EGROUPS * EXPERTS_PER_GROUP
MOE_BLOCK = 512
ROUTE_TILE = 512


def _cparams(*sem):
    return pltpu.CompilerParams(dimension_semantics=sem, vmem_limit_bytes=VMEM_LIMIT)


def _silu(x):
    return x * jax.nn.sigmoid(x)


def _softplus(x):
    return jnp.maximum(x, 0.0) + jnp.log1p(jnp.exp(-jnp.abs(x)))


def _rms(x, w):
    return x * lax.rsqrt(jnp.mean(x * x, axis=-1, keepdims=True) + EPS) * w


def _dot(a, b):
    return jnp.dot(a.astype(BF16), b.astype(BF16), preferred_element_type=F32)


def _split(x):
    hi = x.astype(BF16)
    return hi, (x - hi.astype(F32)).astype(BF16)


def _dot_f32(a, b):
    return jnp.dot(a, b, preferred_element_type=F32, precision=HIGHEST)


def _dot_tn(a, b):
    return lax.dot_general(a.astype(BF16), b.astype(BF16), (((0,), (0,)), ((), ())),
                           preferred_element_type=F32)


def _dot_nt(a, b):
    return lax.dot_general(a.astype(BF16), b.astype(BF16), (((1,), (1,)), ((), ())),
                           preferred_element_type=F32)


def _tri(n, strict=False):
    r = lax.broadcasted_iota(I32, (n, n), 0)
    c = lax.broadcasted_iota(I32, (n, n), 1)
    return (r > c) if strict else (r >= c)


def _norm_matmul_kernel(x_ref, nw_ref, w_ref, o_ref, *, n_chunk):
    xb = _rms(x_ref[...], nw_ref[...]).astype(BF16)
    n = o_ref.shape[1]
    for c0 in range(0, n, n_chunk):
        c1 = min(c0 + n_chunk, n)
        o_ref[:, c0:c1] = jnp.dot(xb, w_ref[:, c0:c1], preferred_element_type=F32)


def norm_matmul(x, nw, w, *, tm=256, n_chunk=512):
    t, d = x.shape
    n = w.shape[1]
    return pl.pallas_call(
        functools.partial(_norm_matmul_kernel, n_chunk=n_chunk),
        grid=(t // tm,),
        in_specs=[pl.BlockSpec((tm, d), lambda i: (i, 0)),
                  pl.BlockSpec((1, d), lambda i: (0, 0)),
                  pl.BlockSpec((d, n), lambda i: (0, 0))],
        out_specs=pl.BlockSpec((tm, n), lambda i: (i, 0)),
        out_shape=jax.ShapeDtypeStruct((t, n), F32),
        compiler_params=_cparams("arbitrary"),
        name="norm_matmul",
    )(x, nw.reshape(1, d), w)


def _matmul_residual_kernel(y_ref, w_ref, r_ref, o_ref):
    o_ref[...] = r_ref[...] + jnp.dot(y_ref[...].astype(BF16), w_ref[...],
                                      preferred_element_type=F32)


def matmul_residual(y, w, res, *, tm=512):
    t, k = y.shape
    d = w.shape[1]
    return pl.pallas_call(
        _matmul_residual_kernel,
        grid=(t // tm,),
        in_specs=[pl.BlockSpec((tm, k), lambda i: (i, 0)),
                  pl.BlockSpec((k, d), lambda i: (0, 0)),
                  pl.BlockSpec((tm, d), lambda i: (i, 0))],
        out_specs=pl.BlockSpec((tm, d), lambda i: (i, 0)),
        out_shape=jax.ShapeDtypeStruct((t, d), F32),
        compiler_params=_cparams("arbitrary"),
        name="matmul_residual",
    )(y, w, res)


def _conv_silu(x_ref, stage_ref, w_ref, bias, first):
    q = x_ref.shape[0]

    @pl.when(first)
    def _():
        stage_ref[0:SUBLANES, :] = jnp.zeros((SUBLANES, stage_ref.shape[1]), F32)

    stage_ref[SUBLANES:SUBLANES + q, :] = x_ref[...]
    acc = stage_ref[SUBLANES:SUBLANES + q, :] * w_ref[CONV_K - 1:CONV_K, :]
    for j in range(CONV_K - 1):
        off = SUBLANES - (CONV_K - 1) + j
        acc = acc + stage_ref[off:off + q, :] * w_ref[j:j + 1, :]
    if bias is not None:
        acc = acc + bias
    stage_ref[0:SUBLANES, :] = stage_ref[q:q + SUBLANES, :]
    return _silu(acc)


def _ssd_kernel(z_ref, x_ref, b_ref, c_ref, dt_ref,
                cwx_ref, cwb_ref, cwc_ref, cbx_ref, cbb_ref, cbc_ref,
                dtb_ref, alog_ref, dfull_ref, nw_ref,
                o_ref,
                sx_ref, sb_ref, sc_ref, y_ref, xw_ref, st_ref):
    q = x_ref.shape[0]
    hpg = x_ref.shape[1] // (SSD_GROUPS * SSD_HEADDIM)
    first = pl.program_id(1) == 0

    @pl.when(first)
    def _():
        st_ref[...] = jnp.zeros(st_ref.shape, F32)

    xs = _conv_silu(x_ref, sx_ref, cwx_ref, cbx_ref[...], first)
    bm = _conv_silu(b_ref, sb_ref, cwb_ref, cbb_ref[...], first)
    cm = _conv_silu(c_ref, sc_ref, cwc_ref, cbc_ref[...], first)

    dt = _softplus(dt_ref[...] + dtb_ref[...])
    da = dt * (-jnp.exp(alog_ref[...]))
    causal = _tri(q)
    a = _dot_f32(causal.astype(F32), da)
    a_t = a.T
    ea_t = jnp.exp(a_t)
    a_last = a[q - 1:q, :]
    to_end_t = (jnp.exp(a_last - a) * dt).T
    ea_last = jnp.exp(a_last)
    dt_t = dt.T
    eye = (lax.broadcasted_iota(I32, (q, q), 0) == lax.broadcasted_iota(I32, (q, q), 1))
    lane_lo = lax.broadcasted_iota(I32, (q, 2 * SSD_HEADDIM), 1) < SSD_HEADDIM

    def group(g):
        bg = bm[:, g * SSD_STATE:(g + 1) * SSD_STATE]
        cg = cm[:, g * SSD_STATE:(g + 1) * SSD_STATE]
        cb = _dot_nt(cg, bg)
        gw = hpg * SSD_HEADDIM
        cs = _dot(cg, st_ref[g])
        yield
        for pr in range(hpg // 2):
            j0 = g * hpg + 2 * pr
            lo, hi = j0 * SSD_HEADDIM, (j0 + 2) * SSD_HEADDIM
            xp = xs[:, lo:hi]
            cp = cs[:, 2 * pr * SSD_HEADDIM:(2 * pr + 2) * SSD_HEADDIM]
            x_sel = [jnp.where(lane_lo, xp, 0.0), jnp.where(lane_lo, 0.0, xp)]
            c_sel = [jnp.where(lane_lo, cp, 0.0), jnp.where(lane_lo, 0.0, cp)]
            lhs, rhs, te = [], [], []
            for k in range(2):
                j = j0 + k
                diff = a[:, j:j + 1] - a_t[j:j + 1, :]
                seg = jnp.exp(jnp.where(causal, diff, -jnp.inf))
                lhs += [cb * seg * dt_t[j:j + 1, :], jnp.where(eye, ea_t[j:j + 1, :], 0.0)]
                rhs += [x_sel[k], c_sel[k]]
                te.append(jnp.where(eye, to_end_t[j:j + 1, :], 0.0))
            y_ref[:, lo:hi] = _dot(jnp.concatenate(lhs, axis=1), jnp.concatenate(rhs, axis=0))
            xw_ref[:, lo:hi] = _dot(jnp.concatenate(te, axis=1), jnp.concatenate(x_sel, axis=0))
            yield
        upd = _dot_tn(bg, xw_ref[:, g * gw:(g + 1) * gw])
        for jj in range(hpg):
            j = g * hpg + jj
            sl = slice(jj * SSD_HEADDIM, (jj + 1) * SSD_HEADDIM)
            st_ref[g, :, sl] = st_ref[g, :, sl] * ea_last[:, j:j + 1] + upd[:, sl]

    gens = [group(g) for g in range(SSD_GROUPS)]
    for _ in range(2 + hpg // 2):
        for gen in gens:
            next(gen, None)

    y = y_ref[...] + dfull_ref[...] * xs
    o_ref[...] = _rms(y * _silu(z_ref[...]), nw_ref[...]).astype(o_ref.dtype)


def ssd_mixer_core(proj, conv_w, conv_b, dt_bias, a_log, d_skip, norm_w, *, batch, seq,
                   inner, heads):
    t = proj.shape[0]
    q = SSD_CHUNK
    gn = SSD_GROUPS * SSD_STATE
    nc = seq // q
    cwx, cwb, cwc = conv_w[:, :inner], conv_w[:, inner:inner + gn], conv_w[:, inner + gn:]
    cbx = conv_b[:inner].reshape(1, inner)
    cbb = conv_b[inner:inner + gn].reshape(1, gn)
    cbc = conv_b[inner + gn:].reshape(1, gn)
    pad = LANES - heads
    dtb = jnp.pad(dt_bias, (0, pad)).reshape(1, LANES)
    alog = jnp.pad(a_log, (0, pad)).reshape(1, LANES)
    dfull = jnp.repeat(d_skip, SSD_HEADDIM).reshape(1, inner)
    nw = norm_w.reshape(1, inner)

    def row(i, c):
        return i * nc + c

    const = lambda shape: pl.BlockSpec(shape, lambda i, c: (0, 0))
    x_blk = inner // inner
    b_blk = (2 * inner) // gn
    c_blk = b_blk + 1
    dt_blk = (2 * inner + 2 * gn) // LANES
    return pl.pallas_call(
        _ssd_kernel,
        grid=(batch, nc),
        in_specs=[pl.BlockSpec((q, inner), lambda i, c: (row(i, c), 0)),
                  pl.BlockSpec((q, inner), lambda i, c: (row(i, c), x_blk)),
                  pl.BlockSpec((q, gn), lambda i, c: (row(i, c), b_blk)),
                  pl.BlockSpec((q, gn), lambda i, c: (row(i, c), c_blk)),
                  pl.BlockSpec((q, LANES), lambda i, c: (row(i, c), dt_blk)),
                  const((CONV_K, inner)), const((CONV_K, gn)), const((CONV_K, gn)),
                  const((1, inner)), const((1, gn)), const((1, gn)),
                  const((1, LANES)), const((1, LANES)), const((1, inner)), const((1, inner))],
        out_specs=pl.BlockSpec((q, inner), lambda i, c: (row(i, c), 0)),
        out_shape=jax.ShapeDtypeStruct((t, inner), BF16),
        scratch_shapes=[pltpu.VMEM((q + SUBLANES, inner), F32),
                        pltpu.VMEM((q + SUBLANES, gn), F32),
                        pltpu.VMEM((q + SUBLANES, gn), F32),
                        pltpu.VMEM((q, inner), F32),
                        pltpu.VMEM((q, inner), F32),
                        pltpu.VMEM((SSD_GROUPS, SSD_STATE, inner // SSD_GROUPS), F32)],
        compiler_params=_cparams("arbitrary", "arbitrary"),
        name="ssd_scan",
    )(proj, proj, proj, proj, proj, cwx, cwb, cwc, cbx, cbb, cbc, dtb, alog, dfull, nw)


def _l2n(x):
    return x * lax.rsqrt(jnp.sum(x * x, axis=-1, keepdims=True) + EPS)


def _pair_cols(x, i0, i1, half):
    c = x.shape[0]
    lane = lax.broadcasted_iota(I32, (c, 2 * half), 1)
    return jnp.where(lane < half, jnp.broadcast_to(x[:, i0:i0 + 1], (c, 2 * half)),
                     jnp.broadcast_to(x[:, i1:i1 + 1], (c, 2 * half)))


def _gdn_kernel(q_ref, k_ref, v_ref, z_ref, ba_ref,
                cwq_ref, cwk_ref, cwv_ref, dtb_ref, alog_ref, nw_ref,
                o_ref,
                sq_ref, sk_ref, sv_ref, s_ref, *, v_heads):
    nb, c = q_ref.shape[0], q_ref.shape[1]
    qk_heads = q_ref.shape[2] // GDN_HEAD
    rep = v_heads // qk_heads
    gsz = GDN_STACK // c
    n_grp = v_heads // gsz
    first = pl.program_id(1) == 0

    @pl.when(first)
    def _():
        s_ref[...] = jnp.zeros(s_ref.shape, F32)

    t_idx = lax.broadcasted_iota(I32, (c, GDN_STACK), 0)
    s_idx = lax.broadcasted_iota(I32, (c, GDN_STACK), 1) & (c - 1)
    causal = t_idx >= s_idx
    strict = t_idx > s_idx
    eye = t_idx == s_idx
    eye_f = eye.astype(F32)
    sb_shift = GDN_SUBBLOCK.bit_length() - 1
    same_blk = (t_idx >> sb_shift) == (s_idx >> sb_shift)
    n_col = GDN_STACK // LANES
    lane_lo = lax.broadcasted_iota(I32, (c, LANES), 1) < c

    def bdiag(parts):
        zero = jnp.zeros((), BF16)
        cols = lambda p: [p[:, k * LANES:(k + 1) * LANES] for k in range(n_col)]
        return tuple([jnp.concatenate([jnp.where(lane_lo, y, zero), jnp.where(lane_lo, zero, y)],
                                      axis=0) for y in cols(p)] for p in parts)

    def mm(lhs, rhs_bd):
        n = len(lhs) * c
        parts = [_split(x) for x in lhs]
        l_hi = jnp.concatenate([p[0] for p in parts], axis=0)
        l_both = jnp.concatenate([l_hi] + [p[1] for p in parts], axis=0)
        outs = []
        for k in range(n_col):
            sl = slice(k * LANES, (k + 1) * LANES)
            r = jnp.dot(l_both[:, sl], rhs_bd[0][k], preferred_element_type=F32)
            outs.append(r[:n] + r[n:] + jnp.dot(l_hi[:, sl], rhs_bd[1][k],
                                                preferred_element_type=F32))
        out = jnp.concatenate(outs, axis=1)
        return [out[i * c:(i + 1) * c] for i in range(len(lhs))]

    def neumann(p0s, n_sq):
        ps = list(p0s)
        ds = [eye_f + p for p in ps]
        for i in range(n_sq):
            for g in range(len(ps)):
                bd = bdiag(_split(ps[g]))
                if i == 0:
                    ps[g], = mm([ps[g]], bd)
                else:
                    ps[g], x = mm([ps[g], ds[g]], bd)
                    ds[g] = ds[g] + x
        if n_sq > 0:
            for g in range(len(ps)):
                x, = mm([ds[g]], bdiag(_split(ps[g])))
                ds[g] = ds[g] + x
        return ds

    def prepare(b):
        qc = _conv_silu(q_ref.at[b], sq_ref.at[b], cwq_ref, None, first)
        kc = _conv_silu(k_ref.at[b], sk_ref.at[b], cwk_ref, None, first)
        vc = _conv_silu(v_ref.at[b], sv_ref.at[b], cwv_ref, None, first)
        ba = ba_ref[b]
        beta = jax.nn.sigmoid(ba)
        gl = -jnp.exp(alog_ref[...]) * _softplus(ba + dtb_ref[...])
        gcs = _dot_f32(_tri(c).astype(F32), gl)
        g_last = gcs[c - 1:c, :]
        st = dict(vc=vc, eg_last=jnp.exp(g_last), qns=[], kns=[], kts=[], a_kks=[], a_qks=[],
                  gcols=[], brows=[], grows=[])
        kks, qks = [], []
        for hq in range(qk_heads):
            sl = slice(hq * GDN_HEAD, (hq + 1) * GDN_HEAD)
            qn = _l2n(qc[:, sl]) * (GDN_HEAD ** -0.5)
            kn = _l2n(kc[:, sl])
            k2 = jnp.concatenate([kn] * rep, axis=0)
            st["qns"].append(qn)
            st["kns"].append(kn)
            st["kts"].append(kn.T.astype(BF16))
            kks.append(_dot_nt(kn, k2))
            qks.append(_dot_nt(qn, k2))
        for grp in range(n_grp):
            h0 = grp * gsz
            hq0 = h0 // rep
            nq = gsz // rep
            kk = jnp.concatenate(kks[hq0:hq0 + nq], axis=1)
            qk = jnp.concatenate(qks[hq0:hq0 + nq], axis=1)
            gcol = jnp.concatenate(
                [_pair_cols(gcs, v_heads + h0 + 2 * i, v_heads + h0 + 2 * i + 1, c)
                 for i in range(gsz // 2)], axis=1)
            bcol = jnp.concatenate([_pair_cols(beta, h0 + 2 * i, h0 + 2 * i + 1, c)
                                    for i in range(gsz // 2)], axis=1)
            grow = jnp.sum(jnp.where(eye, gcol, 0.0), axis=0, keepdims=True)
            decay = jnp.exp(jnp.where(causal, gcol - grow, -jnp.inf))
            st["a_kks"].append(jnp.where(strict, bcol * kk * decay, 0.0))
            st["a_qks"].append(jnp.where(causal, qk * decay, 0.0))
            st["gcols"].append(gcol)
            st["grows"].append(grow)
            st["brows"].append(jnp.sum(jnp.where(eye, bcol, 0.0), axis=0, keepdims=True))
        return st

    def pair_blocks(b0, b1):
        z = jnp.zeros((c, GDN_HEAD), BF16)
        return jnp.concatenate([jnp.concatenate([b0.astype(BF16), z], axis=1),
                                jnp.concatenate([z, b1.astype(BF16)], axis=1)], axis=0)

    def heads(b, st, grp, tinv):
        h0 = grp * gsz
        gcol, grow, brow = st["gcols"][grp], st["grows"][grp], st["brows"][grp]
        hd = lambda x, i: x[:, i * GDN_HEAD:(i + 1) * GDN_HEAD]
        col = lambda x, k: x[:, k * LANES:(k + 1) * LANES]
        tb = tinv * brow
        tbe = tinv * (brow * jnp.exp(grow))
        us, ws_ = [], []
        for k in range(n_col):
            ha = h0 + 2 * k
            us.append(_dot(col(tb, k), pair_blocks(hd(st["vc"], ha), hd(st["vc"], ha + 1))))
            ws_.append(_dot(col(tbe, k), pair_blocks(st["kns"][ha // rep],
                                                      st["kns"][(ha + 1) // rep])))
        u = jnp.concatenate(us, axis=1)
        w = jnp.concatenate(ws_, axis=1)
        yield
        v_news, q_states = [], []
        for i in range(gsz):
            hq = (h0 + i) // rep
            ws = _dot(jnp.concatenate([hd(w, i), st["qns"][hq]], axis=0), s_ref[b, h0 + i])
            v_news.append(hd(u, i) - ws[:c])
            q_states.append(ws[c:])
        yield
        eg_diag =jnp.where(eye, jnp.exp(gcol), 0.0)
        eg_end_diag = jnp.where(eye, jnp.exp(gcol[c - 1:c, :] - gcol), 0.0)
        os_, vss = [], []
        for k in range(n_col):
            pv = pair_blocks(v_news[2 * k], v_news[2 * k + 1])
            pq = pair_blocks(q_states[2 * k], q_states[2 * k + 1])
            os_.append(_dot(jnp.concatenate([col(st["a_qks"][grp], k), col(eg_diag, k)], axis=1),
                            jnp.concatenate([pv, pq], axis=0)))
            vss.append(_dot(col(eg_end_diag, k), pv))
        o = jnp.concatenate(os_, axis=1)
        v_scaled = jnp.concatenate(vss, axis=1)
        yield
        for i in range(gsz):
            h = h0 + i
            gi = v_heads + h
            hs = slice(h * GDN_HEAD, (h + 1) * GDN_HEAD)
            s_ref[b, h] = (s_ref[b, h] * st["eg_last"][:, gi:gi + 1]
                           + _dot(st["kts"][h // rep], hd(v_scaled, i)))
            o_ref[b, :, hs] = (_rms(hd(o, i), nw_ref[...])
                               * _silu(z_ref[b, :, hs])).astype(o_ref.dtype)

    sts = [prepare(b) for b in range(nb)]

    a_all = [a for st in sts for a in st["a_kks"]]
    dms = neumann([jnp.where(same_blk, -a, 0.0) for a in a_all], sb_shift - 1)
    ms = [-mm([d], bdiag(_split(jnp.where(same_blk, 0.0, a))))[0] for d, a in zip(dms, a_all)]
    ws = neumann(ms, (c // GDN_SUBBLOCK).bit_length() - 2)
    tinvs = [mm([w], bdiag(_split(d)))[0] for w, d in zip(ws, dms)]

    gens = [heads(b, sts[b], grp, tinvs[b * n_grp + grp]) for grp in range(n_grp)
            for b in range(nb)]
    for _ in range(4):
        for gen in gens:
            next(gen, None)


def gdn_mixer_core(proj, conv_w, dt_bias, a_log, norm_w, *, batch, seq, qk_dim, v_dim, v_heads):
    t, n = proj.shape
    c = GDN_CHUNK
    nb = GDN_BATCH_PER_STEP
    cwq, cwk, cwv = conv_w[:, :qk_dim], conv_w[:, qk_dim:2 * qk_dim], conv_w[:, 2 * qk_dim:]
    dtb = jnp.pad(dt_bias, (v_heads, LANES - 2 * v_heads)).reshape(1, LANES)
    alog = jnp.pad(a_log, (v_heads, LANES - 2 * v_heads)).reshape(1, LANES)
    nw = norm_w.reshape(1, GDN_HEAD)
    proj3 = proj.reshape(batch, seq, n)

    const = lambda shape: pl.BlockSpec(shape, lambda i, cc: (0, 0))
    blk = lambda width, col: pl.BlockSpec((nb, c, width), lambda i, cc: (i, cc, col))
    v_blk = (2 * qk_dim) // v_dim
    z_blk = v_blk + 1
    ba_blk = (2 * qk_dim + 2 * v_dim) // LANES
    out = pl.pallas_call(
        functools.partial(_gdn_kernel, v_heads=v_heads),
        grid=(batch // nb, seq // c),
        in_specs=[blk(qk_dim, 0), blk(qk_dim, 1), blk(v_dim, v_blk), blk(v_dim, z_blk),
                  blk(LANES, ba_blk),
                  const((CONV_K, qk_dim)), const((CONV_K, qk_dim)), const((CONV_K, v_dim)),
                  const((1, LANES)), const((1, LANES)), const((1, GDN_HEAD))],
        out_specs=blk(v_dim, 0),
        out_shape=jax.ShapeDtypeStruct((batch, seq, v_dim), BF16),
        scratch_shapes=[pltpu.VMEM((nb, c + SUBLANES, qk_dim), F32),
                        pltpu.VMEM((nb, c + SUBLANES, qk_dim), F32),
                        pltpu.VMEM((nb, c + SUBLANES, v_dim), F32),
                        pltpu.VMEM((nb, v_heads, GDN_HEAD, GDN_HEAD), F32)],
        compiler_params=_cparams("arbitrary", "arbitrary"),
        name="gdn_scan",
    )(proj3, proj3, proj3, proj3, proj3, cwq, cwk, cwv, dtb, alog, nw)
    return out.reshape(t, v_dim)


def _first_index(mask, n):
    idx = lax.broadcasted_iota(I32, mask.shape, 0)
    return jnp.min(jnp.where(mask, idx, n), axis=0, keepdims=True)


def _router_kernel(h_ref, nw_ref, wr_ref, br_ref, upper_ref,
                   eid_ref, rank_ref, gate_ref, cnt_ref, base_ref):
    tq = h_ref.shape[0]
    i = pl.program_id(0)

    @pl.when(i == 0)
    def _():
        base_ref[...] = jnp.zeros(base_ref.shape, F32)

    xn = _rms(h_ref[...], nw_ref[...])
    x_hi, x_lo = _split(xn)
    w_hi, w_lo = wr_ref[0], wr_ref[1]
    logits = (_dot_nt(jnp.concatenate([w_hi, w_lo], axis=0), x_hi)
              + jnp.concatenate([_dot_nt(w_hi, x_lo), jnp.zeros((LANES, tq), F32)], axis=0))
    logits = logits[:LANES] + logits[LANES:] + br_ref[:, 0:1]
    gl = logits[0:N_EGROUPS, :]
    gmax = jnp.max(gl, axis=0, keepdims=True)
    g_sel = _first_index(gl == gmax, N_EGROUPS)
    p_sel = 1.0 / jnp.sum(jnp.exp(gl - gmax), axis=0, keepdims=True)
    e_in = logits[N_EGROUPS:N_EGROUPS + EXPERTS_PER_GROUP, :]
    for g in range(1, N_EGROUPS):
        lo = N_EGROUPS + g * EXPERTS_PER_GROUP
        e_in = jnp.where(g_sel == g, logits[lo:lo + EXPERTS_PER_GROUP, :], e_in)
    m1 = jnp.max(e_in, axis=0, keepdims=True)
    i1 = _first_index(e_in == m1, EXPERTS_PER_GROUP)
    sub = lax.broadcasted_iota(I32, e_in.shape, 0)
    rest = jnp.where(sub == i1, -jnp.inf, e_in)
    m2 = jnp.max(rest, axis=0, keepdims=True)
    i2 = _first_index(rest == m2, EXPERTS_PER_GROUP)
    e2 = jnp.exp(m2 - m1)
    denom = 1.0 + e2
    gate_ref[0:1, :] = (1.0 / denom) * p_sel
    gate_ref[1:2, :] = (e2 / denom) * p_sel
    eid0 = g_sel * EXPERTS_PER_GROUP + i1
    eid1 = g_sel * EXPERTS_PER_GROUP + i2

    erow = lax.broadcasted_iota(I32, (N_EXPERTS, tq), 0)
    oh0 = erow == eid0
    oh1 = erow == eid1
    oh0f = oh0.astype(F32)
    oh1f = oh1.astype(F32)
    cum0 = jnp.dot(oh0f.astype(BF16), upper_ref[...], preferred_element_type=F32)
    cum1 = jnp.dot(oh1f.astype(BF16), upper_ref[...], preferred_element_type=F32)
    base = base_ref[:, 0:1]
    tot0 = jnp.sum(oh0f, axis=1, keepdims=True)
    tot1 = jnp.sum(oh1f, axis=1, keepdims=True)
    r0 = jnp.sum(jnp.where(oh0, base + cum0, 0.0), axis=0, keepdims=True)
    r1 = jnp.sum(jnp.where(oh1, base + tot0 + cum1, 0.0), axis=0, keepdims=True)
    sub = tq // LANES
    for k, (e, r) in enumerate(((eid0, r0.astype(I32)), (eid1, r1.astype(I32)))):
        for rr in range(sub):
            row = k * sub + rr
            eid_ref[0, row:row + 1, :] = e[:, rr * LANES:(rr + 1) * LANES]
            rank_ref[0, row:row + 1, :] = r[:, rr * LANES:(rr + 1) * LANES]
    new_base = base + tot0 + tot1
    base_ref[...] = jnp.broadcast_to(new_base, base_ref.shape)
    cnt_ref[...] = jnp.broadcast_to(new_base, cnt_ref.shape).astype(I32)


def moe_router(h, nw, w_group, b_group, w_expert, b_expert):
    t, d = h.shape
    tq = ROUTE_TILE
    nr = N_EGROUPS + N_EXPERTS
    wr = jnp.pad(jnp.concatenate([w_group, w_expert], axis=1).T, ((0, LANES - nr), (0, 0)))
    br = jnp.pad(jnp.concatenate([b_group, b_expert]), (0, LANES - nr))
    br = jnp.broadcast_to(br[:, None], (LANES, LANES))
    upper = (jnp.arange(tq)[:, None] < jnp.arange(tq)[None, :]).astype(BF16)
    rows = 2 * tq // LANES
    tile_spec = pl.BlockSpec((1, rows, LANES), lambda i: (i, 0, 0))
    tile_shape = jax.ShapeDtypeStruct((t // tq, rows, LANES), I32)
    eid, rank, gate, cnt = pl.pallas_call(
        _router_kernel,
        grid=(t // tq,),
        in_specs=[pl.BlockSpec((tq, d), lambda i: (i, 0)),
                  pl.BlockSpec((1, d), lambda i: (0, 0)),
                  pl.BlockSpec((2, LANES, d), lambda i: (0, 0, 0)),
                  pl.BlockSpec((LANES, LANES), lambda i: (0, 0)),
                  pl.BlockSpec((tq, tq), lambda i: (0, 0))],
        out_specs=[tile_spec, tile_spec, pl.BlockSpec((2, tq), lambda i: (0, i)),
                   pl.BlockSpec((N_EXPERTS, LANES), lambda i: (0, 0))],
        out_shape=[tile_shape, tile_shape, jax.ShapeDtypeStruct((2, t), F32),
                   jax.ShapeDtypeStruct((N_EXPERTS, LANES), I32)],
        scratch_shapes=[pltpu.VMEM((N_EXPERTS, LANES), F32)],
        compiler_params=_cparams("arbitrary"),
        name="moe_router",
    )(h, nw.reshape(1, d), jnp.stack(_split(wr)), br, upper)
    return eid, rank, gate, cnt[:, 0]


def _slot_kernel(pstart_ref, eid_ref, rank_ref, pos_ref):
    eid = eid_ref[...]

    def body(e, acc):
        return jnp.where(eid == e, pstart_ref[e], acc)

    pos_ref[...] = rank_ref[...] + lax.fori_loop(0, N_EXPERTS, body, jnp.zeros(eid.shape, I32))


def moe_slots(pstart, eid_tiles, rank_tiles):
    nt, rows, lanes = eid_tiles.shape
    tiles_per_step = math.gcd(nt, 16)
    blk = pl.BlockSpec((tiles_per_step, rows, lanes), lambda i, ps: (i, 0, 0))
    return pl.pallas_call(
        _slot_kernel,
        grid_spec=pltpu.PrefetchScalarGridSpec(num_scalar_prefetch=1, grid=(nt // tiles_per_step,),
                                               in_specs=[blk, blk], out_specs=blk),
        out_shape=jax.ShapeDtypeStruct(eid_tiles.shape, I32),
        compiler_params=_cparams("arbitrary"),
        name="moe_slots",
    )(pstart, eid_tiles, rank_tiles)


def _row_copy(src, src_row, dst, dst_row, sem):
    return pltpu.make_async_copy(src.at[pl.ds(src_row, 1)], dst.at[pl.ds(dst_row, 1)], sem)


def _dispatch_kernel(lo_ref, hi_ref, nu_ref, pos_hbm, h_ref, nw_ref, xb_out, pos_smem, zero_ref,
                     xn_ref, sem_idx, sem_rows, sem_fill, *, n_tiles):
    tq = h_ref.shape[0]
    sub = tq // LANES
    i = pl.program_id(0)
    n_blocks = xb_out.shape[0] // MOE_BLOCK

    def zero_rows(start, size, wait):
        cp = pltpu.make_async_copy(zero_ref.at[pl.ds(0, size)], xb_out.at[pl.ds(start, size)],
                                   sem_fill)
        cp.wait() if wait else cp.start()

    def fill_padding(e, wait):
        lo, hi = lo_ref[e], hi_ref[e]
        lo8 = jnp.minimum((lo + (SUBLANES - 1)) & -SUBLANES, hi)
        for u in range(SUBLANES - 1):
            @pl.when(lo + u < lo8)
            def _(u=u):
                zero_rows(lo + u, 1, wait)
        cur = lo8
        size = MOE_BLOCK // 2
        while size >= SUBLANES:
            take = ((hi - lo8) & size) != 0

            @pl.when(take)
            def _(cur=cur, size=size):
                zero_rows(pl.multiple_of(cur, SUBLANES), size, wait)
            cur = cur + jnp.where(take, size, 0)
            size //= 2

    def fill_tail(blk, wait):
        zero_rows(pl.multiple_of(blk * MOE_BLOCK, MOE_BLOCK), MOE_BLOCK, wait)

    @pl.when(i == 0)
    def _():
        zero_ref[...] = jnp.zeros(zero_ref.shape, F32)
        for wait in (False, True):
            def per_expert(e, carry, wait=wait):
                fill_padding(e, wait)
                return carry

            def per_block(blk, carry, wait=wait):
                fill_tail(blk, wait)
                return carry

            lax.fori_loop(0, N_EXPERTS, per_expert, 0)
            lax.fori_loop(nu_ref[0], n_blocks, per_block, 0)

    n = n_tiles

    def pos_copy(tile, slot):
        return pltpu.make_async_copy(pos_hbm.at[tile], pos_smem.at[slot], sem_idx)

    def wait_rows(slot):
        for _ in range(2):
            pltpu.make_async_copy(xn_ref.at[slot], xb_out.at[pl.ds(0, tq)], sem_rows.at[slot]).wait()

    @pl.when(i == 0)
    def _():
        pos_copy(0, 0).start()

    for slot in range(2):
        @pl.when(i % 2 == slot)
        def _(slot=slot):
            pos_copy(i, slot).wait()
            pos_copy(jnp.minimum(i + 1, n - 1), 1 - slot).start()
            xn_ref[slot] = _rms(h_ref[...], nw_ref[...])
            for rr in range(sub):
                for cc in range(LANES):
                    for k in range(2):
                        _row_copy(xn_ref.at[slot], rr * LANES + cc, xb_out,
                                  pos_smem[slot, k * sub + rr, cc], sem_rows.at[slot]).start()

            @pl.when(i > 0)
            def _():
                wait_rows(1 - slot)

    @pl.when(i == n - 1)
    def _():
        wait_rows((n - 1) % 2)
        pos_copy(i, 1 - (n - 1) % 2).wait()


def moe_dispatch(h, nw, pos_tiles, pad_lo, pad_hi, n_used, n_pad):
    t, d = h.shape
    tq = ROUTE_TILE
    grid_spec = pltpu.PrefetchScalarGridSpec(
        num_scalar_prefetch=3,
        grid=(t // tq,),
        in_specs=[pl.BlockSpec(memory_space=pl.ANY),
                  pl.BlockSpec((tq, d), lambda i, lo, hi, nu: (i, 0)),
                  pl.BlockSpec((1, d), lambda i, lo, hi, nu: (0, 0))],
        out_specs=pl.BlockSpec(memory_space=pl.ANY),
        scratch_shapes=[pltpu.SMEM((2, 2 * tq // LANES, LANES), I32),
                        pltpu.VMEM((MOE_BLOCK, d), F32),
                        pltpu.VMEM((2, tq, d), F32),
                        pltpu.SemaphoreType.DMA, pltpu.SemaphoreType.DMA((2,)),
                        pltpu.SemaphoreType.DMA],
    )
    return pl.pallas_call(
        functools.partial(_dispatch_kernel, n_tiles=t // tq),
        grid_spec=grid_spec,
        out_shape=jax.ShapeDtypeStruct((n_pad, d), F32),
        compiler_params=_cparams("arbitrary"),
        name="moe_dispatch",
    )(pad_lo, pad_hi, n_used, pos_tiles, h, nw.reshape(1, d))


def _expert_kernel(be_ref, nu_ref, x_ref, w1_ref, w3_ref, w2_ref, o_ref,
                   w1b_ref, w3b_ref, w2b_ref):
    b = pl.program_id(0)
    used = b < nu_ref[0]
    new_expert = (b == 0) | (be_ref[b] != be_ref[jnp.maximum(b - 1, 0)])

    @pl.when(used & new_expert)
    def _():
        w1b_ref[...] = w1_ref[0].astype(BF16)
        w3b_ref[...] = w3_ref[0].astype(BF16)
        w2b_ref[...] = w2_ref[0].astype(BF16)

    @pl.when(used)
    def _():
        xb = x_ref[...].astype(BF16)
        h1 = jnp.dot(xb, w1b_ref[...], preferred_element_type=F32)
        h3 = jnp.dot(xb, w3b_ref[...], preferred_element_type=F32)
        hid = (_silu(h1) * h3).astype(BF16)
        o_ref[...] = jnp.dot(hid, w2b_ref[...], preferred_element_type=F32)

    @pl.when(b >= nu_ref[0])
    def _():
        o_ref[...] = jnp.zeros(o_ref.shape, F32)


def moe_experts(xb, w1, w3, w2, blk_expert, n_used):
    n_pad, d = xb.shape
    de = w1.shape[2]
    nb = n_pad // MOE_BLOCK
    grid_spec = pltpu.PrefetchScalarGridSpec(
        num_scalar_prefetch=2,
        grid=(nb,),
        in_specs=[pl.BlockSpec((MOE_BLOCK, d), lambda b, be, nu: (jnp.minimum(b, nu[0] - 1), 0)),
                  pl.BlockSpec((1, d, de), lambda b, be, nu: (be[b], 0, 0)),
                  pl.BlockSpec((1, d, de), lambda b, be, nu: (be[b], 0, 0)),
                  pl.BlockSpec((1, de, d), lambda b, be, nu: (be[b], 0, 0))],
        out_specs=pl.BlockSpec((MOE_BLOCK, d), lambda b, be, nu: (b, 0)),
        scratch_shapes=[pltpu.VMEM((d, de), BF16), pltpu.VMEM((d, de), BF16),
                        pltpu.VMEM((de, d), BF16)],
    )
    return pl.pallas_call(
        _expert_kernel,
        grid_spec=grid_spec,
        out_shape=jax.ShapeDtypeStruct((n_pad, d), F32),
        compiler_params=_cparams("arbitrary"),
        name="moe_experts",
    )(blk_expert, n_used, xb, w1, w3, w2)


def _combine_ple_kernel(pos_hbm, yb_hbm, h_ref, gate_ref, p_ref, nw_ref, wg_ref, wp_ref, fw_ref,
                        o_ref, pos_smem, buf_ref, sem_idx, sem_rows, *, final, n_tiles):
    tq = h_ref.shape[0]
    sub = tq // LANES
    i = pl.program_id(0)
    n = n_tiles

    def pos_copy(tile, slot):
        return pltpu.make_async_copy(pos_hbm.at[tile], pos_smem.at[slot], sem_idx)

    def start_row(slot, k, rr, cc):
        _row_copy(yb_hbm, pos_smem[slot, k * sub + rr, cc], buf_ref.at[slot, k], rr * LANES + cc,
                  sem_rows.at[slot]).start()

    def wait_rows(slot):
        for k in range(2):
            pltpu.make_async_copy(yb_hbm.at[pl.ds(0, tq)], buf_ref.at[slot, k],
                                  sem_rows.at[slot]).wait()

    def combine(slot):
        gate = gate_ref[...]
        h2 = h_ref[...] + gate[:, 0:1] * buf_ref[slot, 0] + gate[:, 1:2] * buf_ref[slot, 1]
        hn = _rms(h2, nw_ref[...]).astype(BF16)
        pg = jax.nn.sigmoid(jnp.dot(hn, wg_ref[...], preferred_element_type=F32))
        pp = jnp.dot(p_ref[...].astype(BF16), wp_ref[...], preferred_element_type=F32)
        out = h2 + pg * pp
        if final:
            out = _rms(out, fw_ref[...])
        o_ref[...] = out

    @pl.when(i == 0)
    def _():
        first = pos_copy(0, 0)
        first.start()
        first.wait()
        for rr in range(sub):
            def body(cc, carry, rr=rr):
                for k in range(2):
                    start_row(0, k, rr, cc)
                return carry
            lax.fori_loop(0, LANES, body, 0)
        pos_copy(min(1, n - 1), 1).start()

    nxt = jnp.minimum(i + 1, n - 1)
    for slot in range(2):
        @pl.when(i % 2 == slot)
        def _(slot=slot):
            pos_copy(nxt, 1 - slot).wait()
            wait_rows(slot)
            pos_copy(jnp.minimum(i + 2, n - 1), slot).start()
            for rr in range(sub):
                for cc in range(LANES):
                    for k in range(2):
                        start_row(1 - slot, k, rr, cc)
            combine(slot)

    @pl.when(i == n - 1)
    def _():
        wait_rows(1 - (n - 1) % 2)
        pos_copy(i, (n - 1) % 2).wait()


def moe_combine_ple(h, yb, pos_tiles, gate_cols, p, nw, wg, wp, fw, *, final):
    t, d = h.shape
    tq = ROUTE_TILE
    pd = p.shape[1]
    return pl.pallas_call(
        functools.partial(_combine_ple_kernel, final=final, n_tiles=t // tq),
        grid=(t // tq,),
        in_specs=[pl.BlockSpec(memory_space=pl.ANY),
                  pl.BlockSpec(memory_space=pl.ANY),
                  pl.BlockSpec((tq, d), lambda i: (i, 0)),
                  pl.BlockSpec((tq, 2), lambda i: (i, 0)),
                  pl.BlockSpec((tq, pd), lambda i: (i, 0)),
                  pl.BlockSpec((1, d), lambda i: (0, 0)),
                  pl.BlockSpec((d, d), lambda i: (0, 0)),
                  pl.BlockSpec((pd, d), lambda i: (0, 0)),
                  pl.BlockSpec((1, d), lambda i: (0, 0))],
        out_specs=pl.BlockSpec((tq, d), lambda i: (i, 0)),
        out_shape=jax.ShapeDtypeStruct((t, d), F32),
        scratch_shapes=[pltpu.SMEM((2, 2 * tq // LANES, LANES), I32),
                        pltpu.VMEM((2, 2, tq, d), F32),
                        pltpu.SemaphoreType.DMA, pltpu.SemaphoreType.DMA((2,))],
        compiler_params=_cparams("arbitrary"),
        name="moe_combine_ple",
    )(pos_tiles, yb, h, gate_cols, p, nw.reshape(1, d), wg, wp, fw.reshape(1, d))


def moe_ple_layer(h, p, norm_moe, w_group, b_group, w_expert, b_expert, w1, w3, w2,
                  norm_ple, wg, wp, fw, *, layer, final):
    t, d = h.shape
    eid_tiles, rank_tiles, gate, counts = moe_router(h, norm_moe, w_group, b_group, w_expert,
                                                     b_expert)
    padded = (counts + MOE_BLOCK - 1) // MOE_BLOCK * MOE_BLOCK
    pend = jnp.cumsum(padded)
    pstart = pend - padded
    na = 2 * t
    n_pad = (na + MOE_BLOCK - 1) // MOE_BLOCK * MOE_BLOCK + N_EXPERTS * MOE_BLOCK
    nb = n_pad // MOE_BLOCK
    blk_start = jnp.arange(nb, dtype=I32) * MOE_BLOCK
    blk_expert = jnp.minimum(jnp.sum(blk_start[:, None] >= pend[None, :], axis=-1),
                             N_EXPERTS - 1).astype(I32)
    n_used = (pend[-1:] // MOE_BLOCK).astype(I32)
    pos_tiles = moe_slots(pstart.astype(I32), eid_tiles, rank_tiles)
    xb = moe_dispatch(h, norm_moe, pos_tiles, (pstart + counts).astype(I32), pend.astype(I32),
                      n_used, n_pad)
    yb = moe_experts(xb, w1, w3, w2, blk_expert + layer * N_EXPERTS, n_used)
    return moe_combine_ple(h, yb, pos_tiles, gate.T, p, norm_ple, wg.astype(BF16),
                           wp.astype(BF16), fw, final=final)


def _pad_cols(w, n):
    return jnp.pad(w, ((0, 0), (0, n - w.shape[1])))


def kernel(x, p, norm_mix, norm_moe, norm_ple, final_norm, m_in_w, m_conv_w, m_conv_b, m_dt_bias, m_A_log, m_D, m_norm_w, m_out_w, g_in_w, g_conv_w, g_dt_bias, g_A_log, g_norm_w, g_out_w, moe_w_group, moe_b_group, moe_w_expert, moe_b_expert, moe_w1, moe_w3, moe_w2, ple_w_proj, ple_w_gate):
    batch, seq, d = x.shape
    t = batch * seq
    depth = p.shape[0]
    pd = p.shape[-1]
    h = x.reshape(t, d)
    p2 = p.reshape(depth, t, pd)
    w1_all = moe_w1.reshape((-1,) + moe_w1.shape[2:])
    w3_all = moe_w3.reshape((-1,) + moe_w3.shape[2:])
    w2_all = moe_w2.reshape((-1,) + moe_w2.shape[2:])
    for i in range(depth):
        j = i // 2
        if i % 2 == 0:
            inner = m_out_w.shape[1]
            heads = m_dt_bias.shape[1]
            conv_dim = m_conv_w.shape[2]
            w_in = _pad_cols(m_in_w[j], inner + conv_dim + LANES).astype(BF16)
            proj = norm_matmul(h, norm_mix[i], w_in)
            y = ssd_mixer_core(proj, m_conv_w[j], m_conv_b[j], m_dt_bias[j], m_A_log[j], m_D[j],
                               m_norm_w[j], batch=batch, seq=seq, inner=inner, heads=heads)
            h = matmul_residual(y, m_out_w[j].astype(BF16), h)
        else:
            v_dim = g_out_w.shape[1]
            v_heads = g_dt_bias.shape[1]
            conv_dim = g_conv_w.shape[2]
            qk_dim = (conv_dim - v_dim) // 2
            w_in = _pad_cols(g_in_w[j], conv_dim + v_dim + LANES).astype(BF16)
            proj = norm_matmul(h, norm_mix[i], w_in)
            y = gdn_mixer_core(proj, g_conv_w[j], g_dt_bias[j], g_A_log[j], g_norm_w[j],
                               batch=batch, seq=seq, qk_dim=qk_dim, v_dim=v_dim, v_heads=v_heads)
            h = matmul_residual(y, g_out_w[j].astype(BF16), h)
        h = moe_ple_layer(h, p2[i], norm_moe[i], moe_w_group[i], moe_b_group[i], moe_w_expert[i],
                          moe_b_expert[i], w1_all, w3_all, w2_all, norm_ple[i],
                          ple_w_gate[i], ple_w_proj[i], final_norm, layer=i,
                          final=(i == depth - 1))
    return h.reshape(batch, seq, d)
```
